```python
import math
import jax
import jax.numpy as jnp
from jax import lax
import numpy as np

D_MODEL = 2048
BATCH = 4
SEQ = 2048
DEPTH = 4

GRID_W = 64
CTX_LEN = 256
N_EVEN = (DEPTH + 1) // 2
N_ODD = DEPTH // 2
EPS = 1e-6

BRANCH = D_MODEL // 2
S5_GROUP = 16
S5_GROUPS = BRANCH // S5_GROUP
S5_STATE = 64
S5_DT_MIN = 1e-3
S5_DT_MAX = 1e-1
DA_HEAD = 64
DA_HEADS = BRANCH // (2 * DA_HEAD)
DA_VDIM = 2 * DA_HEAD
Q_BLOCK = 128
ROPE_BASE = 10000.0
GLA_HEADS = 4
GLA_KEY = D_MODEL // 2
GLA_VAL = D_MODEL
GLA_DK = GLA_KEY // GLA_HEADS
GLA_DV = GLA_VAL // GLA_HEADS
GLA_RANK = 16
GLA_TAU = 16.0
GLA_CHUNK = 64

kernel_name = "hybrid_s5_diffattn_gla_prefix_dit"


def rms_norm(x, g):
    xf = x.astype(jnp.float32)
    xf = xf * lax.rsqrt(jnp.mean(xf * xf, axis=-1, keepdims=True) + EPS)
    return (xf * g.astype(jnp.float32)).astype(x.dtype)


def modulation(cond, w, b, n_chunks):
    width = n_chunks * D_MODEL
    m = jax.nn.silu(cond) @ w[:, :width] + b[:width]
    return jnp.split(m, n_chunks, axis=-1)


def modulate(x, g, shift, scale):
    return rms_norm(x, g) * (1 + scale) + shift


def _flip(t, axis, on):
    return jnp.flip(t, axis=axis) if on else t


def axial_rope(n_tokens):
    rows = n_tokens // GRID_W
    row = jnp.repeat(jnp.arange(rows, dtype=jnp.float32), GRID_W)
    col = jnp.tile(jnp.arange(GRID_W, dtype=jnp.float32), rows)
    n_freq = DA_HEAD // 4
    inv_freq = ROPE_BASE ** (-jnp.arange(n_freq, dtype=jnp.float32) / n_freq)
    ang_r = row[:, None] * inv_freq
    ang_c = col[:, None] * inv_freq
    ang = jnp.concatenate([ang_r, ang_r, ang_c, ang_c], axis=-1)
    return jnp.cos(ang), jnp.sin(ang)


def apply_rope(x, cos, sin):
    a, b, c_, d = jnp.split(x, 4, axis=-1)
    rot = jnp.concatenate([-b, a, -d, c_], axis=-1)
    cos = cos[:, None, None, :].astype(x.dtype)
    sin = sin[:, None, None, :].astype(x.dtype)
    return x * cos + rot * sin


def _diff_block(qb, keys, vals, lam):
    s = jnp.einsum("bqhme,bkhme->bhmqk", qb, keys).astype(jnp.float32) * (DA_HEAD ** -0.5)
    p = jax.nn.softmax(s, axis=-1)
    w = p[:, :, 0] - lam * p[:, :, 1]
    return jnp.einsum("bhqk,bkhv->bqhv", w.astype(vals.dtype), vals)


def diff_attention(q_lat, k_lat, v_lat, q_ctx, k_ctx, v_ctx, qn_g, kn_g, lam_vecs, subln_g, lam_init):
    B, L, _ = q_lat.shape

    def heads_qk(t, g):
        return rms_norm(t.reshape(t.shape[0], t.shape[1], DA_HEADS, 2, DA_HEAD), g)

    def heads_v(t):
        return t.reshape(t.shape[0], t.shape[1], DA_HEADS, DA_VDIM)

    cos, sin = axial_rope(L)
    q = apply_rope(heads_qk(q_lat, qn_g), cos, sin)
    k = apply_rope(heads_qk(k_lat, kn_g), cos, sin)
    kc = heads_qk(k_ctx, kn_g)
    vc = heads_v(v_ctx)
    keys = jnp.concatenate([kc, k], axis=1)
    vals = jnp.concatenate([vc, heads_v(v_lat)], axis=1)
    lv = lam_vecs.astype(jnp.float32)
    lam = jnp.exp(jnp.sum(lv[0] * lv[1])) - jnp.exp(jnp.sum(lv[2] * lv[3])) + lam_init
    nb = L // Q_BLOCK
    qb = q.reshape(B, nb, Q_BLOCK, DA_HEADS, 2, DA_HEAD).swapaxes(0, 1)
    o = lax.map(lambda blk: _diff_block(blk, keys, vals, lam), qb)
    o = o.swapaxes(0, 1).reshape(B, L, DA_HEADS, DA_VDIM)

    def finish(t):
        t = rms_norm(t, subln_g) * (1 - lam_init)
        return t.reshape(t.shape[0], t.shape[1], BRANCH)

    y_lat = finish(o)
    y_ctx = None
    if q_ctx is not None:
        qc = heads_qk(q_ctx, qn_g)
        y_ctx = finish(_diff_block(qc, kc, vc, lam))
    return y_lat, y_ctx


def s5_discretize(a_re, a_im, log_dt, b_re, b_im):
    a_re, a_im = a_re.astype(jnp.float32), a_im.astype(jnp.float32)
    b_re, b_im = b_re.astype(jnp.float32), b_im.astype(jnp.float32)
    dt = jnp.exp(log_dt.astype(jnp.float32))[:, None]
    mag = jnp.exp(a_re * dt)
    ang = a_im * dt
    lb_re, lb_im = mag * jnp.cos(ang), mag * jnp.sin(ang)
    nr, ni = lb_re - 1.0, lb_im
    den = a_re * a_re + a_im * a_im
    f_re = (nr * a_re + ni * a_im) / den
    f_im = (ni * a_re - nr * a_im) / den
    bb_re = f_re[..., None] * b_re - f_im[..., None] * b_im
    bb_im = f_re[..., None] * b_im + f_im[..., None] * b_re
    return lb_re, lb_im, bb_re, bb_im


def _s5_combine(e1, e2):
    ar1, ai1, br1, bi1 = e1
    ar2, ai2, br2, bi2 = e2
    return (ar1 * ar2 - ai1 * ai2, ar1 * ai2 + ai1 * ar2,
            ar2 * br1 - ai2 * bi1 + br2, ar2 * bi1 + ai2 * br1 + bi2)


def s5_scan(u, h0_re, h0_im, lb_re, lb_im, bb_re, bb_im):
    bu_re = jnp.einsum("bngc,gpc->bngp", u, bb_re)
    bu_im = jnp.einsum("bngc,gpc->bngp", u, bb_im)
    bu_re = bu_re.at[:, 0].add(lb_re * h0_re - lb_im * h0_im)
    bu_im = bu_im.at[:, 0].add(lb_re * h0_im + lb_im * h0_re)
    a_re = jnp.broadcast_to(lb_re, bu_re.shape)
    a_im = jnp.broadcast_to(lb_im, bu_im.shape)
    _, _, h_re, h_im = lax.associative_scan(_s5_combine, (a_re, a_im, bu_re, bu_im), axis=1)
    return h_re, h_im


def s5_readout(h_re, h_im, c_re, c_im):
    return jnp.einsum("bngp,gcp->bngc", h_re, c_re) - jnp.einsum("bngp,gcp->bngc", h_im, c_im)


def s5_mixer(u_lat, u_ctx, a_re, a_im, log_dt, b_re, b_im, c_re, c_im, d_skip, w_glu, with_ctx_out):
    dtype = u_lat.dtype

    def grp(u):
        return u.astype(jnp.float32).reshape(u.shape[0], u.shape[1], S5_GROUPS, S5_GROUP)

    ul, uc = grp(u_lat), grp(u_ctx)
    zero = jnp.zeros((ul.shape[0], S5_GROUPS, S5_STATE), jnp.float32)
    ys_lat, ys_ctx = [], []
    for d in range(2):
        rev = d == 1
        lb_re, lb_im, bb_re, bb_im = s5_discretize(a_re[d], a_im[d], log_dt[d], b_re[d], b_im[d])
        cr, ci = c_re[d].astype(jnp.float32), c_im[d].astype(jnp.float32)
        hc_re, hc_im = s5_scan(_flip(uc, 1, rev), zero, zero, lb_re, lb_im, bb_re, bb_im)
        hl_re, hl_im = s5_scan(_flip(ul, 1, rev), hc_re[:, -1], hc_im[:, -1], lb_re, lb_im, bb_re, bb_im)
        ys_lat.append(_flip(s5_readout(hl_re, hl_im, cr, ci), 1, rev))
        if with_ctx_out:
            ys_ctx.append(_flip(s5_readout(hc_re, hc_im, cr, ci), 1, rev))
    dg = d_skip.astype(jnp.float32).reshape(S5_GROUPS, S5_GROUP)
    wg = w_glu.astype(jnp.float32)

    def finish(y, u):
        y = (y + dg * u).reshape(u.shape[0], u.shape[1], BRANCH)
        g = jax.nn.gelu(y)
        return (g * jax.nn.sigmoid(g @ wg)).astype(dtype)

    y_lat = finish(ys_lat[0] + ys_lat[1], ul)
    y_ctx = finish(ys_ctx[0] + ys_ctx[1], uc) if with_ctx_out else None
    return y_lat, y_ctx


def gla_chunk_scan(q, k, v, log_a, s0):
    B, H, N, _ = k.shape
    nc = N // GLA_CHUNK

    def chunks(t):
        return t.reshape(B, H, nc, GLA_CHUNK, t.shape[-1]).transpose(2, 0, 1, 3, 4)

    upto = jnp.tril(jnp.ones((GLA_CHUNK, GLA_CHUNK), dtype=bool))

    def step(s, inp):
        kk, vv, aa = inp[0], inp[1], inp[2]
        b = jnp.cumsum(aa, axis=2)
        b_end = b[:, :, -1:, :]
        s_new = (jnp.exp(b_end[:, :, 0, :, None]) * s
                 + jnp.einsum("bhsd,bhsv->bhdv", kk * jnp.exp(b_end - b), vv))
        if q is None:
            return s_new, None
        q_dec = inp[3] * jnp.exp(b)
        scores = jnp.where(upto, jnp.einsum("bhtd,bhsd->bhts", q_dec, kk * jnp.exp(-b)), 0.0)
        o = jnp.einsum("bhts,bhsv->bhtv", scores, vv) + jnp.einsum("bhtd,bhdv->bhtv", q_dec, s)
        return s_new, o

    xs = [chunks(k), chunks(v), chunks(log_a)]
    if q is not None:
        xs.append(chunks(q))
    s_fin, o = lax.scan(step, s0, tuple(xs))
    if q is not None:
        o = o.transpose(1, 2, 0, 3, 4).reshape(B, H, N, v.shape[-1])
    return o, s_fin


def _gla_heads(t, dh):
    return t.astype(jnp.float32).reshape(t.shape[0], t.shape[1], GLA_HEADS, dh).transpose(0, 2, 1, 3)


def _gla_log_decay(h, wa1, wa2, ba):
    logits = ((h @ wa1) @ wa2 + ba).astype(jnp.float32)
    return _gla_heads(jax.nn.log_sigmoid(logits), GLA_DK) / GLA_TAU


def _gla_out(o, g, dtype):
    o = rms_norm(o.transpose(0, 2, 1, 3), g)
    return o.reshape(o.shape[0], o.shape[1], GLA_VAL).astype(dtype)


def gla_mixer(h_lat, h_ctx, q_lat, k_lat, v_lat, q_ctx, k_ctx, v_ctx, wa1, wa2, ba, norm_g):
    dtype = v_lat.dtype
    scale = GLA_DK ** -0.5
    ql = _gla_heads(q_lat, GLA_DK) * scale
    kl, vl = _gla_heads(k_lat, GLA_DK), _gla_heads(v_lat, GLA_DV)
    kc, vc = _gla_heads(k_ctx, GLA_DK), _gla_heads(v_ctx, GLA_DV)
    qc = None if q_ctx is None else _gla_heads(q_ctx, GLA_DK) * scale
    s0 = jnp.zeros((kl.shape[0], GLA_HEADS, GLA_DK, GLA_DV), jnp.float32)
    outs_lat, outs_ctx = [], []
    for d in range(2):
        rev = d == 1
        al = _flip(_gla_log_decay(h_lat, wa1[d], wa2[d], ba[d]), 2, rev)
        ac = _flip(_gla_log_decay(h_ctx, wa1[d], wa2[d], ba[d]), 2, rev)
        oc, s_ctx = gla_chunk_scan(None if qc is None else _flip(qc, 2, rev),
                                   _flip(kc, 2, rev), _flip(vc, 2, rev), ac, s0)
        ol, _ = gla_chunk_scan(_flip(ql, 2, rev), _flip(kl, 2, rev), _flip(vl, 2, rev), al, s_ctx)
        outs_lat.append(_flip(ol, 2, rev))
        if qc is not None:
            outs_ctx.append(_flip(oc, 2, rev))
    y_lat = _gla_out(outs_lat[0] + outs_lat[1], norm_g, dtype)
    y_ctx = None if qc is None else _gla_out(outs_ctx[0] + outs_ctx[1], norm_g, dtype)
    return y_lat, y_ctx


def _merge_even(s5_out, zs, da_out, zd, w_out):
    return jnp.concatenate([s5_out * jax.nn.silu(zs), da_out * jax.nn.silu(zd)], axis=-1) @ w_out


def even_layer(x_lat, x_ctx, c, c_ctx, ada_w, ada_b, norm_g, w_in, w_out,
               a_re, a_im, log_dt, b_re, b_im, c_re, c_im, d_skip, w_glu,
               qn_g, kn_g, lam_vecs, subln_g, lam_init, with_ctx_out):
    shift, scale, gate = modulation(c[:, None, :], ada_w, ada_b, 3)
    h_lat = modulate(x_lat, norm_g, shift, scale)
    u_l, zs_l, q_l, k_l, v_l, zd_l = jnp.split(h_lat @ w_in, 6, axis=-1)
    mod_c = modulation(c_ctx, ada_w, ada_b, 3 if with_ctx_out else 2)
    h_ctx = modulate(x_ctx, norm_g, mod_c[0], mod_c[1])
    if with_ctx_out:
        u_c, zs_c, q_c, k_c, v_c, zd_c = jnp.split(h_ctx @ w_in, 6, axis=-1)
    else:
        q_c = None
        u_c = h_ctx @ w_in[:, 0:BRANCH]
        k_c = h_ctx @ w_in[:, 3 * BRANCH:4 * BRANCH]
        v_c = h_ctx @ w_in[:, 4 * BRANCH:5 * BRANCH]
    s5_l, s5_c = s5_mixer(u_l, u_c, a_re, a_im, log_dt, b_re, b_im, c_re, c_im, d_skip, w_glu, with_ctx_out)
    da_l, da_c = diff_attention(q_l, k_l, v_l, q_c, k_c, v_c, qn_g, kn_g, lam_vecs, subln_g, lam_init)
    x_lat = x_lat + gate * _merge_even(s5_l, zs_l, da_l, zd_l, w_out)
    if with_ctx_out:
        x_ctx = x_ctx + mod_c[2] * _merge_even(s5_c, zs_c, da_c, zd_c, w_out)
    return x_lat, x_ctx


def odd_layer(x_lat, x_ctx, c, c_ctx, ada_w, ada_b, norm_g, w_in, w_out,
              wa1, wa2, ba, gla_norm_g, with_ctx_out):
    shift, scale, gate = modulation(c[:, None, :], ada_w, ada_b, 3)
    h_lat = modulate(x_lat, norm_g, shift, scale)
    cuts = (GLA_KEY, 2 * GLA_KEY, 2 * GLA_KEY + GLA_VAL)
    q_l, k_l, v_l, z_l = jnp.split(h_lat @ w_in, cuts, axis=-1)
    mod_c = modulation(c_ctx, ada_w, ada_b, 3 if with_ctx_out else 2)
    h_ctx = modulate(x_ctx, norm_g, mod_c[0], mod_c[1])
    if with_ctx_out:
        q_c, k_c, v_c, z_c = jnp.split(h_ctx @ w_in, cuts, axis=-1)
    else:
        q_c = None
        k_c = h_ctx @ w_in[:, cuts[0]:cuts[1]]
        v_c = h_ctx @ w_in[:, cuts[1]:cuts[2]]
    y_l, y_c = gla_mixer(h_lat, h_ctx, q_l, k_l, v_l, q_c, k_c, v_c, wa1, wa2, ba, gla_norm_g)
    x_lat = x_lat + gate * ((y_l * jax.nn.silu(z_l)) @ w_out)
    if with_ctx_out:
        x_ctx = x_ctx + mod_c[2] * ((y_c * jax.nn.silu(z_c)) @ w_out)
    return x_lat, x_ctx


def setup_inputs(seed: int = 0) -> dict:
    key = jax.random.key(seed)
    ks = iter(jax.random.split(key, 40))
    f32 = jnp.float32
    D = D_MODEL

    def nrm(shape, scale):
        return jax.random.normal(next(ks), shape, f32) * scale

    n_idx = jnp.arange(S5_STATE, dtype=f32)
    s5_shape = (N_EVEN, 2, S5_GROUPS, S5_STATE)
    return {
        "x": nrm((BATCH, SEQ, D), 1.0),
        "c": nrm((BATCH, D), 1.0),
        "ctx": nrm((BATCH, CTX_LEN, D), 1.0),
        "c_ctx": nrm((D,), 1.0),
        "ada_w": nrm((DEPTH, D, 3 * D), 0.2 * D ** -0.5),
        "ada_b": nrm((DEPTH, 3 * D), 0.01),
        "norm_g": 1.0 + nrm((DEPTH, D), 0.05),
        "ev_w_in": nrm((N_EVEN, D, 3 * D), D ** -0.5),
        "ev_w_out": nrm((N_EVEN, D, D), D ** -0.5),
        "s5_a_re": -0.5 + nrm(s5_shape, 0.01),
        "s5_a_im": math.pi * n_idx + nrm(s5_shape, 0.01),
        "s5_log_dt": jax.random.uniform(next(ks), (N_EVEN, 2, S5_GROUPS), f32,
                                        math.log(S5_DT_MIN), math.log(S5_DT_MAX)),
        "s5_b_re": nrm((N_EVEN, 2, S5_GROUPS, S5_STATE, S5_GROUP), (2 * S5_GROUP) ** -0.5),
        "s5_b_im": nrm((N_EVEN, 2, S5_GROUPS, S5_STATE, S5_GROUP), (2 * S5_GROUP) ** -0.5),
        "s5_c_re": nrm((N_EVEN, 2, S5_GROUPS, S5_GROUP, S5_STATE), 0.5 ** 0.5),
        "s5_c_im": nrm((N_EVEN, 2, S5_GROUPS, S5_GROUP, S5_STATE), 0.5 ** 0.5),
        "s5_d": nrm((N_EVEN, BRANCH), 1.0),
        "s5_w_glu": nrm((N_EVEN, BRANCH, BRANCH), BRANCH ** -0.5),
        "da_qn_g": 1.0 + nrm((N_EVEN, DA_HEAD), 0.05),
        "da_kn_g": 1.0 + nrm((N_EVEN, DA_HEAD), 0.05),
        "da_lam": nrm((N_EVEN, 4, DA_HEAD), 0.1),
        "da_subln_g": 1.0 + nrm((N_EVEN, DA_VDIM), 0.05),
        "od_w_in": nrm((N_ODD, D, 3 * D), D ** -0.5),
        "od_w_out": nrm((N_ODD, GLA_VAL, D), GLA_VAL ** -0.5),
        "gla_wa1": nrm((N_ODD, 2, D, GLA_RANK), D ** -0.5),
        "gla_wa2": nrm((N_ODD, 2, GLA_RANK, GLA_KEY), GLA_RANK ** -0.5),
        "gla_ba": nrm((N_ODD, 2, GLA_KEY), 0.1),
        "gla_norm_g": 1.0 + nrm((N_ODD, GLA_DV), 0.05),
    }


def reference(x, c, ctx, c_ctx, ada_w, ada_b, norm_g, ev_w_in, ev_w_out,
              s5_a_re, s5_a_im, s5_log_dt, s5_b_re, s5_b_im, s5_c_re, s5_c_im, s5_d, s5_w_glu,
              da_qn_g, da_kn_g, da_lam, da_subln_g,
              od_w_in, od_w_out, gla_wa1, gla_wa2, gla_ba, gla_norm_g):
    x_lat, x_ctx = x, ctx
    for i in range(DEPTH):
        j = i // 2
        with_ctx_out = i < DEPTH - 1
        if i % 2 == 0:
            lam_init = 0.8 - 0.6 * math.exp(-0.3 * i)
            x_lat, x_ctx = even_layer(
                x_lat, x_ctx, c, c_ctx, ada_w[i], ada_b[i], norm_g[i], ev_w_in[j], ev_w_out[j],
                s5_a_re[j], s5_a_im[j], s5_log_dt[j], s5_b_re[j], s5_b_im[j], s5_c_re[j], s5_c_im[j],
                s5_d[j], s5_w_glu[j], da_qn_g[j], da_kn_g[j], da_lam[j], da_subln_g[j],
                lam_init, with_ctx_out)
        else:
            x_lat, x_ctx = odd_layer(
                x_lat, x_ctx, c, c_ctx, ada_w[i], ada_b[i], norm_g[i], od_w_in[j], od_w_out[j],
                gla_wa1[j], gla_wa2[j], gla_ba[j], gla_norm_g[j], with_ctx_out)
    return x_lat
```

```python
import functools
import math

import jax
import jax.numpy as jnp
from jax import lax
from jax.experimental import pallas as pl
from jax.experimental.pallas import tpu as pltpu

F32 = jnp.float32
BF16 = jnp.bfloat16

D_MODEL = 2048
DEPTH = 4
GRID_W = 64
EPS = 1e-6
BRANCH = D_MODEL // 2
S5_GROUP = 16
S5_GROUPS = BRANCH // S5_GROUP
S5_STATE = 64
DA_HEAD = 64
DA_HEADS = BRANCH // (2 * DA_HEAD)
DA_VDIM = 2 * DA_HEAD
ROPE_BASE = 10000.0
GLA_HEADS = 4
GLA_KEY = D_MODEL // 2
GLA_VAL = D_MODEL
GLA_DK = GLA_KEY // GLA_HEADS
GLA_DV = GLA_VAL // GLA_HEADS
GLA_RANK = 16
GLA_TAU = 16.0
GLA_CHUNK = 64

LANES = 128
VMEM_LIMIT = 56 * 1024 * 1024

S5_T = 16
S5_GB = BRANCH // LANES
S5_GPB = LANES // S5_GROUP
S5_W = S5_T * LANES
S5_SW = S5_GPB * S5_STATE


def _cparams(sem):
    return pltpu.CompilerParams(dimension_semantics=sem, vmem_limit_bytes=VMEM_LIMIT)


def _mod_kernel(c_ref, w_ref, b_ref, o_ref):
    c = c_ref[...]
    s = (c * jax.nn.sigmoid(c)).astype(BF16)
    acc = jnp.dot(s, w_ref[...].astype(BF16), preferred_element_type=F32)
    o_ref[...] = acc + b_ref[...]


def modulation_all(cond, ada_w, ada_b):
    tn = 512
    n = 3 * D_MODEL
    return pl.pallas_call(
        _mod_kernel,
        grid=(DEPTH, n // tn),
        in_specs=[
            pl.BlockSpec((8, D_MODEL), lambda l, j: (0, 0)),
            pl.BlockSpec((None, D_MODEL, tn), lambda l, j: (l, 0, j)),
            pl.BlockSpec((None, 1, tn), lambda l, j: (l, 0, j)),
        ],
        out_specs=pl.BlockSpec((None, 8, tn), lambda l, j: (l, 0, j)),
        out_shape=jax.ShapeDtypeStruct((DEPTH, 8, n), F32),
        compiler_params=_cparams(("arbitrary", "arbitrary")),
        name="modulation",
    )(cond, ada_w, ada_b.reshape(DEPTH, 1, n))


def _mm_in_kernel(x_ref, gs_ref, sh_ref, w_ref, *rest, gb_out, has_aux):
    if has_aux:
        wa_ref, o_ref, aux_ref, hn_ref = rest
    else:
        o_ref, hn_ref = rest

    @pl.when(pl.program_id(1) == 0)
    def _():
        x = x_ref[...]
        ms = jnp.mean(x * x, axis=-1, keepdims=True)
        hn = x * lax.rsqrt(ms + EPS) * gs_ref[...] + sh_ref[...]
        hn_ref[...] = hn.astype(BF16)
        if has_aux:
            aux_ref[...] = jnp.dot(hn_ref[...], wa_ref[...], preferred_element_type=F32)

    acc = jnp.dot(hn_ref[...], w_ref[...], preferred_element_type=F32)
    if gb_out:
        for q in range(acc.shape[1] // LANES):
            o_ref[q] = acc[:, q * LANES:(q + 1) * LANES].astype(o_ref.dtype)
    else:
        o_ref[...] = acc.astype(o_ref.dtype)


def mm_in(x, gs, sh, w, col0, ncols, *, rows_per_mod, out_dtype, gb_out=False, w_aux=None, name="mm_in"):
    m = x.shape[0]
    tm = min(1024, rows_per_mod)
    tn = 512
    assert m % tm == 0 and rows_per_mod % tm == 0 and ncols % tn == 0 and col0 % tn == 0
    tpm = rows_per_mod // tm
    jb = col0 // tn
    in_specs = [
        pl.BlockSpec((tm, D_MODEL), lambda i, j: (i, 0)),
        pl.BlockSpec((None, 1, D_MODEL), lambda i, j: (i // tpm, 0, 0)),
        pl.BlockSpec((None, 1, D_MODEL), lambda i, j: (i // tpm, 0, 0)),
        pl.BlockSpec((D_MODEL, tn), lambda i, j: (0, j + jb)),
    ]
    args = [x, gs, sh, w]
    if gb_out:
        out_shape = [jax.ShapeDtypeStruct((ncols // LANES, m, LANES), out_dtype)]
        out_specs = [pl.BlockSpec((tn // LANES, tm, LANES), lambda i, j: (j, i, 0))]
    else:
        out_shape = [jax.ShapeDtypeStruct((m, ncols), out_dtype)]
        out_specs = [pl.BlockSpec((tm, tn), lambda i, j: (i, j))]
    if w_aux is not None:
        in_specs.append(pl.BlockSpec((D_MODEL, LANES), lambda i, j: (0, 0)))
        args.append(w_aux)
        out_shape.append(jax.ShapeDtypeStruct((m, LANES), F32))
        out_specs.append(pl.BlockSpec((tm, LANES), lambda i, j: (i, 0)))
    outs = pl.pallas_call(
        functools.partial(_mm_in_kernel, gb_out=gb_out, has_aux=w_aux is not None),
        grid=(m // tm, ncols // tn),
        in_specs=in_specs,
        out_specs=out_specs,
        out_shape=out_shape,
        scratch_shapes=[pltpu.VMEM((tm, D_MODEL), BF16)],
        compiler_params=_cparams(("arbitrary", "arbitrary")),
        name=name,
    )(*args)
    return outs if w_aux is not None else outs[0]


def _shift_rows(h, s, up):
    n = h.shape[0]
    row = lax.broadcasted_iota(jnp.int32, h.shape, 0)
    if up:
        return jnp.where(row >= n - s, 0.0, pltpu.roll(h, n - s, 0))
    return jnp.where(row < s, 0.0, pltpu.roll(h, s, 0))


def _chunk_scan(zr, zi, pr, pi, reverse):
    n = zr.shape[0]
    ntab = pr.shape[0]
    hr, hi = zr, zi
    s = 1
    while s < n:
        idx = ntab - 1 - s if reverse else s
        ar, ai = pr[idx:idx + 1], pi[idx:idx + 1]
        sr, si = _shift_rows(hr, s, reverse), _shift_rows(hi, s, reverse)
        hr, hi = hr + ar * sr - ai * si, hi + ar * si + ai * sr
        s *= 2
    return hr, hi


def _s5_state_kernel(xl_ref, xc_ref, wz_ref, p_ref, hl_ref, hc_ref):
    ncc = xc_ref.shape[0]
    x = jnp.concatenate([xc_ref[...], xl_ref[...]], axis=0)
    z = jnp.dot(x, wz_ref[...], preferred_element_type=F32)
    p = p_ref[...]
    outs_c, outs_l = [], []
    for d in range(2):
        rev = d == 1
        c0 = 2 * d * S5_SW
        zr, zi = z[:, c0:c0 + S5_SW], z[:, c0 + S5_SW:c0 + 2 * S5_SW]
        pr, pi = p[:, c0:c0 + S5_SW], p[:, c0 + S5_SW:c0 + 2 * S5_SW]
        cr, ci = _chunk_scan(zr[:ncc], zi[:ncc], pr, pi, rev)
        lr, li = _chunk_scan(zr[ncc:], zi[ncc:], pr, pi, rev)
        if rev:
            car_r, car_i = cr[0:1], ci[0:1]
        else:
            car_r, car_i = cr[ncc - 1:ncc], ci[ncc - 1:ncc]
        hcr, hci = _shift_rows(cr, 1, rev), _shift_rows(ci, 1, rev)
        hlr = _shift_rows(lr, 1, rev) + pr * car_r - pi * car_i
        hli = _shift_rows(li, 1, rev) + pr * car_i + pi * car_r
        outs_c += [hcr, hci]
        outs_l += [hlr, hli]
    hc_ref[...] = jnp.concatenate(outs_c, axis=1).astype(hc_ref.dtype)
    hl_ref[...] = jnp.concatenate(outs_l, axis=1).astype(hl_ref.dtype)


def s5_states(u3_lat, u3_ctx, wz, ptab, batch):
    ncl = u3_lat.shape[1] // batch
    ncc = u3_ctx.shape[1] // batch
    assert ptab.shape[1] == ncl
    return pl.pallas_call(
        _s5_state_kernel,
        grid=(S5_GB, batch),
        in_specs=[
            pl.BlockSpec((None, ncl, S5_W), lambda g, b: (g, b, 0)),
            pl.BlockSpec((None, ncc, S5_W), lambda g, b: (g, b, 0)),
            pl.BlockSpec((None, S5_W, 4 * S5_SW), lambda g, b: (g, 0, 0)),
            pl.BlockSpec((None, ncl, 4 * S5_SW), lambda g, b: (g, 0, 0)),
        ],
        out_specs=[
            pl.BlockSpec((None, ncl, 4 * S5_SW), lambda g, b: (g, b, 0)),
            pl.BlockSpec((None, ncc, 4 * S5_SW), lambda g, b: (g, b, 0)),
        ],
        out_shape=[
            jax.ShapeDtypeStruct((S5_GB, u3_lat.shape[1], 4 * S5_SW), BF16),
            jax.ShapeDtypeStruct((S5_GB, u3_ctx.shape[1], 4 * S5_SW), BF16),
        ],
        compiler_params=_cparams(("arbitrary", "arbitrary")),
        name="s5_states",
    )(u3_lat, u3_ctx, wz, ptab)


def _s5_out_kernel(xl_ref, xc_ref, hl_ref, hc_ref, wt_ref, wm_ref, yl_ref, yc_ref):
    nl = xl_ref.shape[0]
    x = jnp.concatenate([xl_ref[...], xc_ref[...]], axis=0)
    h = jnp.concatenate([hl_ref[...], hc_ref[...]], axis=0)
    y = jnp.dot(x, wt_ref[...], preferred_element_type=F32)
    y = y + jnp.dot(h, wm_ref[...], preferred_element_type=F32)
    yl_ref[...] = y[:nl].astype(yl_ref.dtype)
    yc_ref[...] = y[nl:].astype(yc_ref.dtype)


def s5_outputs(u3_lat, u3_ctx, h_lat, h_ctx, wt, wm):
    nl, nc = u3_lat.shape[1], u3_ctx.shape[1]
    tn = 1024
    return pl.pallas_call(
        _s5_out_kernel,
        grid=(S5_GB, S5_W // tn),
        in_specs=[
            pl.BlockSpec((None, nl, S5_W), lambda g, j: (g, 0, 0)),
            pl.BlockSpec((None, nc, S5_W), lambda g, j: (g, 0, 0)),
            pl.BlockSpec((None, nl, 4 * S5_SW), lambda g, j: (g, 0, 0)),
            pl.BlockSpec((None, nc, 4 * S5_SW), lambda g, j: (g, 0, 0)),
            pl.BlockSpec((None, S5_W, tn), lambda g, j: (g, 0, j)),
            pl.BlockSpec((None, 4 * S5_SW, tn), lambda g, j: (g, 0, j)),
        ],
        out_specs=[
            pl.BlockSpec((None, nl, tn), lambda g, j: (g, 0, j)),
            pl.BlockSpec((None, nc, tn), lambda g, j: (g, 0, j)),
        ],
        out_shape=[
            jax.ShapeDtypeStruct((S5_GB, nl, S5_W), F32),
            jax.ShapeDtypeStruct((S5_GB, nc, S5_W), F32),
        ],
        compiler_params=_cparams(("arbitrary", "arbitrary")),
        name="s5_outputs",
    )(u3_lat, u3_ctx, h_lat, h_ctx, wt, wm)


def _s5_finish_kernel(y_ref, wg_ref, zs_ref, o_ref):
    y = jnp.concatenate([y_ref[q] for q in range(S5_GB)], axis=1)
    g = jax.nn.gelu(y)
    t = jnp.dot(g.astype(BF16), wg_ref[...], preferred_element_type=F32)
    zs = zs_ref[...]
    o_ref[...] = (g * jax.nn.sigmoid(t) * (zs * jax.nn.sigmoid(zs))).astype(o_ref.dtype)


def s5_finish(y3, w_glu, hin, zs_block):
    m = y3.shape[1]
    tm = min(512, m)
    return pl.pallas_call(
        _s5_finish_kernel,
        grid=(m // tm,),
        in_specs=[
            pl.BlockSpec((S5_GB, tm, LANES), lambda i: (0, i, 0)),
            pl.BlockSpec((BRANCH, BRANCH), lambda i: (0, 0)),
            pl.BlockSpec((tm, BRANCH), lambda i: (i, zs_block)),
        ],
        out_specs=pl.BlockSpec((tm, BRANCH), lambda i: (i, 0)),
        out_shape=jax.ShapeDtypeStruct((m, BRANCH), BF16),
        compiler_params=_cparams(("arbitrary",)),
        name="s5_finish",
    )(y3, w_glu, hin)


def s5_prepare(a_re, a_im, log_dt, b_re, b_im, c_re, c_im, d_skip, ncl):
    T, G, P, C = S5_T, S5_GROUPS, S5_STATE, S5_GROUP
    a_re, a_im = a_re.astype(F32), a_im.astype(F32)
    dt = jnp.exp(log_dt.astype(F32))[..., None]
    la_re, la_im = a_re * dt, a_im * dt
    mag = jnp.exp(la_re)
    lb_re, lb_im = mag * jnp.cos(la_im), mag * jnp.sin(la_im)
    nr, ni = lb_re - 1.0, lb_im
    den = a_re * a_re + a_im * a_im
    f_re = (nr * a_re + ni * a_im) / den
    f_im = (ni * a_re - nr * a_im) / den
    bb_re = f_re[..., None] * b_re - f_im[..., None] * b_im
    bb_im = f_re[..., None] * b_im + f_im[..., None] * b_re

    def powers(n):
        k = jnp.arange(n, dtype=F32)[None, None, :, None]
        m = jnp.exp(k * la_re[:, :, None, :])
        ang = k * la_im[:, :, None, :]
        return m * jnp.cos(ang), m * jnp.sin(ang)

    pw_re, pw_im = powers(T + 1)
    e_re = pw_re[:, :, :T, :, None] * bb_re[:, :, None] - pw_im[:, :, :T, :, None] * bb_im[:, :, None]
    e_im = pw_re[:, :, :T, :, None] * bb_im[:, :, None] + pw_im[:, :, :T, :, None] * bb_re[:, :, None]
    cr, ci = c_re.astype(F32), c_im.astype(F32)
    hp = lax.Precision.HIGHEST
    kk = (jnp.einsum("dgop,dgtpi->dgtoi", cr, e_re, precision=hp)
          - jnp.einsum("dgop,dgtpi->dgtoi", ci, e_im, precision=hp))
    tin = jnp.arange(T)[:, None]
    tout = jnp.arange(T)[None, :]
    tau = tout - tin
    kf = jnp.where((tau >= 0)[None, :, :, None, None], kk[0][:, jnp.clip(tau, 0, T - 1)], 0.0)
    kb = jnp.where((tau <= 0)[None, :, :, None, None], kk[1][:, jnp.clip(-tau, 0, T - 1)], 0.0)
    skip = (jnp.eye(T, dtype=F32)[None, :, :, None, None]
            * (jnp.eye(C, dtype=F32)[None] * d_skip.astype(F32).reshape(G, C, 1))[:, None, None])
    kt = kf + kb + skip
    eye_g = jnp.eye(S5_GPB, dtype=F32)
    kt = kt.reshape(S5_GB, S5_GPB, T, T, C, C)
    wt = jnp.einsum("bgstoi,gh->bsgitho", kt, eye_g).reshape(S5_GB, S5_W, S5_W)

    ef_re, ef_im = e_re[0][:, ::-1], e_im[0][:, ::-1]
    eb_re, eb_im = e_re[1], e_im[1]
    parts = jnp.stack([ef_re, ef_im, eb_re, eb_im], axis=0)
    parts = parts.reshape(4, S5_GB, S5_GPB, T, P, C)
    wz = jnp.einsum("qbgtpc,gh->btgcqhp", parts, eye_g).reshape(S5_GB, S5_W, 4 * S5_SW)

    pf_re, pf_im = pw_re[0][:, 1:], pw_im[0][:, 1:]
    pb_re, pb_im = pw_re[1][:, :0:-1], pw_im[1][:, :0:-1]
    def cl(c_r, c_i, p_r, p_i):
        return (c_r[:, None] * p_r[:, :, None] - c_i[:, None] * p_i[:, :, None],
                c_r[:, None] * p_i[:, :, None] + c_i[:, None] * p_r[:, :, None])
    clf_re, clf_im = cl(cr[0], ci[0], pf_re, pf_im)
    clb_re, clb_im = cl(cr[1], ci[1], pb_re, pb_im)
    mparts = jnp.stack([clf_re, -clf_im, clb_re, -clb_im], axis=0)
    mparts = mparts.reshape(4, S5_GB, S5_GPB, T, C, P)
    wm = jnp.einsum("qbgtcp,gh->bqgpthc", mparts, eye_g).reshape(S5_GB, 4 * S5_SW, S5_W)

    k = jnp.arange(ncl, dtype=F32)[None, None, :, None] * float(T)
    am = jnp.exp(k * la_re[:, :, None, :])
    aang = k * la_im[:, :, None, :]
    ap_re, ap_im = am * jnp.cos(aang), am * jnp.sin(aang)
    tab = jnp.stack([ap_re[0], ap_im[0], ap_re[1][:, ::-1], ap_im[1][:, ::-1]], axis=0)
    tab = tab.reshape(4, S5_GB, S5_GPB, ncl, P).transpose(1, 3, 0, 2, 4).reshape(S5_GB, ncl, 4 * S5_SW)
    return wz.astype(BF16), wt.astype(BF16), wm.astype(BF16), tab


def _seg_mean_sq(x, ones_bd):
    return jnp.dot(x * x, ones_bd, preferred_element_type=F32,
                   precision=lax.Precision.HIGHEST) * (1.0 / DA_HEAD)


def _rope(x, cos, sin_signed):
    lane = lax.broadcasted_iota(jnp.int32, x.shape, 1)
    first = (lane % (DA_HEAD // 2)) < (DA_HEAD // 4)
    rot = jnp.where(first, pltpu.roll(x, LANES - DA_HEAD // 4, 1), pltpu.roll(x, DA_HEAD // 4, 1))
    return x * cos + rot * sin_signed


def _diff_attn_kernel(*refs, with_lat, lam_scale):
    if with_lat:
        (q_ref, kl_ref, vl_ref, kc_ref, vc_ref, zd_ref, cq_ref, sq_ref, ck_ref, sk_ref,
         qg_ref, kg_ref, sg_ref, lam_ref, ones_ref, o_ref, kn_ref, vb_ref) = refs
    else:
        (q_ref, kc_ref, vc_ref, zd_ref, qg_ref, kg_ref, sg_ref, lam_ref, ones_ref,
         o_ref, kn_ref, vb_ref) = refs
    ones_bd = ones_ref[...]
    nctx = kc_ref.shape[0]

    @pl.when(pl.program_id(2) == 0)
    def _():
        kc = kc_ref[...]
        kn_ref[0:nctx, :] = (kc * lax.rsqrt(_seg_mean_sq(kc, ones_bd) + EPS) * kg_ref[...]).astype(BF16)
        vb_ref[0:nctx, :] = vc_ref[...].astype(BF16)
        if with_lat:
            kl = kl_ref[...]
            kl = kl * lax.rsqrt(_seg_mean_sq(kl, ones_bd) + EPS) * kg_ref[...]
            kn_ref[nctx:, :] = _rope(kl, ck_ref[...], sk_ref[...]).astype(BF16)
            vb_ref[nctx:, :] = vl_ref[...].astype(BF16)

    q = q_ref[...]
    tq = q.shape[0]
    q = q * lax.rsqrt(_seg_mean_sq(q, ones_bd) + EPS) * qg_ref[...]
    if with_lat:
        q = _rope(q, cq_ref[...], sq_ref[...])
    q = q * (DA_HEAD ** -0.5)
    lane = lax.broadcasted_iota(jnp.int32, q.shape, 1)
    q2 = jnp.concatenate([jnp.where(lane < DA_HEAD, q, 0.0), jnp.where(lane >= DA_HEAD, q, 0.0)], axis=0)
    s = lax.dot_general(q2.astype(BF16), kn_ref[...], (((1,), (1,)), ((), ())),
                        preferred_element_type=F32)
    mx = jnp.max(s, axis=-1, keepdims=True)
    p = jnp.exp(s - mx)
    l = jnp.sum(p, axis=-1, keepdims=True)
    pv = jnp.dot(p.astype(BF16), vb_ref[...], preferred_element_type=F32) / l
    o = pv[:tq] - lam_ref[...] * pv[tq:]
    o = o * lax.rsqrt(jnp.mean(o * o, axis=-1, keepdims=True) + EPS) * sg_ref[...] * lam_scale
    zd = zd_ref[...]
    o_ref[...] = (o * (zd * jax.nn.sigmoid(zd))).astype(o_ref.dtype)


def diff_attention(hin_lat, hin_ctx, col_q, col_k, col_v, col_zd, batch, qg, kg, sg, lam_row, lam_init,
                   cos_t, sin_t, ones_bd):
    ml, mc = hin_lat.shape[0], hin_ctx.shape[0]
    L, nctx = ml // batch, mc // batch
    tq = min(256, L)
    nq = L // tq
    small = lambda b, h, i: (0, 0)
    common = [
        pl.BlockSpec((1, LANES), small), pl.BlockSpec((1, LANES), small), pl.BlockSpec((1, LANES), small),
        pl.BlockSpec((1, LANES), small), pl.BlockSpec((LANES, LANES), small),
    ]
    cargs = [qg, kg, sg, lam_row, ones_bd]
    scratch = lambda nk: [pltpu.VMEM((nk, LANES), BF16), pltpu.VMEM((nk, LANES), BF16)]
    lam_scale = 1.0 - lam_init
    y_lat = pl.pallas_call(
        functools.partial(_diff_attn_kernel, with_lat=True, lam_scale=lam_scale),
        grid=(batch, DA_HEADS, nq),
        in_specs=[
            pl.BlockSpec((tq, LANES), lambda b, h, i: (b * nq + i, col_q + h)),
            pl.BlockSpec((L, LANES), lambda b, h, i: (b, col_k + h)),
            pl.BlockSpec((L, LANES), lambda b, h, i: (b, col_v + h)),
            pl.BlockSpec((nctx, LANES), lambda b, h, i: (b, col_k + h)),
            pl.BlockSpec((nctx, LANES), lambda b, h, i: (b, col_v + h)),
            pl.BlockSpec((tq, LANES), lambda b, h, i: (b * nq + i, col_zd + h)),
            pl.BlockSpec((tq, LANES), lambda b, h, i: (i, 0)),
            pl.BlockSpec((tq, LANES), lambda b, h, i: (i, 0)),
            pl.BlockSpec((L, LANES), small),
            pl.BlockSpec((L, LANES), small),
        ] + common,
        out_specs=pl.BlockSpec((tq, LANES), lambda b, h, i: (b * nq + i, h)),
        out_shape=jax.ShapeDtypeStruct((ml, BRANCH), BF16),
        scratch_shapes=scratch(nctx + L),
        compiler_params=_cparams(("arbitrary", "arbitrary", "arbitrary")),
        name="diff_attn_lat",
    )(hin_lat, hin_lat, hin_lat, hin_ctx, hin_ctx, hin_lat, cos_t, sin_t, cos_t, sin_t, *cargs)
    y_ctx = pl.pallas_call(
        functools.partial(_diff_attn_kernel, with_lat=False, lam_scale=lam_scale),
        grid=(batch, DA_HEADS, 1),
        in_specs=[
            pl.BlockSpec((nctx, LANES), lambda b, h, i: (b, col_q + h)),
            pl.BlockSpec((nctx, LANES), lambda b, h, i: (b, col_k + h)),
            pl.BlockSpec((nctx, LANES), lambda b, h, i: (b, col_v + h)),
            pl.BlockSpec((nctx, LANES), lambda b, h, i: (b, col_zd + h)),
        ] + common,
        out_specs=pl.BlockSpec((nctx, LANES), lambda b, h, i: (b, h)),
        out_shape=jax.ShapeDtypeStruct((mc, BRANCH), BF16),
        scratch_shapes=scratch(nctx),
        compiler_params=_cparams(("arbitrary", "arbitrary", "arbitrary")),
        name="diff_attn_ctx",
    )(hin_ctx, hin_ctx, hin_ctx, hin_ctx, *cargs)
    return y_lat, y_ctx


def rope_tables(n_tokens):
    rows = n_tokens // GRID_W
    row = jnp.repeat(jnp.arange(rows, dtype=F32), GRID_W)
    col = jnp.tile(jnp.arange(GRID_W, dtype=F32), rows)
    n_freq = DA_HEAD // 4
    inv_freq = ROPE_BASE ** (-jnp.arange(n_freq, dtype=F32) / n_freq)
    ang_r = row[:, None] * inv_freq
    ang_c = col[:, None] * inv_freq
    ang = jnp.concatenate([ang_r, ang_r, ang_c, ang_c], axis=-1)
    sign = jnp.tile(jnp.concatenate([-jnp.ones(n_freq, F32), jnp.ones(n_freq, F32)]), 2)
    cos = jnp.tile(jnp.cos(ang), (1, 2))
    sin_signed = jnp.tile(jnp.sin(ang) * sign, (1, 2))
    return cos, sin_signed


def _mm_out_even_kernel(a1_ref, a2_ref, w1_ref, w2_ref, x_ref, g_ref, o_ref):
    acc = jnp.dot(a1_ref[...], w1_ref[...], preferred_element_type=F32)
    acc = acc + jnp.dot(a2_ref[...], w2_ref[...], preferred_element_type=F32)
    o_ref[...] = x_ref[...] + g_ref[...] * acc


def mm_out_even(a1, a2, w_out, x, gate, rows_per_mod):
    m = x.shape[0]
    tm = min(1024, rows_per_mod)
    tn = 512
    tpm = rows_per_mod // tm
    return pl.pallas_call(
        _mm_out_even_kernel,
        grid=(m // tm, D_MODEL // tn),
        in_specs=[
            pl.BlockSpec((tm, BRANCH), lambda i, j: (i, 0)),
            pl.BlockSpec((tm, BRANCH), lambda i, j: (i, 0)),
            pl.BlockSpec((BRANCH, tn), lambda i, j: (0, j)),
            pl.BlockSpec((BRANCH, tn), lambda i, j: (1, j)),
            pl.BlockSpec((tm, tn), lambda i, j: (i, j)),
            pl.BlockSpec((None, 1, tn), lambda i, j: (i // tpm, 0, j)),
        ],
        out_specs=pl.BlockSpec((tm, tn), lambda i, j: (i, j)),
        out_shape=jax.ShapeDtypeStruct((m, D_MODEL), F32),
        compiler_params=_cparams(("arbitrary", "arbitrary")),
        name="mm_out_even",
    )(a1, a2, w_out, w_out, x, gate)


def _mm_out_odd_kernel(o_in_ref, z_ref, ng_ref, w_ref, x_ref, g_ref, o_ref, a_ref):
    @pl.when(pl.program_id(1) == 0)
    def _():
        z = z_ref[...]
        sz = z * jax.nn.sigmoid(z)
        for h in range(GLA_HEADS):
            sl = slice(h * GLA_DV, (h + 1) * GLA_DV)
            o = o_in_ref[:, sl]
            y = o * lax.rsqrt(jnp.mean(o * o, axis=-1, keepdims=True) + EPS) * ng_ref[...]
            a_ref[:, sl] = (y * sz[:, sl]).astype(BF16)

    acc = jnp.dot(a_ref[...], w_ref[...], preferred_element_type=F32)
    o_ref[...] = x_ref[...] + g_ref[...] * acc


def mm_out_odd(o_gla, hin, z_block, norm_g, w_out, x, gate, rows_per_mod):
    m = x.shape[0]
    tm = min(512, rows_per_mod)
    tn = 512
    tpm = rows_per_mod // tm
    return pl.pallas_call(
        _mm_out_odd_kernel,
        grid=(m // tm, D_MODEL // tn),
        in_specs=[
            pl.BlockSpec((tm, GLA_VAL), lambda i, j: (i, 0)),
            pl.BlockSpec((tm, GLA_VAL), lambda i, j: (i, z_block)),
            pl.BlockSpec((1, GLA_DV), lambda i, j: (0, 0)),
            pl.BlockSpec((GLA_VAL, tn), lambda i, j: (0, j)),
            pl.BlockSpec((tm, tn), lambda i, j: (i, j)),
            pl.BlockSpec((None, 1, tn), lambda i, j: (i // tpm, 0, j)),
        ],
        out_specs=pl.BlockSpec((tm, tn), lambda i, j: (i, j)),
        out_shape=jax.ShapeDtypeStruct((m, D_MODEL), F32),
        scratch_shapes=[pltpu.VMEM((tm, GLA_VAL), BF16)],
        compiler_params=_cparams(("arbitrary", "arbitrary")),
        name="mm_out_odd",
    )(o_gla, hin, norm_g, w_out, x, gate)


def _gla_kernel(ql_ref, kl_ref, vl_ref, rl_ref, qc_ref, kc_ref, vc_ref, rc_ref, wa_ref, ba_ref,
                *rest, ctx_out):
    if ctx_out:
        ol_ref, oc_ref, s_ref = rest
    else:
        ol_ref, s_ref = rest
    C = GLA_CHUNK
    nl, nc = ql_ref.shape[0] // C, kc_ref.shape[0] // C
    row = lax.broadcasted_iota(jnp.int32, (C, C), 0)
    col = lax.broadcasted_iota(jnp.int32, (C, C), 1)
    hp = lax.Precision.HIGHEST
    ol_ref[...] = jnp.zeros_like(ol_ref)
    if ctx_out:
        oc_ref[...] = jnp.zeros_like(oc_ref)

    for d in range(2):
        rev = d == 1
        incl = (col >= row) if rev else (col <= row)
        tri = incl.astype(F32)
        s_ref[...] = jnp.zeros_like(s_ref)

        def chunk(ci, q_ref, k_ref, v_ref, r_ref, o_ref, n_chunks):
            c = (n_chunks - 1 - ci) if rev else ci
            rows = pl.ds(pl.multiple_of(c * C, C), C)
            logits = jnp.dot(r_ref[rows, :], wa_ref[d], preferred_element_type=F32, precision=hp) + ba_ref[d]
            a = jax.nn.log_sigmoid(logits) * (1.0 / GLA_TAU)
            b = jnp.dot(tri, a, preferred_element_type=F32, precision=hp)
            b_end = b[0:1] if rev else b[C - 1:C]
            k = k_ref[rows, :]
            v = v_ref[rows, :].astype(BF16)
            k_end = (k * jnp.exp(b_end - b)).astype(BF16)
            s_old = s_ref[...]
            if o_ref is not None:
                q_dec = (q_ref[rows, :] * (GLA_DK ** -0.5) * jnp.exp(b)).astype(BF16)
                k_dec = (k * jnp.exp(-b)).astype(BF16)
                sc = lax.dot_general(q_dec, k_dec, (((1,), (1,)), ((), ())), preferred_element_type=F32)
                sc = jnp.where(incl, sc, 0.0).astype(BF16)
                o = jnp.dot(sc, v, preferred_element_type=F32)
                o = o + jnp.dot(q_dec, s_old.astype(BF16), preferred_element_type=F32)
                o_ref[rows, :] += o
            dec = jnp.transpose(jnp.broadcast_to(jnp.exp(b_end), (LANES, GLA_DK)))[:, 0:1]
            s_ref[...] = dec * s_old + lax.dot_general(k_end, v, (((0,), (0,)), ((), ())),
                                                       preferred_element_type=F32)

        def ctx_body(ci, carry):
            chunk(ci, qc_ref, kc_ref, vc_ref, rc_ref, oc_ref if ctx_out else None, nc)
            return carry

        def lat_body(ci, carry):
            chunk(ci, ql_ref, kl_ref, vl_ref, rl_ref, ol_ref, nl)
            return carry

        lax.fori_loop(0, nc, ctx_body, 0)
        lax.fori_loop(0, nl, lat_body, 0)


def gla_mix(hin_lat, hin_ctx, r_lat, r_ctx, wa2p, ba, batch, ctx_out):
    ml, mc = hin_lat.shape[0], hin_ctx.shape[0]
    L, nctx = ml // batch, mc // batch
    kb = GLA_KEY // GLA_DK
    vb = 2 * GLA_KEY // GLA_DV
    in_specs = [
        pl.BlockSpec((L, GLA_DK), lambda b, h: (b, h)),
        pl.BlockSpec((L, GLA_DK), lambda b, h: (b, kb + h)),
        pl.BlockSpec((L, GLA_DV), lambda b, h: (b, vb + h)),
        pl.BlockSpec((L, LANES), lambda b, h: (b, 0)),
        pl.BlockSpec((nctx, GLA_DK), lambda b, h: (b, h)),
        pl.BlockSpec((nctx, GLA_DK), lambda b, h: (b, kb + h)),
        pl.BlockSpec((nctx, GLA_DV), lambda b, h: (b, vb + h)),
        pl.BlockSpec((nctx, LANES), lambda b, h: (b, 0)),
        pl.BlockSpec((2, LANES, GLA_DK), lambda b, h: (0, 0, h)),
        pl.BlockSpec((2, 1, GLA_DK), lambda b, h: (0, 0, h)),
    ]
    out_specs = [pl.BlockSpec((L, GLA_DV), lambda b, h: (b, h)),
                 pl.BlockSpec((nctx, GLA_DV), lambda b, h: (b, h))]
    out_shape = [jax.ShapeDtypeStruct((ml, GLA_VAL), F32), jax.ShapeDtypeStruct((mc, GLA_VAL), F32)]
    if not ctx_out:
        out_specs, out_shape = out_specs[:1], out_shape[:1]
    outs = pl.pallas_call(
        functools.partial(_gla_kernel, ctx_out=ctx_out),
        grid=(batch, GLA_HEADS),
        in_specs=in_specs,
        out_specs=out_specs,
        out_shape=out_shape,
        scratch_shapes=[pltpu.VMEM((GLA_DK, GLA_DV), F32)],
        compiler_params=_cparams(("arbitrary", "arbitrary")),
        name="gla_mix",
    )(hin_lat, hin_lat, hin_lat, r_lat, hin_ctx, hin_ctx, hin_ctx, r_ctx, wa2p, ba)
    return (outs[0], outs[1]) if ctx_out else (outs[0], None)


def _mods(mod_l, norm_g, batch):
    shift, scale, gate = mod_l[:, :D_MODEL], mod_l[:, D_MODEL:2 * D_MODEL], mod_l[:, 2 * D_MODEL:]
    gs = (norm_g.astype(F32)[None, :] * (1.0 + scale))[:, None, :]
    sh = shift[:, None, :]
    gt = gate[:, None, :]
    lat = (gs[:batch], sh[:batch], gt[:batch])
    ctx = (gs[batch:batch + 1], sh[batch:batch + 1], gt[batch:batch + 1])
    return lat, ctx


def even_layer(x_lat, x_ctx, mod_l, norm_g, w_in, w_out, s5p, w_glu, qn_g, kn_g, lam_vecs, subln_g,
               lam_init, batch, rope):
    L, nctx = x_lat.shape[0] // batch, x_ctx.shape[0] // batch
    (gs_l, sh_l, gt_l), (gs_c, sh_c, gt_c) = _mods(mod_l, norm_g, batch)
    w_in_b = w_in.astype(BF16)
    rest = 5 * BRANCH
    u3_l = mm_in(x_lat, gs_l, sh_l, w_in_b, 0, BRANCH, rows_per_mod=L, out_dtype=BF16, gb_out=True, name="mm_in_u")
    u3_c = mm_in(x_ctx, gs_c, sh_c, w_in_b, 0, BRANCH, rows_per_mod=x_ctx.shape[0], out_dtype=BF16,
                 gb_out=True, name="mm_in_u_ctx")
    hin_l = mm_in(x_lat, gs_l, sh_l, w_in_b, BRANCH, rest, rows_per_mod=L, out_dtype=F32, name="mm_in_rest")
    hin_c = mm_in(x_ctx, gs_c, sh_c, w_in_b, BRANCH, rest, rows_per_mod=x_ctx.shape[0], out_dtype=F32,
                  name="mm_in_rest_ctx")

    wz, wt, wm, ptab = s5p
    u3_l = u3_l.reshape(S5_GB, x_lat.shape[0] // S5_T, S5_W)
    u3_c = u3_c.reshape(S5_GB, x_ctx.shape[0] // S5_T, S5_W)
    h_l, h_c = s5_states(u3_l, u3_c, wz, ptab, batch)
    y_l, y_c = s5_outputs(u3_l, u3_c, h_l, h_c, wt, wm)
    wg = w_glu.astype(BF16)
    a_s5_l = s5_finish(y_l.reshape(S5_GB, x_lat.shape[0], LANES), wg, hin_l, 0)
    a_s5_c = s5_finish(y_c.reshape(S5_GB, x_ctx.shape[0], LANES), wg, hin_c, 0)

    lv = lam_vecs.astype(F32)
    lam = jnp.exp(jnp.sum(lv[0] * lv[1])) - jnp.exp(jnp.sum(lv[2] * lv[3])) + lam_init
    lam_row = jnp.full((1, LANES), lam, F32)
    qg = jnp.tile(qn_g.astype(F32), 2)[None, :]
    kg = jnp.tile(kn_g.astype(F32), 2)[None, :]
    sg = subln_g.astype(F32)[None, :]
    cb = BRANCH // LANES
    cos_t, sin_t, ones_bd = rope
    a_da_l, a_da_c = diff_attention(hin_l, hin_c, cb, 2 * cb, 3 * cb, 4 * cb, batch, qg, kg, sg, lam_row,
                                    lam_init, cos_t, sin_t, ones_bd)

    w_out_b = w_out.astype(BF16)
    x_lat = mm_out_even(a_s5_l, a_da_l, w_out_b, x_lat, gt_l, L)
    x_ctx = mm_out_even(a_s5_c, a_da_c, w_out_b, x_ctx, gt_c, x_ctx.shape[0])
    return x_lat, x_ctx


def odd_layer(x_lat, x_ctx, mod_l, norm_g, w_in, w_out, wa1, wa2, ba, gla_norm_g, batch, with_ctx_out):
    L = x_lat.shape[0] // batch
    (gs_l, sh_l, gt_l), (gs_c, sh_c, gt_c) = _mods(mod_l, norm_g, batch)
    w_in_b = w_in.astype(BF16)
    n = 3 * D_MODEL
    w_aux = jnp.zeros((D_MODEL, LANES), F32).at[:, :GLA_RANK].set(wa1[0]).at[:, GLA_RANK:2 * GLA_RANK].set(wa1[1])
    w_aux = w_aux.astype(BF16)
    hin_l, r_l = mm_in(x_lat, gs_l, sh_l, w_in_b, 0, n, rows_per_mod=L, out_dtype=F32, w_aux=w_aux,
                       name="mm_in_odd")
    hin_c, r_c = mm_in(x_ctx, gs_c, sh_c, w_in_b, 0, n, rows_per_mod=x_ctx.shape[0], out_dtype=F32,
                       w_aux=w_aux, name="mm_in_odd_ctx")
    wa2p = jnp.zeros((2, LANES, GLA_KEY), F32)
    wa2p = wa2p.at[0, :GLA_RANK].set(wa2[0]).at[1, GLA_RANK:2 * GLA_RANK].set(wa2[1])
    o_l, o_c = gla_mix(hin_l, hin_c, r_l, r_c, wa2p, ba.astype(F32).reshape(2, 1, GLA_KEY), batch, with_ctx_out)
    w_out_b = w_out.astype(BF16)
    ng = gla_norm_g.astype(F32)[None, :]
    x_lat = mm_out_odd(o_l, hin_l, 2, ng, w_out_b, x_lat, gt_l, L)
    if with_ctx_out:
        x_ctx = mm_out_odd(o_c, hin_c, 2, ng, w_out_b, x_ctx, gt_c, x_ctx.shape[0])
    return x_lat, x_ctx


def kernel(x, c, ctx, c_ctx, ada_w, ada_b, norm_g, ev_w_in, ev_w_out, s5_a_re, s5_a_im, s5_log_dt, s5_b_re, s5_b_im, s5_c_re, s5_c_im, s5_d, s5_w_glu, da_qn_g, da_kn_g, da_lam, da_subln_g, od_w_in, od_w_out, gla_wa1, gla_wa2, gla_ba, gla_norm_g):
    batch, L, _ = x.shape
    nctx = ctx.shape[1]
    x_lat = x.reshape(batch * L, D_MODEL)
    x_ctx = ctx.reshape(batch * nctx, D_MODEL)
    cond = jnp.zeros((8, D_MODEL), F32).at[:batch].set(c).at[batch].set(c_ctx)
    mod = modulation_all(cond, ada_w, ada_b)

    cos_t, sin_t = rope_tables(L)
    seg = jnp.arange(LANES) // DA_HEAD
    ones_bd = (seg[:, None] == seg[None, :]).astype(F32)
    rope = (cos_t, sin_t, ones_bd)

    for i in range(DEPTH):
        j = i // 2
        with_ctx_out = i < DEPTH - 1
        if i % 2 == 0:
            lam_init = 0.8 - 0.6 * math.exp(-0.3 * i)
            s5p = s5_prepare(s5_a_re[j], s5_a_im[j], s5_log_dt[j], s5_b_re[j], s5_b_im[j], s5_c_re[j],
                             s5_c_im[j], s5_d[j], L // S5_T)
            x_lat, x_ctx = even_layer(x_lat, x_ctx, mod[i], norm_g[i], ev_w_in[j], ev_w_out[j], s5p,
                                      s5_w_glu[j], da_qn_g[j], da_kn_g[j], da_lam[j], da_subln_g[j],
                                      lam_init, batch, rope)
        else:
            x_lat, x_ctx = odd_layer(x_lat, x_ctx, mod[i], norm_g[i], od_w_in[j], od_w_out[j],
                                     gla_wa1[j], gla_wa2[j], gla_ba[j], gla_norm_g[j], batch, with_ctx_out)
    return x_lat.reshape(batch, L, D_MODEL)
```

```python
import functools
import math

import jax
import jax.numpy as jnp
from jax import lax
from jax.experimental import pallas as pl
from jax.experimental.pallas import tpu as pltpu

F32 = jnp.float32
BF16 = jnp.bfloat16

D_MODEL = 2048
DEPTH = 4
GRID_W = 64
EPS = 1e-6
BRANCH = D_MODEL // 2
S5_GROUP = 16
S5_GROUPS = BRANCH // S5_GROUP
S5_STATE = 64
DA_HEAD = 64
DA_HEADS = BRANCH // (2 * DA_HEAD)
DA_VDIM = 2 * DA_HEAD
ROPE_BASE = 10000.0
GLA_HEADS = 4
GLA_KEY = D_MODEL // 2
GLA_VAL = D_MODEL
GLA_DK = GLA_KEY // GLA_HEADS
GLA_DV = GLA_VAL // GLA_HEADS
GLA_RANK = 16
GLA_TAU = 16.0
GLA_CHUNK = 64

LANES = 128
VMEM_LIMIT = 56 * 1024 * 1024

S5_T = 16
S5_GB = BRANCH // LANES
S5_GPB = LANES // S5_GROUP
S5_W = S5_T * LANES
S5_SW = S5_GPB * S5_STATE


def _cparams(sem):
    return pltpu.CompilerParams(dimension_semantics=sem, vmem_limit_bytes=VMEM_LIMIT)


def _mod_kernel(c_ref, w_ref, b_ref, o_ref):
    c = c_ref[...]
    s = (c * jax.nn.sigmoid(c)).astype(BF16)
    acc = jnp.dot(s, w_ref[...].astype(BF16), preferred_element_type=F32)
    o_ref[...] = acc + b_ref[...]


def modulation_all(cond, ada_w, ada_b):
    tn = 512
    n = 3 * D_MODEL
    return pl.pallas_call(
        _mod_kernel,
        grid=(DEPTH, n // tn),
        in_specs=[
            pl.BlockSpec((8, D_MODEL), lambda l, j: (0, 0)),
            pl.BlockSpec((None, D_MODEL, tn), lambda l, j: (l, 0, j)),
            pl.BlockSpec((None, 1, tn), lambda l, j: (l, 0, j)),
        ],
        out_specs=pl.BlockSpec((None, 8, tn), lambda l, j: (l, 0, j)),
        out_shape=jax.ShapeDtypeStruct((DEPTH, 8, n), F32),
        compiler_params=_cparams(("arbitrary", "arbitrary")),
        name="modulation",
    )(cond, ada_w, ada_b.reshape(DEPTH, 1, n))


def _mm_in_kernel(x_ref, gs_ref, sh_ref, w_ref, *rest, gb_out, has_aux):
    if has_aux:
        wa_ref, o_ref, aux_ref, hn_ref = rest
    else:
        o_ref, hn_ref = rest

    @pl.when(pl.program_id(1) == 0)
    def _():
        x = x_ref[...]
        ms = jnp.mean(x * x, axis=-1, keepdims=True)
        hn = x * lax.rsqrt(ms + EPS) * gs_ref[...] + sh_ref[...]
        hn_ref[...] = hn.astype(BF16)
        if has_aux:
            aux_ref[...] = jnp.dot(hn_ref[...], wa_ref[...], preferred_element_type=F32)

    acc = jnp.dot(hn_ref[...], w_ref[...], preferred_element_type=F32)
    if gb_out:
        for q in range(acc.shape[1] // LANES):
            o_ref[q] = acc[:, q * LANES:(q + 1) * LANES].astype(o_ref.dtype)
    else:
        o_ref[...] = acc.astype(o_ref.dtype)


def mm_in(x, gs, sh, w, col0, ncols, *, rows_per_mod, out_dtype, gb_out=False, w_aux=None, name="mm_in"):
    m = x.shape[0]
    tm = min(1024, rows_per_mod)
    tn = 512
    assert m % tm == 0 and rows_per_mod % tm == 0 and ncols % tn == 0 and col0 % tn == 0
    tpm = rows_per_mod // tm
    jb = col0 // tn
    in_specs = [
        pl.BlockSpec((tm, D_MODEL), lambda i, j: (i, 0)),
        pl.BlockSpec((None, 1, D_MODEL), lambda i, j: (i // tpm, 0, 0)),
        pl.BlockSpec((None, 1, D_MODEL), lambda i, j: (i // tpm, 0, 0)),
        pl.BlockSpec((D_MODEL, tn), lambda i, j: (0, j + jb)),
    ]
    args = [x, gs, sh, w]
    if gb_out:
        out_shape = [jax.ShapeDtypeStruct((ncols // LANES, m, LANES), out_dtype)]
        out_specs = [pl.BlockSpec((tn // LANES, tm, LANES), lambda i, j: (j, i, 0))]
    else:
        out_shape = [jax.ShapeDtypeStruct((m, ncols), out_dtype)]
        out_specs = [pl.BlockSpec((tm, tn), lambda i, j: (i, j))]
    if w_aux is not None:
        in_specs.append(pl.BlockSpec((D_MODEL, LANES), lambda i, j: (0, 0)))
        args.append(w_aux)
        out_shape.append(jax.ShapeDtypeStruct((m, LANES), F32))
        out_specs.append(pl.BlockSpec((tm, LANES), lambda i, j: (i, 0)))
    outs = pl.pallas_call(
        functools.partial(_mm_in_kernel, gb_out=gb_out, has_aux=w_aux is not None),
        grid=(m // tm, ncols // tn),
        in_specs=in_specs,
        out_specs=out_specs,
        out_shape=out_shape,
        scratch_shapes=[pltpu.VMEM((tm, D_MODEL), BF16)],
        compiler_params=_cparams(("arbitrary", "arbitrary")),
        name=name,
    )(*args)
    return outs if w_aux is not None else outs[0]


def _shift_rows(h, s, up):
    n = h.shape[0]
    row = lax.broadcasted_iota(jnp.int32, h.shape, 0)
    if up:
        return jnp.where(row >= n - s, 0.0, pltpu.roll(h, n - s, 0))
    return jnp.where(row < s, 0.0, pltpu.roll(h, s, 0))


def _chunk_scan(zr, zi, pr, pi, reverse):
    n = zr.shape[0]
    ntab = pr.shape[0]
    hr, hi = zr, zi
    s = 1
    while s < n:
        idx = ntab - 1 - s if reverse else s
        ar, ai = pr[idx:idx + 1], pi[idx:idx + 1]
        sr, si = _shift_rows(hr, s, reverse), _shift_rows(hi, s, reverse)
        hr, hi = hr + ar * sr - ai * si, hi + ar * si + ai * sr
        s *= 2
    return hr, hi


def _group_mask(shape, row_shift, col_shift):
    rg = lax.broadcasted_iota(jnp.int32, shape, 0) >> row_shift
    cg = (lax.broadcasted_iota(jnp.int32, shape, 1) >> col_shift) & (S5_GPB - 1)
    return rg == cg


def _s5_state_kernel(xl_ref, xc_ref, ez_ref, p_ref, hl_ref, hc_ref, wz_ref):
    ncc = xc_ref.shape[0]

    @pl.when(pl.program_id(1) == 0)
    def _():
        mask = _group_mask((LANES, S5_SW), 4, 6)
        for t in range(S5_T):
            for q in range(4):
                e = jnp.concatenate([ez_ref[t, q]] * S5_GPB, axis=0)
                wz_ref[t * LANES:(t + 1) * LANES, q * S5_SW:(q + 1) * S5_SW] = jnp.where(
                    mask, e, jnp.zeros_like(e))

    x = jnp.concatenate([xc_ref[...], xl_ref[...]], axis=0)
    z = jnp.dot(x, wz_ref[...], preferred_element_type=F32)
    p = p_ref[...]
    outs_c, outs_l = [], []
    for d in range(2):
        rev = d == 1
        c0 = 2 * d * S5_SW
        zr, zi = z[:, c0:c0 + S5_SW], z[:, c0 + S5_SW:c0 + 2 * S5_SW]
        pr, pi = p[:, c0:c0 + S5_SW], p[:, c0 + S5_SW:c0 + 2 * S5_SW]
        cr, ci = _chunk_scan(zr[:ncc], zi[:ncc], pr, pi, rev)
        lr, li = _chunk_scan(zr[ncc:], zi[ncc:], pr, pi, rev)
        if rev:
            car_r, car_i = cr[0:1], ci[0:1]
        else:
            car_r, car_i = cr[ncc - 1:ncc], ci[ncc - 1:ncc]
        hcr, hci = _shift_rows(cr, 1, rev), _shift_rows(ci, 1, rev)
        hlr = _shift_rows(lr, 1, rev) + pr * car_r - pi * car_i
        hli = _shift_rows(li, 1, rev) + pr * car_i + pi * car_r
        outs_c += [hcr, hci]
        outs_l += [hlr, hli]
    hc_ref[...] = jnp.concatenate(outs_c, axis=1).astype(hc_ref.dtype)
    hl_ref[...] = jnp.concatenate(outs_l, axis=1).astype(hl_ref.dtype)


def s5_states(u3_lat, u3_ctx, ez, ptab, batch):
    ncl = u3_lat.shape[1] // batch
    ncc = u3_ctx.shape[1] // batch
    assert ptab.shape[1] == ncl
    return pl.pallas_call(
        _s5_state_kernel,
        grid=(S5_GB, batch),
        in_specs=[
            pl.BlockSpec((None, ncl, S5_W), lambda g, b: (g, b, 0)),
            pl.BlockSpec((None, ncc, S5_W), lambda g, b: (g, b, 0)),
            pl.BlockSpec((None, S5_T, 4, S5_GROUP, S5_SW), lambda g, b: (g, 0, 0, 0, 0)),
            pl.BlockSpec((None, ncl, 4 * S5_SW), lambda g, b: (g, 0, 0)),
        ],
        out_specs=[
            pl.BlockSpec((None, ncl, 4 * S5_SW), lambda g, b: (g, b, 0)),
            pl.BlockSpec((None, ncc, 4 * S5_SW), lambda g, b: (g, b, 0)),
        ],
        out_shape=[
            jax.ShapeDtypeStruct((S5_GB, u3_lat.shape[1], 4 * S5_SW), BF16),
            jax.ShapeDtypeStruct((S5_GB, u3_ctx.shape[1], 4 * S5_SW), BF16),
        ],
        scratch_shapes=[pltpu.VMEM((S5_W, 4 * S5_SW), BF16)],
        compiler_params=_cparams(("arbitrary", "arbitrary")),
        name="s5_states",
    )(u3_lat, u3_ctx, ez, ptab)


S5_TAPS = 2 * S5_T
S5_SEL_IN = LANES
S5_SEL_OUT = (LANES // S5_GROUP) * LANES


def _s5_out_kernel(xl_ref, xc_ref, hl_ref, hc_ref, kl_ref, cl_ref, sel_ref, yl_ref, yc_ref, wt_ref, wm_ref):
    nl = xl_ref.shape[0]
    sel = sel_ref[...]
    taps = []
    mask_t = _group_mask((LANES, S5_SEL_OUT), 4, 4)
    for a in range(S5_TAPS * S5_GROUP // S5_SEL_IN):
        t = jnp.dot(kl_ref[:, a * S5_SEL_IN:(a + 1) * S5_SEL_IN], sel, preferred_element_type=F32)
        taps.append(jnp.where(mask_t, t, 0.0).astype(BF16))
    taps = jnp.concatenate(taps, axis=1)
    for s in range(S5_T):
        lo = (S5_T - 1 - s) * LANES
        wt_ref[s * LANES:(s + 1) * LANES, :] = taps[:, lo:lo + S5_W]
    mask_m = _group_mask((S5_SW, S5_SEL_OUT), 6, 4)
    for q in range(4):
        for a in range(S5_T * S5_GROUP // S5_SEL_IN):
            m = jnp.dot(cl_ref[q, :, a * S5_SEL_IN:(a + 1) * S5_SEL_IN], sel, preferred_element_type=F32)
            wm_ref[q * S5_SW:(q + 1) * S5_SW, a * S5_SEL_OUT:(a + 1) * S5_SEL_OUT] = jnp.where(
                mask_m, m, 0.0).astype(BF16)
    x = jnp.concatenate([xl_ref[...], xc_ref[...]], axis=0)
    h = jnp.concatenate([hl_ref[...], hc_ref[...]], axis=0)
    y = jnp.dot(x, wt_ref[...], preferred_element_type=F32)
    y = y + jnp.dot(h, wm_ref[...], preferred_element_type=F32)
    yl_ref[...] = y[:nl].astype(yl_ref.dtype)
    yc_ref[...] = y[nl:].astype(yc_ref.dtype)


def s5_outputs(u3_lat, u3_ctx, h_lat, h_ctx, kl, cl, sel):
    nl, nc = u3_lat.shape[1], u3_ctx.shape[1]
    return pl.pallas_call(
        _s5_out_kernel,
        grid=(S5_GB,),
        in_specs=[
            pl.BlockSpec((None, nl, S5_W), lambda g: (g, 0, 0)),
            pl.BlockSpec((None, nc, S5_W), lambda g: (g, 0, 0)),
            pl.BlockSpec((None, nl, 4 * S5_SW), lambda g: (g, 0, 0)),
            pl.BlockSpec((None, nc, 4 * S5_SW), lambda g: (g, 0, 0)),
            pl.BlockSpec((None, LANES, S5_TAPS * S5_GROUP), lambda g: (g, 0, 0)),
            pl.BlockSpec((None, 4, S5_SW, S5_T * S5_GROUP), lambda g: (g, 0, 0, 0)),
            pl.BlockSpec((S5_SEL_IN, S5_SEL_OUT), lambda g: (0, 0)),
        ],
        out_specs=[
            pl.BlockSpec((None, nl, S5_W), lambda g: (g, 0, 0)),
            pl.BlockSpec((None, nc, S5_W), lambda g: (g, 0, 0)),
        ],
        out_shape=[
            jax.ShapeDtypeStruct((S5_GB, nl, S5_W), BF16),
            jax.ShapeDtypeStruct((S5_GB, nc, S5_W), BF16),
        ],
        scratch_shapes=[pltpu.VMEM((S5_W, S5_W), BF16), pltpu.VMEM((4 * S5_SW, S5_W), BF16)],
        compiler_params=_cparams(("arbitrary",)),
        name="s5_outputs",
    )(u3_lat, u3_ctx, h_lat, h_ctx, kl, cl, sel)


def _s5_finish_kernel(y_ref, wg_ref, zs_ref, o_ref):
    y = jnp.concatenate([y_ref[q] for q in range(S5_GB)], axis=1)
    g = jax.nn.gelu(y.astype(F32))
    t = jnp.dot(g.astype(BF16), wg_ref[...], preferred_element_type=F32)
    zs = zs_ref[...]
    o_ref[...] = (g * jax.nn.sigmoid(t) * (zs * jax.nn.sigmoid(zs))).astype(o_ref.dtype)


def s5_finish(y3, w_glu, hin, zs_block):
    m = y3.shape[1]
    tm = min(512, m)
    return pl.pallas_call(
        _s5_finish_kernel,
        grid=(m // tm,),
        in_specs=[
            pl.BlockSpec((S5_GB, tm, LANES), lambda i: (0, i, 0)),
            pl.BlockSpec((BRANCH, BRANCH), lambda i: (0, 0)),
            pl.BlockSpec((tm, BRANCH), lambda i: (i, zs_block)),
        ],
        out_specs=pl.BlockSpec((tm, BRANCH), lambda i: (i, 0)),
        out_shape=jax.ShapeDtypeStruct((m, BRANCH), BF16),
        compiler_params=_cparams(("arbitrary",)),
        name="s5_finish",
    )(y3, w_glu, hin)


def s5_prepare(a_re, a_im, log_dt, b_re, b_im, c_re, c_im, d_skip, ncl):
    T, G, P, C = S5_T, S5_GROUPS, S5_STATE, S5_GROUP
    GB, GPB, SW = S5_GB, S5_GPB, S5_SW
    hp = lax.Precision.HIGHEST
    a_re, a_im = a_re.astype(F32), a_im.astype(F32)
    dt = jnp.exp(log_dt.astype(F32))[..., None]
    la_re, la_im = a_re * dt, a_im * dt
    mag = jnp.exp(la_re)
    lb_re, lb_im = mag * jnp.cos(la_im), mag * jnp.sin(la_im)
    nr, ni = lb_re - 1.0, lb_im
    den = a_re * a_re + a_im * a_im
    f_re = (nr * a_re + ni * a_im) / den
    f_im = (ni * a_re - nr * a_im) / den
    bb_re = f_re[..., None] * b_re - f_im[..., None] * b_im
    bb_im = f_re[..., None] * b_im + f_im[..., None] * b_re

    def cpow(k, lr, li):
        m = jnp.exp(k * lr)
        return m * jnp.cos(k * li), m * jnp.sin(k * li)

    lad_re, lad_im = la_re.reshape(2, GB, 1, SW), la_im.reshape(2, GB, 1, SW)
    tt = jnp.arange(T, dtype=F32).reshape(1, 1, T, 1)
    pw_re, pw_im = cpow(tt, lad_re, lad_im)
    to_lanes = lambda w: w.reshape(2, GB, GPB, P, C).transpose(0, 1, 4, 2, 3).reshape(2, GB, C, SW)
    bt_re, bt_im = to_lanes(bb_re), to_lanes(bb_im)
    e_re = pw_re[:, :, :, None] * bt_re[:, :, None] - pw_im[:, :, :, None] * bt_im[:, :, None]
    e_im = pw_re[:, :, :, None] * bt_im[:, :, None] + pw_im[:, :, :, None] * bt_re[:, :, None]
    ez = jnp.stack([e_re[0][:, ::-1], e_im[0][:, ::-1], e_re[1], e_im[1]], axis=2)

    cr = c_re.astype(F32).reshape(2, GB, GPB, C, P)
    ci = c_im.astype(F32).reshape(2, GB, GPB, C, P)
    e6_re, e6_im = e_re.reshape(2, GB, T, C, GPB, P), e_im.reshape(2, GB, T, C, GPB, P)
    kk = (jnp.einsum("dbgop,dbtigp->dbgito", cr, e6_re, precision=hp)
          - jnp.einsum("dbgop,dbtigp->dbgito", ci, e6_im, precision=hp))
    skip = jnp.eye(C, dtype=F32) * d_skip.astype(F32).reshape(GB, GPB, C, 1)
    center = kk[0][..., 0, :] + kk[1][..., 0, :] + skip
    kl = jnp.concatenate([kk[1][..., :0:-1, :], center[..., None, :], kk[0][..., 1:, :],
                          jnp.zeros((GB, GPB, C, 1, C), F32)], axis=3)
    kl = kl.reshape(GB, LANES, S5_TAPS * C)

    lar_re, lar_im = la_re[..., None], la_im[..., None]
    steps = jnp.stack([jnp.arange(1, T + 1, dtype=F32), jnp.arange(T, 0, -1).astype(F32)])
    pr_re, pr_im = cpow(steps.reshape(2, 1, 1, T), lar_re, lar_im)
    cp_re = c_re.astype(F32).transpose(0, 1, 3, 2)[:, :, :, None, :]
    cp_im = c_im.astype(F32).transpose(0, 1, 3, 2)[:, :, :, None, :]
    cl_re = cp_re * pr_re[..., None] - cp_im * pr_im[..., None]
    cl_im = cp_re * pr_im[..., None] + cp_im * pr_re[..., None]
    cl = jnp.stack([cl_re[0], -cl_im[0], cl_re[1], -cl_im[1]], axis=0)
    cl = cl.reshape(4, GB, SW, T * C).transpose(1, 0, 2, 3)

    kf = jnp.arange(ncl, dtype=F32).reshape(1, ncl, 1) * float(T)
    af_re, af_im = cpow(kf, lad_re[0], lad_im[0])
    ab_re, ab_im = cpow(kf[:, ::-1], lad_re[1], lad_im[1])
    tab = jnp.concatenate([af_re, af_im, ab_re, ab_im], axis=-1)

    r = jnp.arange(S5_SEL_IN)
    q = jnp.arange(S5_SEL_OUT)
    sel = ((r[:, None] // C == q[None, :] // LANES) & (r[:, None] % C == q[None, :] % C)).astype(BF16)
    return ez.astype(BF16), kl.astype(BF16), cl.astype(BF16), tab, sel


def _seg_mean_sq(x, ones_bd):
    return jnp.dot(x * x, ones_bd, preferred_element_type=F32,
                   precision=lax.Precision.HIGHEST) * (1.0 / DA_HEAD)


def _rope(x, cos, sin_signed):
    lane = lax.broadcasted_iota(jnp.int32, x.shape, 1)
    first = (lane % (DA_HEAD // 2)) < (DA_HEAD // 4)
    rot = jnp.where(first, pltpu.roll(x, LANES - DA_HEAD // 4, 1), pltpu.roll(x, DA_HEAD // 4, 1))
    return x * cos + rot * sin_signed


LOG2E = 1.4426950408889634


def _stack_maps(q):
    lane = lax.broadcasted_iota(jnp.int32, q.shape, 1)
    return jnp.concatenate([jnp.where(lane < DA_HEAD, q, 0.0), jnp.where(lane >= DA_HEAD, q, 0.0)], axis=0)


def _diff_combine(pv, tq, lam, sg, lam_scale, zd):
    o = pv[:, :DA_VDIM] / pv[:, DA_VDIM:]
    o = o[:tq] - lam * o[tq:]
    o = o * lax.rsqrt(jnp.mean(o * o, axis=-1, keepdims=True) + EPS) * sg * lam_scale
    return o * (zd * jax.nn.sigmoid(zd))


def _diff_attn_ctx_kernel(q_ref, kc_ref, vc_ref, zd_ref, qg_ref, kg_ref, sg_ref, lam_ref, ones_ref, o_ref,
                          *, lam_scale):
    ones_bd = ones_ref[...]
    kc = kc_ref[...]
    kn = (kc * lax.rsqrt(_seg_mean_sq(kc, ones_bd) + EPS) * kg_ref[...]).astype(BF16)
    v1 = jnp.concatenate([vc_ref[...], jnp.ones_like(vc_ref)], axis=1).astype(BF16)
    q = q_ref[...]
    tq = q.shape[0]
    q = q * lax.rsqrt(_seg_mean_sq(q, ones_bd) + EPS) * qg_ref[...] * (DA_HEAD ** -0.5)
    s = lax.dot_general(_stack_maps(q).astype(BF16), kn, (((1,), (1,)), ((), ())),
                        preferred_element_type=F32)
    p = jnp.exp(s - jnp.max(s, axis=-1, keepdims=True))
    pv = jnp.dot(p.astype(BF16), v1, preferred_element_type=F32)
    o_ref[...] = _diff_combine(pv, tq, lam_ref[...], sg_ref[...], lam_scale, zd_ref[...]).astype(o_ref.dtype)


def _diff_attn_lat_kernel(q_ref, kl_ref, vl_ref, kc_ref, vc_ref, zd_ref, cos_ref, sin_ref,
                          qg_ref, kg_ref, sg_ref, lam_ref, ones_ref, o_ref, kn_ref, v1_ref,
                          s0_ref, s1_ref, m0_ref, m1_ref, *, tq, lam_scale):
    ones_bd = ones_ref[...]
    nctx = kc_ref.shape[0]
    nq = q_ref.shape[0] // tq

    kc = kc_ref[...]
    kn_ref[0:nctx, :] = (kc * lax.rsqrt(_seg_mean_sq(kc, ones_bd) + EPS) * kg_ref[...]).astype(BF16)
    kl = kl_ref[...]
    kl = kl * lax.rsqrt(_seg_mean_sq(kl, ones_bd) + EPS) * kg_ref[...]
    kn_ref[nctx:, :] = _rope(kl, cos_ref[...], sin_ref[...]).astype(BF16)
    v1_ref[0:nctx, :] = jnp.concatenate([vc_ref[...], jnp.ones_like(vc_ref)], axis=1).astype(BF16)
    v1_ref[nctx:, :] = jnp.concatenate([vl_ref[...], jnp.ones_like(vl_ref)], axis=1).astype(BF16)

    bufs = ((s0_ref, m0_ref), (s1_ref, m1_ref))

    def scores(i, slot):
        s_ref, m_ref = bufs[slot]
        rows = pl.ds(pl.multiple_of(i * tq, tq), tq)
        q = q_ref[rows, :]
        q = q * lax.rsqrt(_seg_mean_sq(q, ones_bd) + EPS) * qg_ref[...]
        q = _rope(q, cos_ref[rows, :], sin_ref[rows, :]) * (DA_HEAD ** -0.5 * LOG2E)
        s = lax.dot_general(_stack_maps(q).astype(BF16), kn_ref[...], (((1,), (1,)), ((), ())),
                            preferred_element_type=F32)
        s_ref[...] = s
        m_ref[...] = jnp.broadcast_to(jnp.max(s, axis=-1, keepdims=True), m_ref.shape)

    def finish(i, slot):
        s_ref, m_ref = bufs[slot]
        rows = pl.ds(pl.multiple_of(i * tq, tq), tq)
        p = jnp.exp2(s_ref[...] - m_ref[:, 0:1])
        pv = jnp.dot(p.astype(BF16), v1_ref[...], preferred_element_type=F32)
        o = _diff_combine(pv, tq, lam_ref[...], sg_ref[...], lam_scale, zd_ref[rows, :])
        o_ref[rows, :] = o.astype(o_ref.dtype)

    assert nq % 2 == 0
    scores(0, 0)

    def body(j, carry):
        scores(2 * j + 1, 1)
        finish(2 * j, 0)
        scores(2 * j + 2, 0)
        finish(2 * j + 1, 1)
        return carry

    lax.fori_loop(0, nq // 2 - 1, body, 0)
    scores(nq - 1, 1)
    finish(nq - 2, 0)
    finish(nq - 1, 1)


def diff_attention(hin_lat, hin_ctx, col_q, col_k, col_v, col_zd, batch, qg, kg, sg, lam_row, lam_init,
                   cos_t, sin_t, ones_bd):
    ml, mc = hin_lat.shape[0], hin_ctx.shape[0]
    L, nctx = ml // batch, mc // batch
    tq = min(256, L)
    small = lambda b, h: (0, 0)
    common = [
        pl.BlockSpec((1, LANES), small), pl.BlockSpec((1, LANES), small), pl.BlockSpec((1, LANES), small),
        pl.BlockSpec((1, LANES), small), pl.BlockSpec((LANES, LANES), small),
    ]
    cargs = [qg, kg, sg, lam_row, ones_bd]
    lam_scale = 1.0 - lam_init
    nk = nctx + L
    y_lat = pl.pallas_call(
        functools.partial(_diff_attn_lat_kernel, tq=tq, lam_scale=lam_scale),
        grid=(batch, DA_HEADS),
        in_specs=[
            pl.BlockSpec((L, LANES), lambda b, h: (b, col_q + h)),
            pl.BlockSpec((L, LANES), lambda b, h: (b, col_k + h)),
            pl.BlockSpec((L, LANES), lambda b, h: (b, col_v + h)),
            pl.BlockSpec((nctx, LANES), lambda b, h: (b, col_k + h)),
            pl.BlockSpec((nctx, LANES), lambda b, h: (b, col_v + h)),
            pl.BlockSpec((L, LANES), lambda b, h: (b, col_zd + h)),
            pl.BlockSpec((L, LANES), small),
            pl.BlockSpec((L, LANES), small),
        ] + common,
        out_specs=pl.BlockSpec((L, LANES), lambda b, h: (b, h)),
        out_shape=jax.ShapeDtypeStruct((ml, BRANCH), BF16),
        scratch_shapes=[
            pltpu.VMEM((nk, LANES), BF16),
            pltpu.VMEM((nk, 2 * DA_VDIM), BF16),
            pltpu.VMEM((2 * tq, nk), F32), pltpu.VMEM((2 * tq, nk), F32),
            pltpu.VMEM((2 * tq, LANES), F32), pltpu.VMEM((2 * tq, LANES), F32),
        ],
        compiler_params=_cparams(("arbitrary", "arbitrary")),
        name="diff_attn_lat",
    )(hin_lat, hin_lat, hin_lat, hin_ctx, hin_ctx, hin_lat, cos_t, sin_t, *cargs)
    y_ctx = pl.pallas_call(
        functools.partial(_diff_attn_ctx_kernel, lam_scale=lam_scale),
        grid=(batch, DA_HEADS),
        in_specs=[
            pl.BlockSpec((nctx, LANES), lambda b, h: (b, col_q + h)),
            pl.BlockSpec((nctx, LANES), lambda b, h: (b, col_k + h)),
            pl.BlockSpec((nctx, LANES), lambda b, h: (b, col_v + h)),
            pl.BlockSpec((nctx, LANES), lambda b, h: (b, col_zd + h)),
        ] + common,
        out_specs=pl.BlockSpec((nctx, LANES), lambda b, h: (b, h)),
        out_shape=jax.ShapeDtypeStruct((mc, BRANCH), BF16),
        compiler_params=_cparams(("arbitrary", "arbitrary")),
        name="diff_attn_ctx",
    )(hin_ctx, hin_ctx, hin_ctx, hin_ctx, *cargs)
    return y_lat, y_ctx


def rope_tables(n_tokens):
    rows = n_tokens // GRID_W
    row = jnp.repeat(jnp.arange(rows, dtype=F32), GRID_W)
    col = jnp.tile(jnp.arange(GRID_W, dtype=F32), rows)
    n_freq = DA_HEAD // 4
    inv_freq = ROPE_BASE ** (-jnp.arange(n_freq, dtype=F32) / n_freq)
    ang_r = row[:, None] * inv_freq
    ang_c = col[:, None] * inv_freq
    ang = jnp.concatenate([ang_r, ang_r, ang_c, ang_c], axis=-1)
    sign = jnp.tile(jnp.concatenate([-jnp.ones(n_freq, F32), jnp.ones(n_freq, F32)]), 2)
    cos = jnp.tile(jnp.cos(ang), (1, 2))
    sin_signed = jnp.tile(jnp.sin(ang) * sign, (1, 2))
    return cos, sin_signed


def _mm_out_even_kernel(a1_ref, a2_ref, w1_ref, w2_ref, x_ref, g_ref, o_ref):
    acc = jnp.dot(a1_ref[...], w1_ref[...], preferred_element_type=F32)
    acc = acc + jnp.dot(a2_ref[...], w2_ref[...], preferred_element_type=F32)
    o_ref[...] = x_ref[...] + g_ref[...] * acc


def mm_out_even(a1, a2, w_out, x, gate, rows_per_mod):
    m = x.shape[0]
    tm = min(1024, rows_per_mod)
    tn = 512
    tpm = rows_per_mod // tm
    return pl.pallas_call(
        _mm_out_even_kernel,
        grid=(m // tm, D_MODEL // tn),
        in_specs=[
            pl.BlockSpec((tm, BRANCH), lambda i, j: (i, 0)),
            pl.BlockSpec((tm, BRANCH), lambda i, j: (i, 0)),
            pl.BlockSpec((BRANCH, tn), lambda i, j: (0, j)),
            pl.BlockSpec((BRANCH, tn), lambda i, j: (1, j)),
            pl.BlockSpec((tm, tn), lambda i, j: (i, j)),
            pl.BlockSpec((None, 1, tn), lambda i, j: (i // tpm, 0, j)),
        ],
        out_specs=pl.BlockSpec((tm, tn), lambda i, j: (i, j)),
        out_shape=jax.ShapeDtypeStruct((m, D_MODEL), F32),
        compiler_params=_cparams(("arbitrary", "arbitrary")),
        name="mm_out_even",
    )(a1, a2, w_out, w_out, x, gate)


def _mm_out_odd_kernel(o_in_ref, z_ref, ng_ref, w_ref, x_ref, g_ref, o_ref, a_ref):
    @pl.when(pl.program_id(1) == 0)
    def _():
        z = z_ref[...]
        sz = z * jax.nn.sigmoid(z)
        for h in range(GLA_HEADS):
            sl = slice(h * GLA_DV, (h + 1) * GLA_DV)
            o = o_in_ref[:, sl]
            y = o * lax.rsqrt(jnp.mean(o * o, axis=-1, keepdims=True) + EPS) * ng_ref[...]
            a_ref[:, sl] = (y * sz[:, sl]).astype(BF16)

    acc = jnp.dot(a_ref[...], w_ref[...], preferred_element_type=F32)
    o_ref[...] = x_ref[...] + g_ref[...] * acc


def mm_out_odd(o_gla, hin, z_block, norm_g, w_out, x, gate, rows_per_mod):
    m = x.shape[0]
    tm = min(512, rows_per_mod)
    tn = 512
    tpm = rows_per_mod // tm
    return pl.pallas_call(
        _mm_out_odd_kernel,
        grid=(m // tm, D_MODEL // tn),
        in_specs=[
            pl.BlockSpec((tm, GLA_VAL), lambda i, j: (i, 0)),
            pl.BlockSpec((tm, GLA_VAL), lambda i, j: (i, z_block)),
            pl.BlockSpec((1, GLA_DV), lambda i, j: (0, 0)),
            pl.BlockSpec((GLA_VAL, tn), lambda i, j: (0, j)),
            pl.BlockSpec((tm, tn), lambda i, j: (i, j)),
            pl.BlockSpec((None, 1, tn), lambda i, j: (i // tpm, 0, j)),
        ],
        out_specs=pl.BlockSpec((tm, tn), lambda i, j: (i, j)),
        out_shape=jax.ShapeDtypeStruct((m, D_MODEL), F32),
        scratch_shapes=[pltpu.VMEM((tm, GLA_VAL), BF16)],
        compiler_params=_cparams(("arbitrary", "arbitrary")),
        name="mm_out_odd",
    )(o_gla, hin, norm_g, w_out, x, gate)


def _gla_kernel(ql_ref, kl_ref, vl_ref, rl_ref, qc_ref, kc_ref, vc_ref, rc_ref, wa_ref, ba_ref,
                *rest, ctx_out):
    if ctx_out:
        ol_ref, oc_ref, s_ref = rest
    else:
        ol_ref, s_ref = rest
    C = GLA_CHUNK
    nl, nc = ql_ref.shape[0] // C, kc_ref.shape[0] // C
    row = lax.broadcasted_iota(jnp.int32, (C, C), 0)
    col = lax.broadcasted_iota(jnp.int32, (C, C), 1)
    hp = lax.Precision.HIGHEST
    ol_ref[...] = jnp.zeros_like(ol_ref)
    if ctx_out:
        oc_ref[...] = jnp.zeros_like(oc_ref)

    for d in range(2):
        rev = d == 1
        incl = (col >= row) if rev else (col <= row)
        tri = incl.astype(F32)
        s_ref[...] = jnp.zeros_like(s_ref)

        def chunk(ci, q_ref, k_ref, v_ref, r_ref, o_ref, n_chunks):
            c = (n_chunks - 1 - ci) if rev else ci
            rows = pl.ds(pl.multiple_of(c * C, C), C)
            logits = jnp.dot(r_ref[rows, :], wa_ref[d], preferred_element_type=F32, precision=hp) + ba_ref[d]
            a = jax.nn.log_sigmoid(logits) * (1.0 / GLA_TAU)
            b = jnp.dot(tri, a, preferred_element_type=F32, precision=hp)
            b_end = b[0:1] if rev else b[C - 1:C]
            k = k_ref[rows, :]
            v = v_ref[rows, :].astype(BF16)
            k_end = (k * jnp.exp(b_end - b)).astype(BF16)
            s_old = s_ref[...]
            if o_ref is not None:
                q_dec = (q_ref[rows, :] * (GLA_DK ** -0.5) * jnp.exp(b)).astype(BF16)
                k_dec = (k * jnp.exp(-b)).astype(BF16)
                sc = lax.dot_general(q_dec, k_dec, (((1,), (1,)), ((), ())), preferred_element_type=F32)
                sc = jnp.where(incl, sc, 0.0).astype(BF16)
                o = jnp.dot(sc, v, preferred_element_type=F32)
                o = o + jnp.dot(q_dec, s_old.astype(BF16), preferred_element_type=F32)
                o_ref[rows, :] += o
            dec = jnp.transpose(jnp.broadcast_to(jnp.exp(b_end), (LANES, GLA_DK)))[:, 0:1]
            s_ref[...] = dec * s_old + lax.dot_general(k_end, v, (((0,), (0,)), ((), ())),
                                                       preferred_element_type=F32)

        def ctx_body(ci, carry):
            chunk(ci, qc_ref, kc_ref, vc_ref, rc_ref, oc_ref if ctx_out else None, nc)
            return carry

        def lat_body(ci, carry):
            chunk(ci, ql_ref, kl_ref, vl_ref, rl_ref, ol_ref, nl)
            return carry

        lax.fori_loop(0, nc, ctx_body, 0)
        lax.fori_loop(0, nl, lat_body, 0)


def gla_mix(hin_lat, hin_ctx, r_lat, r_ctx, wa2p, ba, batch, ctx_out):
    ml, mc = hin_lat.shape[0], hin_ctx.shape[0]
    L, nctx = ml // batch, mc // batch
    kb = GLA_KEY // GLA_DK
    vb = 2 * GLA_KEY // GLA_DV
    in_specs = [
        pl.BlockSpec((L, GLA_DK), lambda b, h: (b, h)),
        pl.BlockSpec((L, GLA_DK), lambda b, h: (b, kb + h)),
        pl.BlockSpec((L, GLA_DV), lambda b, h: (b, vb + h)),
        pl.BlockSpec((L, LANES), lambda b, h: (b, 0)),
        pl.BlockSpec((nctx, GLA_DK), lambda b, h: (b, h)),
        pl.BlockSpec((nctx, GLA_DK), lambda b, h: (b, kb + h)),
        pl.BlockSpec((nctx, GLA_DV), lambda b, h: (b, vb + h)),
        pl.BlockSpec((nctx, LANES), lambda b, h: (b, 0)),
        pl.BlockSpec((2, LANES, GLA_DK), lambda b, h: (0, 0, h)),
        pl.BlockSpec((2, 1, GLA_DK), lambda b, h: (0, 0, h)),
    ]
    out_specs = [pl.BlockSpec((L, GLA_DV), lambda b, h: (b, h)),
                 pl.BlockSpec((nctx, GLA_DV), lambda b, h: (b, h))]
    out_shape = [jax.ShapeDtypeStruct((ml, GLA_VAL), F32), jax.ShapeDtypeStruct((mc, GLA_VAL), F32)]
    if not ctx_out:
        out_specs, out_shape = out_specs[:1], out_shape[:1]
    outs = pl.pallas_call(
        functools.partial(_gla_kernel, ctx_out=ctx_out),
        grid=(batch, GLA_HEADS),
        in_specs=in_specs,
        out_specs=out_specs,
        out_shape=out_shape,
        scratch_shapes=[pltpu.VMEM((GLA_DK, GLA_DV), F32)],
        compiler_params=_cparams(("arbitrary", "arbitrary")),
        name="gla_mix",
    )(hin_lat, hin_lat, hin_lat, r_lat, hin_ctx, hin_ctx, hin_ctx, r_ctx, wa2p, ba)
    return (outs[0], outs[1]) if ctx_out else (outs[0], None)


def _mods(mod_l, norm_g, batch):
    shift, scale, gate = mod_l[:, :D_MODEL], mod_l[:, D_MODEL:2 * D_MODEL], mod_l[:, 2 * D_MODEL:]
    gs = (norm_g.astype(F32)[None, :] * (1.0 + scale))[:, None, :]
    sh = shift[:, None, :]
    gt = gate[:, None, :]
    lat = (gs[:batch], sh[:batch], gt[:batch])
    ctx = (gs[batch:batch + 1], sh[batch:batch + 1], gt[batch:batch + 1])
    return lat, ctx


def even_layer(x_lat, x_ctx, mod_l, norm_g, w_in, w_out, s5p, w_glu, qn_g, kn_g, lam_vecs, subln_g,
               lam_init, batch, rope):
    L, nctx = x_lat.shape[0] // batch, x_ctx.shape[0] // batch
    (gs_l, sh_l, gt_l), (gs_c, sh_c, gt_c) = _mods(mod_l, norm_g, batch)
    w_in_b = w_in.astype(BF16)
    rest = 5 * BRANCH
    u3_l = mm_in(x_lat, gs_l, sh_l, w_in_b, 0, BRANCH, rows_per_mod=L, out_dtype=BF16, gb_out=True, name="mm_in_u")
    u3_c = mm_in(x_ctx, gs_c, sh_c, w_in_b, 0, BRANCH, rows_per_mod=x_ctx.shape[0], out_dtype=BF16,
                 gb_out=True, name="mm_in_u_ctx")
    hin_l = mm_in(x_lat, gs_l, sh_l, w_in_b, BRANCH, rest, rows_per_mod=L, out_dtype=F32, name="mm_in_rest")
    hin_c = mm_in(x_ctx, gs_c, sh_c, w_in_b, BRANCH, rest, rows_per_mod=x_ctx.shape[0], out_dtype=F32,
                  name="mm_in_rest_ctx")

    ez, kl, cl, ptab, sel = s5p
    u3_l = u3_l.reshape(S5_GB, x_lat.shape[0] // S5_T, S5_W)
    u3_c = u3_c.reshape(S5_GB, x_ctx.shape[0] // S5_T, S5_W)
    h_l, h_c = s5_states(u3_l, u3_c, ez, ptab, batch)
    y_l, y_c = s5_outputs(u3_l, u3_c, h_l, h_c, kl, cl, sel)
    wg = w_glu.astype(BF16)
    a_s5_l = s5_finish(y_l.reshape(S5_GB, x_lat.shape[0], LANES), wg, hin_l, 0)
    a_s5_c = s5_finish(y_c.reshape(S5_GB, x_ctx.shape[0], LANES), wg, hin_c, 0)

    lv = lam_vecs.astype(F32)
    lam = jnp.exp(jnp.sum(lv[0] * lv[1])) - jnp.exp(jnp.sum(lv[2] * lv[3])) + lam_init
    lam_row = jnp.full((1, LANES), lam, F32)
    qg = jnp.tile(qn_g.astype(F32), 2)[None, :]
    kg = jnp.tile(kn_g.astype(F32), 2)[None, :]
    sg = subln_g.astype(F32)[None, :]
    cb = BRANCH // LANES
    cos_t, sin_t, ones_bd = rope
    a_da_l, a_da_c = diff_attention(hin_l, hin_c, cb, 2 * cb, 3 * cb, 4 * cb, batch, qg, kg, sg, lam_row,
                                    lam_init, cos_t, sin_t, ones_bd)

    w_out_b = w_out.astype(BF16)
    x_lat = mm_out_even(a_s5_l, a_da_l, w_out_b, x_lat, gt_l, L)
    x_ctx = mm_out_even(a_s5_c, a_da_c, w_out_b, x_ctx, gt_c, x_ctx.shape[0])
    return x_lat, x_ctx


def odd_layer(x_lat, x_ctx, mod_l, norm_g, w_in, w_out, wa1, wa2, ba, gla_norm_g, batch, with_ctx_out):
    L = x_lat.shape[0] // batch
    (gs_l, sh_l, gt_l), (gs_c, sh_c, gt_c) = _mods(mod_l, norm_g, batch)
    w_in_b = w_in.astype(BF16)
    n = 3 * D_MODEL
    w_aux = jnp.zeros((D_MODEL, LANES), F32).at[:, :GLA_RANK].set(wa1[0]).at[:, GLA_RANK:2 * GLA_RANK].set(wa1[1])
    w_aux = w_aux.astype(BF16)
    hin_l, r_l = mm_in(x_lat, gs_l, sh_l, w_in_b, 0, n, rows_per_mod=L, out_dtype=F32, w_aux=w_aux,
                       name="mm_in_odd")
    hin_c, r_c = mm_in(x_ctx, gs_c, sh_c, w_in_b, 0, n, rows_per_mod=x_ctx.shape[0], out_dtype=F32,
                       w_aux=w_aux, name="mm_in_odd_ctx")
    wa2p = jnp.zeros((2, LANES, GLA_KEY), F32)
    wa2p = wa2p.at[0, :GLA_RANK].set(wa2[0]).at[1, GLA_RANK:2 * GLA_RANK].set(wa2[1])
    o_l, o_c = gla_mix(hin_l, hin_c, r_l, r_c, wa2p, ba.astype(F32).reshape(2, 1, GLA_KEY), batch, with_ctx_out)
    w_out_b = w_out.astype(BF16)
    ng = gla_norm_g.astype(F32)[None, :]
    x_lat = mm_out_odd(o_l, hin_l, 2, ng, w_out_b, x_lat, gt_l, L)
    if with_ctx_out:
        x_ctx = mm_out_odd(o_c, hin_c, 2, ng, w_out_b, x_ctx, gt_c, x_ctx.shape[0])
    return x_lat, x_ctx


def kernel(x, c, ctx, c_ctx, ada_w, ada_b, norm_g, ev_w_in, ev_w_out, s5_a_re, s5_a_im, s5_log_dt, s5_b_re, s5_b_im, s5_c_re, s5_c_im, s5_d, s5_w_glu, da_qn_g, da_kn_g, da_lam, da_subln_g, od_w_in, od_w_out, gla_wa1, gla_wa2, gla_ba, gla_norm_g):
    batch, L, _ = x.shape
    nctx = ctx.shape[1]
    x_lat = x.reshape(batch * L, D_MODEL)
    x_ctx = ctx.reshape(batch * nctx, D_MODEL)
    cond = jnp.zeros((8, D_MODEL), F32).at[:batch].set(c).at[batch].set(c_ctx)
    mod = modulation_all(cond, ada_w, ada_b)

    cos_t, sin_t = rope_tables(L)
    seg = jnp.arange(LANES) // DA_HEAD
    ones_bd = (seg[:, None] == seg[None, :]).astype(F32)
    rope = (cos_t, sin_t, ones_bd)

    for i in range(DEPTH):
        j = i // 2
        with_ctx_out = i < DEPTH - 1
        if i % 2 == 0:
            lam_init = 0.8 - 0.6 * math.exp(-0.3 * i)
            s5p = s5_prepare(s5_a_re[j], s5_a_im[j], s5_log_dt[j], s5_b_re[j], s5_b_im[j], s5_c_re[j],
                             s5_c_im[j], s5_d[j], L // S5_T)
            x_lat, x_ctx = even_layer(x_lat, x_ctx, mod[i], norm_g[i], ev_w_in[j], ev_w_out[j], s5p,
                                      s5_w_glu[j], da_qn_g[j], da_kn_g[j], da_lam[j], da_subln_g[j],
                                      lam_init, batch, rope)
        else:
            x_lat, x_ctx = odd_layer(x_lat, x_ctx, mod[i], norm_g[i], od_w_in[j], od_w_out[j],
                                     gla_wa1[j], gla_wa2[j], gla_ba[j], gla_norm_g[j], batch, with_ctx_out)
    return x_lat.reshape(batch, L, D_MODEL)
```

```python
import functools
import math

import jax
import jax.numpy as jnp
from jax import lax
from jax.experimental import pallas as pl
from jax.experimental.pallas import tpu as pltpu

F32 = jnp.float32
BF16 = jnp.bfloat16

D_MODEL = 2048
DEPTH = 4
GRID_W = 64
EPS = 1e-6
BRANCH = D_MODEL // 2
S5_GROUP = 16
S5_GROUPS = BRANCH // S5_GROUP
S5_STATE = 64
DA_HEAD = 64
DA_HEADS = BRANCH // (2 * DA_HEAD)
DA_VDIM = 2 * DA_HEAD
ROPE_BASE = 10000.0
GLA_HEADS = 4
GLA_KEY = D_MODEL // 2
GLA_VAL = D_MODEL
GLA_DK = GLA_KEY // GLA_HEADS
GLA_DV = GLA_VAL // GLA_HEADS
GLA_RANK = 16
GLA_TAU = 16.0
GLA_CHUNK = 64

LANES = 128
VMEM_LIMIT = 56 * 1024 * 1024

S5_T = 16
S5_GB = BRANCH // LANES
S5_GPB = LANES // S5_GROUP
S5_W = S5_T * LANES
S5_SW = S5_GPB * S5_STATE


def _cparams(sem):
    return pltpu.CompilerParams(dimension_semantics=sem, vmem_limit_bytes=VMEM_LIMIT)


def _mod_kernel(c_ref, w_ref, b_ref, o_ref):
    c = c_ref[...]
    s = (c * jax.nn.sigmoid(c)).astype(BF16)
    acc = jnp.dot(s, w_ref[...].astype(BF16), preferred_element_type=F32)
    o_ref[...] = acc + b_ref[...]


def modulation_all(cond, ada_w, ada_b):
    tn = 512
    n = 3 * D_MODEL
    return pl.pallas_call(
        _mod_kernel,
        grid=(DEPTH, n // tn),
        in_specs=[
            pl.BlockSpec((8, D_MODEL), lambda l, j: (0, 0)),
            pl.BlockSpec((None, D_MODEL, tn), lambda l, j: (l, 0, j)),
            pl.BlockSpec((None, 1, tn), lambda l, j: (l, 0, j)),
        ],
        out_specs=pl.BlockSpec((None, 8, tn), lambda l, j: (l, 0, j)),
        out_shape=jax.ShapeDtypeStruct((DEPTH, 8, n), F32),
        compiler_params=_cparams(("arbitrary", "arbitrary")),
        name="modulation",
    )(cond, ada_w, ada_b.reshape(DEPTH, 1, n))


def _mm_in_kernel(x_ref, gs_ref, sh_ref, w_ref, *rest, gb_out, has_aux):
    if has_aux:
        wa_ref, o_ref, aux_ref, hn_ref = rest
    else:
        o_ref, hn_ref = rest

    @pl.when(pl.program_id(1) == 0)
    def _():
        x = x_ref[...]
        ms = jnp.mean(x * x, axis=-1, keepdims=True)
        hn = x * lax.rsqrt(ms + EPS) * gs_ref[...] + sh_ref[...]
        hn_ref[...] = hn.astype(BF16)
        if has_aux:
            aux_ref[...] = jnp.dot(hn_ref[...], wa_ref[...], preferred_element_type=F32)

    acc = jnp.dot(hn_ref[...], w_ref[...], preferred_element_type=F32)
    if gb_out:
        for q in range(acc.shape[1] // LANES):
            o_ref[q] = acc[:, q * LANES:(q + 1) * LANES].astype(o_ref.dtype)
    else:
        o_ref[...] = acc.astype(o_ref.dtype)


def mm_in(x, gs, sh, w, col0, ncols, *, rows_per_mod, out_dtype, gb_out=False, w_aux=None, name="mm_in"):
    m = x.shape[0]
    tm = min(1024, rows_per_mod)
    tn = 512
    assert m % tm == 0 and rows_per_mod % tm == 0 and ncols % tn == 0 and col0 % tn == 0
    tpm = rows_per_mod // tm
    jb = col0 // tn
    in_specs = [
        pl.BlockSpec((tm, D_MODEL), lambda i, j: (i, 0)),
        pl.BlockSpec((None, 1, D_MODEL), lambda i, j: (i // tpm, 0, 0)),
        pl.BlockSpec((None, 1, D_MODEL), lambda i, j: (i // tpm, 0, 0)),
        pl.BlockSpec((D_MODEL, tn), lambda i, j: (0, j + jb)),
    ]
    args = [x, gs, sh, w]
    if gb_out:
        out_shape = [jax.ShapeDtypeStruct((ncols // LANES, m, LANES), out_dtype)]
        out_specs = [pl.BlockSpec((tn // LANES, tm, LANES), lambda i, j: (j, i, 0))]
    else:
        out_shape = [jax.ShapeDtypeStruct((m, ncols), out_dtype)]
        out_specs = [pl.BlockSpec((tm, tn), lambda i, j: (i, j))]
    if w_aux is not None:
        in_specs.append(pl.BlockSpec((D_MODEL, LANES), lambda i, j: (0, 0)))
        args.append(w_aux)
        out_shape.append(jax.ShapeDtypeStruct((m, LANES), F32))
        out_specs.append(pl.BlockSpec((tm, LANES), lambda i, j: (i, 0)))
    outs = pl.pallas_call(
        functools.partial(_mm_in_kernel, gb_out=gb_out, has_aux=w_aux is not None),
        grid=(m // tm, ncols // tn),
        in_specs=in_specs,
        out_specs=out_specs,
        out_shape=out_shape,
        scratch_shapes=[pltpu.VMEM((tm, D_MODEL), BF16)],
        compiler_params=_cparams(("arbitrary", "arbitrary")),
        name=name,
    )(*args)
    return outs if w_aux is not None else outs[0]


def _shift_rows(h, s, up):
    n = h.shape[0]
    row = lax.broadcasted_iota(jnp.int32, h.shape, 0)
    if up:
        return jnp.where(row >= n - s, 0.0, pltpu.roll(h, n - s, 0))
    return jnp.where(row < s, 0.0, pltpu.roll(h, s, 0))


def _chunk_scan(zr, zi, pr, pi, reverse):
    n = zr.shape[0]
    ntab = pr.shape[0]
    hr, hi = zr, zi
    s = 1
    while s < n:
        idx = ntab - 1 - s if reverse else s
        ar, ai = pr[idx:idx + 1], pi[idx:idx + 1]
        sr, si = _shift_rows(hr, s, reverse), _shift_rows(hi, s, reverse)
        hr, hi = hr + ar * sr - ai * si, hi + ar * si + ai * sr
        s *= 2
    return hr, hi


def _group_mask(shape, row_shift, col_shift):
    rg = lax.broadcasted_iota(jnp.int32, shape, 0) >> row_shift
    cg = (lax.broadcasted_iota(jnp.int32, shape, 1) >> col_shift) & (S5_GPB - 1)
    return rg == cg


def _s5_state_kernel(xl_ref, xc_ref, ez_ref, p_ref, hl_ref, hc_ref, wz_ref):
    ncc = xc_ref.shape[0]

    @pl.when(pl.program_id(1) == 0)
    def _():
        mask = _group_mask((LANES, S5_SW), 4, 6)
        for t in range(S5_T):
            for q in range(4):
                e = jnp.concatenate([ez_ref[t, q]] * S5_GPB, axis=0)
                wz_ref[t * LANES:(t + 1) * LANES, q * S5_SW:(q + 1) * S5_SW] = jnp.where(
                    mask, e, jnp.zeros_like(e))

    x = jnp.concatenate([xc_ref[...], xl_ref[...]], axis=0)
    z = jnp.dot(x, wz_ref[...], preferred_element_type=F32)
    p = p_ref[...]
    outs_c, outs_l = [], []
    for d in range(2):
        rev = d == 1
        c0 = 2 * d * S5_SW
        zr, zi = z[:, c0:c0 + S5_SW], z[:, c0 + S5_SW:c0 + 2 * S5_SW]
        pr, pi = p[:, c0:c0 + S5_SW], p[:, c0 + S5_SW:c0 + 2 * S5_SW]
        cr, ci = _chunk_scan(zr[:ncc], zi[:ncc], pr, pi, rev)
        lr, li = _chunk_scan(zr[ncc:], zi[ncc:], pr, pi, rev)
        if rev:
            car_r, car_i = cr[0:1], ci[0:1]
        else:
            car_r, car_i = cr[ncc - 1:ncc], ci[ncc - 1:ncc]
        hcr, hci = _shift_rows(cr, 1, rev), _shift_rows(ci, 1, rev)
        hlr = _shift_rows(lr, 1, rev) + pr * car_r - pi * car_i
        hli = _shift_rows(li, 1, rev) + pr * car_i + pi * car_r
        outs_c += [hcr, hci]
        outs_l += [hlr, hli]
    hc_ref[...] = jnp.concatenate(outs_c, axis=1).astype(hc_ref.dtype)
    hl_ref[...] = jnp.concatenate(outs_l, axis=1).astype(hl_ref.dtype)


def s5_states(u3_lat, u3_ctx, ez, ptab, batch):
    ncl = u3_lat.shape[1] // batch
    ncc = u3_ctx.shape[1] // batch
    assert ptab.shape[1] == ncl
    return pl.pallas_call(
        _s5_state_kernel,
        grid=(S5_GB, batch),
        in_specs=[
            pl.BlockSpec((None, ncl, S5_W), lambda g, b: (g, b, 0)),
            pl.BlockSpec((None, ncc, S5_W), lambda g, b: (g, b, 0)),
            pl.BlockSpec((None, S5_T, 4, S5_GROUP, S5_SW), lambda g, b: (g, 0, 0, 0, 0)),
            pl.BlockSpec((None, ncl, 4 * S5_SW), lambda g, b: (g, 0, 0)),
        ],
        out_specs=[
            pl.BlockSpec((None, ncl, 4 * S5_SW), lambda g, b: (g, b, 0)),
            pl.BlockSpec((None, ncc, 4 * S5_SW), lambda g, b: (g, b, 0)),
        ],
        out_shape=[
            jax.ShapeDtypeStruct((S5_GB, u3_lat.shape[1], 4 * S5_SW), BF16),
            jax.ShapeDtypeStruct((S5_GB, u3_ctx.shape[1], 4 * S5_SW), BF16),
        ],
        scratch_shapes=[pltpu.VMEM((S5_W, 4 * S5_SW), BF16)],
        compiler_params=_cparams(("arbitrary", "arbitrary")),
        name="s5_states",
    )(u3_lat, u3_ctx, ez, ptab)


S5_TAPS = 2 * S5_T
S5_SEL_IN = LANES
S5_SEL_OUT = (LANES // S5_GROUP) * LANES


def _s5_out_kernel(xl_ref, xc_ref, hl_ref, hc_ref, kl_ref, cl_ref, sel_ref, yl_ref, yc_ref, wt_ref, wm_ref):
    nl = xl_ref.shape[0]
    sel = sel_ref[...]
    taps = []
    mask_t = _group_mask((LANES, S5_SEL_OUT), 4, 4)
    for a in range(S5_TAPS * S5_GROUP // S5_SEL_IN):
        t = jnp.dot(kl_ref[:, a * S5_SEL_IN:(a + 1) * S5_SEL_IN], sel, preferred_element_type=F32)
        taps.append(jnp.where(mask_t, t, 0.0).astype(BF16))
    taps = jnp.concatenate(taps, axis=1)
    for s in range(S5_T):
        lo = (S5_T - 1 - s) * LANES
        wt_ref[s * LANES:(s + 1) * LANES, :] = taps[:, lo:lo + S5_W]
    mask_m = _group_mask((S5_SW, S5_SEL_OUT), 6, 4)
    for q in range(4):
        for a in range(S5_T * S5_GROUP // S5_SEL_IN):
            m = jnp.dot(cl_ref[q, :, a * S5_SEL_IN:(a + 1) * S5_SEL_IN], sel, preferred_element_type=F32)
            wm_ref[q * S5_SW:(q + 1) * S5_SW, a * S5_SEL_OUT:(a + 1) * S5_SEL_OUT] = jnp.where(
                mask_m, m, 0.0).astype(BF16)
    x = jnp.concatenate([xl_ref[...], xc_ref[...]], axis=0)
    h = jnp.concatenate([hl_ref[...], hc_ref[...]], axis=0)
    y = jnp.dot(x, wt_ref[...], preferred_element_type=F32)
    y = y + jnp.dot(h, wm_ref[...], preferred_element_type=F32)
    yl_ref[...] = y[:nl].astype(yl_ref.dtype)
    yc_ref[...] = y[nl:].astype(yc_ref.dtype)


def s5_outputs(u3_lat, u3_ctx, h_lat, h_ctx, kl, cl, sel):
    nl, nc = u3_lat.shape[1], u3_ctx.shape[1]
    return pl.pallas_call(
        _s5_out_kernel,
        grid=(S5_GB,),
        in_specs=[
            pl.BlockSpec((None, nl, S5_W), lambda g: (g, 0, 0)),
            pl.BlockSpec((None, nc, S5_W), lambda g: (g, 0, 0)),
            pl.BlockSpec((None, nl, 4 * S5_SW), lambda g: (g, 0, 0)),
            pl.BlockSpec((None, nc, 4 * S5_SW), lambda g: (g, 0, 0)),
            pl.BlockSpec((None, LANES, S5_TAPS * S5_GROUP), lambda g: (g, 0, 0)),
            pl.BlockSpec((None, 4, S5_SW, S5_T * S5_GROUP), lambda g: (g, 0, 0, 0)),
            pl.BlockSpec((S5_SEL_IN, S5_SEL_OUT), lambda g: (0, 0)),
        ],
        out_specs=[
            pl.BlockSpec((None, nl, S5_W), lambda g: (g, 0, 0)),
            pl.BlockSpec((None, nc, S5_W), lambda g: (g, 0, 0)),
        ],
        out_shape=[
            jax.ShapeDtypeStruct((S5_GB, nl, S5_W), BF16),
            jax.ShapeDtypeStruct((S5_GB, nc, S5_W), BF16),
        ],
        scratch_shapes=[pltpu.VMEM((S5_W, S5_W), BF16), pltpu.VMEM((4 * S5_SW, S5_W), BF16)],
        compiler_params=_cparams(("arbitrary",)),
        name="s5_outputs",
    )(u3_lat, u3_ctx, h_lat, h_ctx, kl, cl, sel)


def _s5_finish_kernel(y_ref, wg_ref, zs_ref, o_ref):
    y = jnp.concatenate([y_ref[q] for q in range(S5_GB)], axis=1)
    g = jax.nn.gelu(y.astype(F32))
    t = jnp.dot(g.astype(BF16), wg_ref[...], preferred_element_type=F32)
    zs = zs_ref[...].astype(F32)
    o_ref[...] = (g * jax.nn.sigmoid(t) * (zs * jax.nn.sigmoid(zs))).astype(o_ref.dtype)


def s5_finish(y3, w_glu, hin, zs_block):
    m = y3.shape[1]
    tm = min(512, m)
    return pl.pallas_call(
        _s5_finish_kernel,
        grid=(m // tm,),
        in_specs=[
            pl.BlockSpec((S5_GB, tm, LANES), lambda i: (0, i, 0)),
            pl.BlockSpec((BRANCH, BRANCH), lambda i: (0, 0)),
            pl.BlockSpec((tm, BRANCH), lambda i: (i, zs_block)),
        ],
        out_specs=pl.BlockSpec((tm, BRANCH), lambda i: (i, 0)),
        out_shape=jax.ShapeDtypeStruct((m, BRANCH), BF16),
        compiler_params=_cparams(("arbitrary",)),
        name="s5_finish",
    )(y3, w_glu, hin)


def s5_prepare(a_re, a_im, log_dt, b_re, b_im, c_re, c_im, d_skip, ncl):
    T, G, P, C = S5_T, S5_GROUPS, S5_STATE, S5_GROUP
    GB, GPB, SW = S5_GB, S5_GPB, S5_SW
    hp = lax.Precision.HIGHEST
    a_re, a_im = a_re.astype(F32), a_im.astype(F32)
    dt = jnp.exp(log_dt.astype(F32))[..., None]
    la_re, la_im = a_re * dt, a_im * dt
    mag = jnp.exp(la_re)
    lb_re, lb_im = mag * jnp.cos(la_im), mag * jnp.sin(la_im)
    nr, ni = lb_re - 1.0, lb_im
    den = a_re * a_re + a_im * a_im
    f_re = (nr * a_re + ni * a_im) / den
    f_im = (ni * a_re - nr * a_im) / den
    bb_re = f_re[..., None] * b_re - f_im[..., None] * b_im
    bb_im = f_re[..., None] * b_im + f_im[..., None] * b_re

    def cpow(k, lr, li):
        m = jnp.exp(k * lr)
        return m * jnp.cos(k * li), m * jnp.sin(k * li)

    lad_re, lad_im = la_re.reshape(2, GB, 1, SW), la_im.reshape(2, GB, 1, SW)
    tt = jnp.arange(T, dtype=F32).reshape(1, 1, T, 1)
    pw_re, pw_im = cpow(tt, lad_re, lad_im)
    to_lanes = lambda w: w.reshape(2, GB, GPB, P, C).transpose(0, 1, 4, 2, 3).reshape(2, GB, C, SW)
    bt_re, bt_im = to_lanes(bb_re), to_lanes(bb_im)
    e_re = pw_re[:, :, :, None] * bt_re[:, :, None] - pw_im[:, :, :, None] * bt_im[:, :, None]
    e_im = pw_re[:, :, :, None] * bt_im[:, :, None] + pw_im[:, :, :, None] * bt_re[:, :, None]
    ez = jnp.stack([e_re[0][:, ::-1], e_im[0][:, ::-1], e_re[1], e_im[1]], axis=2)

    cr = c_re.astype(F32).reshape(2, GB, GPB, C, P)
    ci = c_im.astype(F32).reshape(2, GB, GPB, C, P)
    e6_re, e6_im = e_re.reshape(2, GB, T, C, GPB, P), e_im.reshape(2, GB, T, C, GPB, P)
    kk = (jnp.einsum("dbgop,dbtigp->dbgito", cr, e6_re, precision=hp)
          - jnp.einsum("dbgop,dbtigp->dbgito", ci, e6_im, precision=hp))
    skip = jnp.eye(C, dtype=F32) * d_skip.astype(F32).reshape(GB, GPB, C, 1)
    center = kk[0][..., 0, :] + kk[1][..., 0, :] + skip
    kl = jnp.concatenate([kk[1][..., :0:-1, :], center[..., None, :], kk[0][..., 1:, :],
                          jnp.zeros((GB, GPB, C, 1, C), F32)], axis=3)
    kl = kl.reshape(GB, LANES, S5_TAPS * C)

    lar_re, lar_im = la_re[..., None], la_im[..., None]
    steps = jnp.stack([jnp.arange(1, T + 1, dtype=F32), jnp.arange(T, 0, -1).astype(F32)])
    pr_re, pr_im = cpow(steps.reshape(2, 1, 1, T), lar_re, lar_im)
    cp_re = c_re.astype(F32).transpose(0, 1, 3, 2)[:, :, :, None, :]
    cp_im = c_im.astype(F32).transpose(0, 1, 3, 2)[:, :, :, None, :]
    cl_re = cp_re * pr_re[..., None] - cp_im * pr_im[..., None]
    cl_im = cp_re * pr_im[..., None] + cp_im * pr_re[..., None]
    cl = jnp.stack([cl_re[0], -cl_im[0], cl_re[1], -cl_im[1]], axis=0)
    cl = cl.reshape(4, GB, SW, T * C).transpose(1, 0, 2, 3)

    kf = jnp.arange(ncl, dtype=F32).reshape(1, ncl, 1) * float(T)
    af_re, af_im = cpow(kf, lad_re[0], lad_im[0])
    ab_re, ab_im = cpow(kf[:, ::-1], lad_re[1], lad_im[1])
    tab = jnp.concatenate([af_re, af_im, ab_re, ab_im], axis=-1)

    r = jnp.arange(S5_SEL_IN)
    q = jnp.arange(S5_SEL_OUT)
    sel = ((r[:, None] // C == q[None, :] // LANES) & (r[:, None] % C == q[None, :] % C)).astype(BF16)
    return ez.astype(BF16), kl.astype(BF16), cl.astype(BF16), tab, sel


def _seg_mean_sq(x, ones_bd):
    return jnp.dot(x * x, ones_bd, preferred_element_type=F32,
                   precision=lax.Precision.HIGHEST) * (1.0 / DA_HEAD)


def _rope(x, cos, sin_signed):
    lane = lax.broadcasted_iota(jnp.int32, x.shape, 1)
    first = (lane % (DA_HEAD // 2)) < (DA_HEAD // 4)
    rot = jnp.where(first, pltpu.roll(x, LANES - DA_HEAD // 4, 1), pltpu.roll(x, DA_HEAD // 4, 1))
    return x * cos + rot * sin_signed


LOG2E = 1.4426950408889634


def _stack_maps(q):
    lane = lax.broadcasted_iota(jnp.int32, q.shape, 1)
    return jnp.concatenate([jnp.where(lane < DA_HEAD, q, 0.0), jnp.where(lane >= DA_HEAD, q, 0.0)], axis=0)


def _diff_combine(pv, tq, lam, sg, lam_scale, zd):
    o = pv[:, :DA_VDIM] / pv[:, DA_VDIM:]
    o = o[:tq] - lam * o[tq:]
    o = o * lax.rsqrt(jnp.mean(o * o, axis=-1, keepdims=True) + EPS) * sg * lam_scale
    return o * (zd * jax.nn.sigmoid(zd))


def _diff_attn_ctx_kernel(q_ref, kc_ref, vc_ref, zd_ref, qg_ref, kg_ref, sg_ref, lam_ref, ones_ref, o_ref,
                          *, lam_scale):
    ones_bd = ones_ref[...]
    kc = kc_ref[...].astype(F32)
    kn = (kc * lax.rsqrt(_seg_mean_sq(kc, ones_bd) + EPS) * kg_ref[...]).astype(BF16)
    v1 = jnp.concatenate([vc_ref[...], jnp.ones_like(vc_ref)], axis=1).astype(BF16)
    q = q_ref[...].astype(F32)
    tq = q.shape[0]
    q = q * lax.rsqrt(_seg_mean_sq(q, ones_bd) + EPS) * qg_ref[...] * (DA_HEAD ** -0.5)
    s = lax.dot_general(_stack_maps(q).astype(BF16), kn, (((1,), (1,)), ((), ())),
                        preferred_element_type=F32)
    p = jnp.exp(s - jnp.max(s, axis=-1, keepdims=True))
    pv = jnp.dot(p.astype(BF16), v1, preferred_element_type=F32)
    o_ref[...] = _diff_combine(pv, tq, lam_ref[...], sg_ref[...], lam_scale,
                               zd_ref[...].astype(F32)).astype(o_ref.dtype)


def _diff_attn_lat_kernel(q_ref, kl_ref, vl_ref, kc_ref, vc_ref, zd_ref, cos_ref, sin_ref,
                          qg_ref, kg_ref, sg_ref, lam_ref, ones_ref, o_ref, kn_ref, v1_ref,
                          s0_ref, s1_ref, m0_ref, m1_ref, *, tq, lam_scale):
    ones_bd = ones_ref[...]
    nctx = kc_ref.shape[0]
    nq = q_ref.shape[0] // tq

    kc = kc_ref[...].astype(F32)
    kn_ref[0:nctx, :] = (kc * lax.rsqrt(_seg_mean_sq(kc, ones_bd) + EPS) * kg_ref[...]).astype(BF16)
    kl = kl_ref[...].astype(F32)
    kl = kl * lax.rsqrt(_seg_mean_sq(kl, ones_bd) + EPS) * kg_ref[...]
    kn_ref[nctx:, :] = _rope(kl, cos_ref[...], sin_ref[...]).astype(BF16)
    v1_ref[0:nctx, :] = jnp.concatenate([vc_ref[...], jnp.ones_like(vc_ref)], axis=1).astype(BF16)
    v1_ref[nctx:, :] = jnp.concatenate([vl_ref[...], jnp.ones_like(vl_ref)], axis=1).astype(BF16)

    bufs = ((s0_ref, m0_ref), (s1_ref, m1_ref))

    def scores(i, slot):
        s_ref, m_ref = bufs[slot]
        rows = pl.ds(pl.multiple_of(i * tq, tq), tq)
        q = q_ref[rows, :].astype(F32)
        q = q * lax.rsqrt(_seg_mean_sq(q, ones_bd) + EPS) * qg_ref[...]
        q = _rope(q, cos_ref[rows, :], sin_ref[rows, :]) * (DA_HEAD ** -0.5 * LOG2E)
        s = lax.dot_general(_stack_maps(q).astype(BF16), kn_ref[...], (((1,), (1,)), ((), ())),
                            preferred_element_type=F32)
        s_ref[...] = s
        m_ref[...] = jnp.broadcast_to(jnp.max(s, axis=-1, keepdims=True), m_ref.shape)

    def finish(i, slot):
        s_ref, m_ref = bufs[slot]
        rows = pl.ds(pl.multiple_of(i * tq, tq), tq)
        p = jnp.exp2(s_ref[...] - m_ref[:, 0:1])
        pv = jnp.dot(p.astype(BF16), v1_ref[...], preferred_element_type=F32)
        o = _diff_combine(pv, tq, lam_ref[...], sg_ref[...], lam_scale, zd_ref[rows, :].astype(F32))
        o_ref[rows, :] = o.astype(o_ref.dtype)

    assert nq % 2 == 0
    scores(0, 0)

    def body(j, carry):
        scores(2 * j + 1, 1)
        finish(2 * j, 0)
        scores(2 * j + 2, 0)
        finish(2 * j + 1, 1)
        return carry

    lax.fori_loop(0, nq // 2 - 1, body, 0)
    scores(nq - 1, 1)
    finish(nq - 2, 0)
    finish(nq - 1, 1)


def diff_attention(hin_lat, hin_ctx, col_q, col_k, col_v, col_zd, batch, qg, kg, sg, lam_row, lam_init,
                   cos_t, sin_t, ones_bd):
    ml, mc = hin_lat.shape[0], hin_ctx.shape[0]
    L, nctx = ml // batch, mc // batch
    tq = min(256, L)
    small = lambda b, h: (0, 0)
    common = [
        pl.BlockSpec((1, LANES), small), pl.BlockSpec((1, LANES), small), pl.BlockSpec((1, LANES), small),
        pl.BlockSpec((1, LANES), small), pl.BlockSpec((LANES, LANES), small),
    ]
    cargs = [qg, kg, sg, lam_row, ones_bd]
    lam_scale = 1.0 - lam_init
    nk = nctx + L
    y_lat = pl.pallas_call(
        functools.partial(_diff_attn_lat_kernel, tq=tq, lam_scale=lam_scale),
        grid=(batch, DA_HEADS),
        in_specs=[
            pl.BlockSpec((L, LANES), lambda b, h: (b, col_q + h)),
            pl.BlockSpec((L, LANES), lambda b, h: (b, col_k + h)),
            pl.BlockSpec((L, LANES), lambda b, h: (b, col_v + h)),
            pl.BlockSpec((nctx, LANES), lambda b, h: (b, col_k + h)),
            pl.BlockSpec((nctx, LANES), lambda b, h: (b, col_v + h)),
            pl.BlockSpec((L, LANES), lambda b, h: (b, col_zd + h)),
            pl.BlockSpec((L, LANES), small),
            pl.BlockSpec((L, LANES), small),
        ] + common,
        out_specs=pl.BlockSpec((L, LANES), lambda b, h: (b, h)),
        out_shape=jax.ShapeDtypeStruct((ml, BRANCH), BF16),
        scratch_shapes=[
            pltpu.VMEM((nk, LANES), BF16),
            pltpu.VMEM((nk, 2 * DA_VDIM), BF16),
            pltpu.VMEM((2 * tq, nk), F32), pltpu.VMEM((2 * tq, nk), F32),
            pltpu.VMEM((2 * tq, LANES), F32), pltpu.VMEM((2 * tq, LANES), F32),
        ],
        compiler_params=_cparams(("arbitrary", "arbitrary")),
        name="diff_attn_lat",
    )(hin_lat, hin_lat, hin_lat, hin_ctx, hin_ctx, hin_lat, cos_t, sin_t, *cargs)
    y_ctx = pl.pallas_call(
        functools.partial(_diff_attn_ctx_kernel, lam_scale=lam_scale),
        grid=(batch, DA_HEADS),
        in_specs=[
            pl.BlockSpec((nctx, LANES), lambda b, h: (b, col_q + h)),
            pl.BlockSpec((nctx, LANES), lambda b, h: (b, col_k + h)),
            pl.BlockSpec((nctx, LANES), lambda b, h: (b, col_v + h)),
            pl.BlockSpec((nctx, LANES), lambda b, h: (b, col_zd + h)),
        ] + common,
        out_specs=pl.BlockSpec((nctx, LANES), lambda b, h: (b, h)),
        out_shape=jax.ShapeDtypeStruct((mc, BRANCH), BF16),
        compiler_params=_cparams(("arbitrary", "arbitrary")),
        name="diff_attn_ctx",
    )(hin_ctx, hin_ctx, hin_ctx, hin_ctx, *cargs)
    return y_lat, y_ctx


def rope_tables(n_tokens):
    rows = n_tokens // GRID_W
    row = jnp.repeat(jnp.arange(rows, dtype=F32), GRID_W)
    col = jnp.tile(jnp.arange(GRID_W, dtype=F32), rows)
    n_freq = DA_HEAD // 4
    inv_freq = ROPE_BASE ** (-jnp.arange(n_freq, dtype=F32) / n_freq)
    ang_r = row[:, None] * inv_freq
    ang_c = col[:, None] * inv_freq
    ang = jnp.concatenate([ang_r, ang_r, ang_c, ang_c], axis=-1)
    sign = jnp.tile(jnp.concatenate([-jnp.ones(n_freq, F32), jnp.ones(n_freq, F32)]), 2)
    cos = jnp.tile(jnp.cos(ang), (1, 2))
    sin_signed = jnp.tile(jnp.sin(ang) * sign, (1, 2))
    return cos, sin_signed


def _mm_out_even_kernel(a1_ref, a2_ref, w1_ref, w2_ref, x_ref, g_ref, o_ref):
    acc = jnp.dot(a1_ref[...], w1_ref[...], preferred_element_type=F32)
    acc = acc + jnp.dot(a2_ref[...], w2_ref[...], preferred_element_type=F32)
    o_ref[...] = x_ref[...] + g_ref[...] * acc


def mm_out_even(a1, a2, w_out, x, gate, rows_per_mod):
    m = x.shape[0]
    tm = min(1024, rows_per_mod)
    tn = 512
    tpm = rows_per_mod // tm
    return pl.pallas_call(
        _mm_out_even_kernel,
        grid=(m // tm, D_MODEL // tn),
        in_specs=[
            pl.BlockSpec((tm, BRANCH), lambda i, j: (i, 0)),
            pl.BlockSpec((tm, BRANCH), lambda i, j: (i, 0)),
            pl.BlockSpec((BRANCH, tn), lambda i, j: (0, j)),
            pl.BlockSpec((BRANCH, tn), lambda i, j: (1, j)),
            pl.BlockSpec((tm, tn), lambda i, j: (i, j)),
            pl.BlockSpec((None, 1, tn), lambda i, j: (i // tpm, 0, j)),
        ],
        out_specs=pl.BlockSpec((tm, tn), lambda i, j: (i, j)),
        out_shape=jax.ShapeDtypeStruct((m, D_MODEL), F32),
        compiler_params=_cparams(("arbitrary", "arbitrary")),
        name="mm_out_even",
    )(a1, a2, w_out, w_out, x, gate)


def _mm_out_odd_kernel(o_in_ref, z_ref, ng_ref, w_ref, x_ref, g_ref, o_ref, a_ref):
    @pl.when(pl.program_id(1) == 0)
    def _():
        z = z_ref[...].astype(F32)
        sz = z * jax.nn.sigmoid(z)
        for h in range(GLA_HEADS):
            sl = slice(h * GLA_DV, (h + 1) * GLA_DV)
            o = o_in_ref[:, sl].astype(F32)
            y = o * lax.rsqrt(jnp.mean(o * o, axis=-1, keepdims=True) + EPS) * ng_ref[...]
            a_ref[:, sl] = (y * sz[:, sl]).astype(BF16)

    acc = jnp.dot(a_ref[...], w_ref[...], preferred_element_type=F32)
    o_ref[...] = x_ref[...] + g_ref[...] * acc


def mm_out_odd(o_gla, hin, z_block, norm_g, w_out, x, gate, rows_per_mod):
    m = x.shape[0]
    tm = min(1024, rows_per_mod)
    tn = 512
    tpm = rows_per_mod // tm
    return pl.pallas_call(
        _mm_out_odd_kernel,
        grid=(m // tm, D_MODEL // tn),
        in_specs=[
            pl.BlockSpec((tm, GLA_VAL), lambda i, j: (i, 0)),
            pl.BlockSpec((tm, GLA_VAL), lambda i, j: (i, z_block)),
            pl.BlockSpec((1, GLA_DV), lambda i, j: (0, 0)),
            pl.BlockSpec((GLA_VAL, tn), lambda i, j: (0, j)),
            pl.BlockSpec((tm, tn), lambda i, j: (i, j)),
            pl.BlockSpec((None, 1, tn), lambda i, j: (i // tpm, 0, j)),
        ],
        out_specs=pl.BlockSpec((tm, tn), lambda i, j: (i, j)),
        out_shape=jax.ShapeDtypeStruct((m, D_MODEL), F32),
        scratch_shapes=[pltpu.VMEM((tm, GLA_VAL), BF16)],
        compiler_params=_cparams(("arbitrary", "arbitrary")),
        name="mm_out_odd",
    )(o_gla, hin, norm_g, w_out, x, gate)


def _seg_cumsum(x, reverse):
    n = x.shape[0]
    pos = lax.broadcasted_iota(jnp.int32, x.shape, 0) & (GLA_CHUNK - 1)
    s = 1
    while s < GLA_CHUNK:
        if reverse:
            x = x + jnp.where(pos < GLA_CHUNK - s, pltpu.roll(x, n - s, 0), 0.0)
        else:
            x = x + jnp.where(pos >= s, pltpu.roll(x, s, 0), 0.0)
        s *= 2
    return x


def _gla_kernel(ql_ref, kl_ref, vl_ref, rl_ref, qc_ref, kc_ref, vc_ref, rc_ref, wa_ref, ba_ref,
                *rest, ctx_out):
    if ctx_out:
        ol_ref, oc_ref = rest[:2]
        rest = rest[2:]
    else:
        ol_ref, oc_ref = rest[0], None
        rest = rest[1:]
    sf_ref, sb_ref, qd_ref, kd_ref, b_ref, vb_ref, ofl_ref, ofc_ref, obl_ref, obc_ref = rest
    C = GLA_CHUNK
    L, nctx = ql_ref.shape[0], kc_ref.shape[0]
    nl, nc = L // C, nctx // C
    row = lax.broadcasted_iota(jnp.int32, (C, C), 0)
    col = lax.broadcasted_iota(jnp.int32, (C, C), 1)

    for base, n, q_ref, k_ref, v_ref, r_ref in ((0, nctx, qc_ref, kc_ref, vc_ref, rc_ref),
                                                (nctx, L, ql_ref, kl_ref, vl_ref, rl_ref)):
        rows = slice(base, base + n)
        vb_ref[rows, :] = v_ref[...].astype(BF16)
        r = r_ref[...].astype(BF16)
        k = k_ref[...].astype(F32)
        with_q = ctx_out or base > 0
        for d in range(2):
            logits = jnp.dot(r, wa_ref[d], preferred_element_type=F32) + ba_ref[d]
            ls = jnp.minimum(logits, 0.0) - jnp.log(1.0 + jnp.exp(-jnp.abs(logits)))
            b = _seg_cumsum(ls * (1.0 / GLA_TAU), reverse=d == 1)
            b_ref[d, rows, :] = b
            kd_ref[d, rows, :] = (k * jnp.exp(-b)).astype(BF16)
            if with_q:
                qd_ref[d, rows, :] = (q_ref[...].astype(F32) * (GLA_DK ** -0.5) * jnp.exp(b)).astype(BF16)

    sf_ref[...] = jnp.zeros_like(sf_ref)
    sb_ref[...] = jnp.zeros_like(sb_ref)

    def chunk(d, row0, o_ref, orow0):
        rev = d == 1
        s_ref = sb_ref if rev else sf_ref
        incl = (col >= row) if rev else (col <= row)
        rows = pl.ds(row0, C)
        kd = kd_ref[d, rows, :]
        v = vb_ref[rows, :]
        dec = jnp.exp(b_ref[d, pl.ds(row0 if rev else row0 + C - 1, 1), :])
        k_end = (kd.astype(F32) * dec).astype(BF16)
        s_old = s_ref[...]
        if o_ref is not None:
            qd = qd_ref[d, rows, :]
            sc = lax.dot_general(qd, kd, (((1,), (1,)), ((), ())), preferred_element_type=F32)
            sc = jnp.where(incl, sc, 0.0).astype(BF16)
            o = jnp.dot(sc, v, preferred_element_type=F32)
            o = o + jnp.dot(qd, s_old.astype(BF16), preferred_element_type=F32)
            o_ref[pl.ds(orow0, C), :] = o
        dec_col = jnp.transpose(jnp.broadcast_to(dec, (LANES, GLA_DK)))[:, 0:1]
        s_ref[...] = dec_col * s_old + lax.dot_general(k_end, v, (((0,), (0,)), ((), ())),
                                                       preferred_element_type=F32)

    def segment(base, n_chunks, of_ref, ob_ref):
        unroll = 2
        assert n_chunks % unroll == 0

        def body(i, carry):
            for u in range(unroll):
                cf = i * unroll + u
                cb = n_chunks - 1 - cf
                chunk(0, pl.multiple_of(base + cf * C, C), of_ref, pl.multiple_of(cf * C, C))
                chunk(1, pl.multiple_of(base + cb * C, C), ob_ref, pl.multiple_of(cb * C, C))
            return carry

        lax.fori_loop(0, n_chunks // unroll, body, 0)

    segment(0, nc, ofc_ref if ctx_out else None, obc_ref if ctx_out else None)
    segment(nctx, nl, ofl_ref, obl_ref)
    ol_ref[...] = (ofl_ref[...] + obl_ref[...]).astype(ol_ref.dtype)
    if ctx_out:
        oc_ref[...] = (ofc_ref[...] + obc_ref[...]).astype(oc_ref.dtype)


def gla_mix(hin_lat, hin_ctx, r_lat, r_ctx, wa2p, ba, batch, ctx_out):
    ml, mc = hin_lat.shape[0], hin_ctx.shape[0]
    L, nctx = ml // batch, mc // batch
    kb = GLA_KEY // GLA_DK
    vb = 2 * GLA_KEY // GLA_DV
    in_specs = [
        pl.BlockSpec((L, GLA_DK), lambda b, h: (b, h)),
        pl.BlockSpec((L, GLA_DK), lambda b, h: (b, kb + h)),
        pl.BlockSpec((L, GLA_DV), lambda b, h: (b, vb + h)),
        pl.BlockSpec((L, LANES), lambda b, h: (b, 0)),
        pl.BlockSpec((nctx, GLA_DK), lambda b, h: (b, h)),
        pl.BlockSpec((nctx, GLA_DK), lambda b, h: (b, kb + h)),
        pl.BlockSpec((nctx, GLA_DV), lambda b, h: (b, vb + h)),
        pl.BlockSpec((nctx, LANES), lambda b, h: (b, 0)),
        pl.BlockSpec((2, LANES, GLA_DK), lambda b, h: (0, 0, h)),
        pl.BlockSpec((2, 1, GLA_DK), lambda b, h: (0, 0, h)),
    ]
    out_specs = [pl.BlockSpec((L, GLA_DV), lambda b, h: (b, h)),
                 pl.BlockSpec((nctx, GLA_DV), lambda b, h: (b, h))]
    out_shape = [jax.ShapeDtypeStruct((ml, GLA_VAL), BF16), jax.ShapeDtypeStruct((mc, GLA_VAL), BF16)]
    if not ctx_out:
        out_specs, out_shape = out_specs[:1], out_shape[:1]
    outs = pl.pallas_call(
        functools.partial(_gla_kernel, ctx_out=ctx_out),
        grid=(batch, GLA_HEADS),
        in_specs=in_specs,
        out_specs=out_specs,
        out_shape=out_shape,
        scratch_shapes=[
            pltpu.VMEM((GLA_DK, GLA_DV), F32), pltpu.VMEM((GLA_DK, GLA_DV), F32),
            pltpu.VMEM((2, nctx + L, GLA_DK), BF16),
            pltpu.VMEM((2, nctx + L, GLA_DK), BF16),
            pltpu.VMEM((2, nctx + L, GLA_DK), F32),
            pltpu.VMEM((nctx + L, GLA_DV), BF16),
            pltpu.VMEM((L, GLA_DV), F32), pltpu.VMEM((nctx, GLA_DV), F32),
            pltpu.VMEM((L, GLA_DV), F32), pltpu.VMEM((nctx, GLA_DV), F32),
        ],
        compiler_params=_cparams(("arbitrary", "arbitrary")),
        name="gla_mix",
    )(hin_lat, hin_lat, hin_lat, r_lat, hin_ctx, hin_ctx, hin_ctx, r_ctx, wa2p, ba)
    return (outs[0], outs[1]) if ctx_out else (outs[0], None)


def _mods(mod_l, norm_g, batch):
    shift, scale, gate = mod_l[:, :D_MODEL], mod_l[:, D_MODEL:2 * D_MODEL], mod_l[:, 2 * D_MODEL:]
    gs = (norm_g.astype(F32)[None, :] * (1.0 + scale))[:, None, :]
    sh = shift[:, None, :]
    gt = gate[:, None, :]
    lat = (gs[:batch], sh[:batch], gt[:batch])
    ctx = (gs[batch:batch + 1], sh[batch:batch + 1], gt[batch:batch + 1])
    return lat, ctx


def even_layer(x_lat, x_ctx, mod_l, norm_g, w_in, w_out, s5p, w_glu, qn_g, kn_g, lam_vecs, subln_g,
               lam_init, batch, rope):
    L, nctx = x_lat.shape[0] // batch, x_ctx.shape[0] // batch
    (gs_l, sh_l, gt_l), (gs_c, sh_c, gt_c) = _mods(mod_l, norm_g, batch)
    w_in_b = w_in.astype(BF16)
    rest = 5 * BRANCH
    u3_l = mm_in(x_lat, gs_l, sh_l, w_in_b, 0, BRANCH, rows_per_mod=L, out_dtype=BF16, gb_out=True, name="mm_in_u")
    u3_c = mm_in(x_ctx, gs_c, sh_c, w_in_b, 0, BRANCH, rows_per_mod=x_ctx.shape[0], out_dtype=BF16,
                 gb_out=True, name="mm_in_u_ctx")
    hin_l = mm_in(x_lat, gs_l, sh_l, w_in_b, BRANCH, rest, rows_per_mod=L, out_dtype=BF16, name="mm_in_rest")
    hin_c = mm_in(x_ctx, gs_c, sh_c, w_in_b, BRANCH, rest, rows_per_mod=x_ctx.shape[0], out_dtype=BF16,
                  name="mm_in_rest_ctx")

    ez, kl, cl, ptab, sel = s5p
    u3_l = u3_l.reshape(S5_GB, x_lat.shape[0] // S5_T, S5_W)
    u3_c = u3_c.reshape(S5_GB, x_ctx.shape[0] // S5_T, S5_W)
    h_l, h_c = s5_states(u3_l, u3_c, ez, ptab, batch)
    y_l, y_c = s5_outputs(u3_l, u3_c, h_l, h_c, kl, cl, sel)
    wg = w_glu.astype(BF16)
    a_s5_l = s5_finish(y_l.reshape(S5_GB, x_lat.shape[0], LANES), wg, hin_l, 0)
    a_s5_c = s5_finish(y_c.reshape(S5_GB, x_ctx.shape[0], LANES), wg, hin_c, 0)

    lv = lam_vecs.astype(F32)
    lam = jnp.exp(jnp.sum(lv[0] * lv[1])) - jnp.exp(jnp.sum(lv[2] * lv[3])) + lam_init
    lam_row = jnp.full((1, LANES), lam, F32)
    qg = jnp.tile(qn_g.astype(F32), 2)[None, :]
    kg = jnp.tile(kn_g.astype(F32), 2)[None, :]
    sg = subln_g.astype(F32)[None, :]
    cb = BRANCH // LANES
    cos_t, sin_t, ones_bd = rope
    a_da_l, a_da_c = diff_attention(hin_l, hin_c, cb, 2 * cb, 3 * cb, 4 * cb, batch, qg, kg, sg, lam_row,
                                    lam_init, cos_t, sin_t, ones_bd)

    w_out_b = w_out.astype(BF16)
    x_lat = mm_out_even(a_s5_l, a_da_l, w_out_b, x_lat, gt_l, L)
    x_ctx = mm_out_even(a_s5_c, a_da_c, w_out_b, x_ctx, gt_c, x_ctx.shape[0])
    return x_lat, x_ctx


def odd_layer(x_lat, x_ctx, mod_l, norm_g, w_in, w_out, wa1, wa2, ba, gla_norm_g, batch, with_ctx_out):
    L = x_lat.shape[0] // batch
    (gs_l, sh_l, gt_l), (gs_c, sh_c, gt_c) = _mods(mod_l, norm_g, batch)
    w_in_b = w_in.astype(BF16)
    n = 3 * D_MODEL
    w_aux = jnp.zeros((D_MODEL, LANES), F32).at[:, :GLA_RANK].set(wa1[0]).at[:, GLA_RANK:2 * GLA_RANK].set(wa1[1])
    w_aux = w_aux.astype(BF16)
    hin_l, r_l = mm_in(x_lat, gs_l, sh_l, w_in_b, 0, n, rows_per_mod=L, out_dtype=BF16, w_aux=w_aux,
                       name="mm_in_odd")
    hin_c, r_c = mm_in(x_ctx, gs_c, sh_c, w_in_b, 0, n, rows_per_mod=x_ctx.shape[0], out_dtype=BF16,
                       w_aux=w_aux, name="mm_in_odd_ctx")
    wa2p = jnp.zeros((2, LANES, GLA_KEY), F32)
    wa2p = wa2p.at[0, :GLA_RANK].set(wa2[0]).at[1, GLA_RANK:2 * GLA_RANK].set(wa2[1]).astype(BF16)
    o_l, o_c = gla_mix(hin_l, hin_c, r_l, r_c, wa2p, ba.astype(F32).reshape(2, 1, GLA_KEY), batch, with_ctx_out)
    w_out_b = w_out.astype(BF16)
    ng = gla_norm_g.astype(F32)[None, :]
    x_lat = mm_out_odd(o_l, hin_l, 2, ng, w_out_b, x_lat, gt_l, L)
    if with_ctx_out:
        x_ctx = mm_out_odd(o_c, hin_c, 2, ng, w_out_b, x_ctx, gt_c, x_ctx.shape[0])
    return x_lat, x_ctx


def kernel(x, c, ctx, c_ctx, ada_w, ada_b, norm_g, ev_w_in, ev_w_out, s5_a_re, s5_a_im, s5_log_dt, s5_b_re, s5_b_im, s5_c_re, s5_c_im, s5_d, s5_w_glu, da_qn_g, da_kn_g, da_lam, da_subln_g, od_w_in, od_w_out, gla_wa1, gla_wa2, gla_ba, gla_norm_g):
    batch, L, _ = x.shape
    nctx = ctx.shape[1]
    x_lat = x.reshape(batch * L, D_MODEL)
    x_ctx = ctx.reshape(batch * nctx, D_MODEL)
    cond = jnp.zeros((8, D_MODEL), F32).at[:batch].set(c).at[batch].set(c_ctx)
    mod = modulation_all(cond, ada_w, ada_b)

    cos_t, sin_t = rope_tables(L)
    seg = jnp.arange(LANES) // DA_HEAD
    ones_bd = (seg[:, None] == seg[None, :]).astype(F32)
    rope = (cos_t, sin_t, ones_bd)

    for i in range(DEPTH):
        j = i // 2
        with_ctx_out = i < DEPTH - 1
        if i % 2 == 0:
            lam_init = 0.8 - 0.6 * math.exp(-0.3 * i)
            s5p = s5_prepare(s5_a_re[j], s5_a_im[j], s5_log_dt[j], s5_b_re[j], s5_b_im[j], s5_c_re[j],
                             s5_c_im[j], s5_d[j], L // S5_T)
            x_lat, x_ctx = even_layer(x_lat, x_ctx, mod[i], norm_g[i], ev_w_in[j], ev_w_out[j], s5p,
                                      s5_w_glu[j], da_qn_g[j], da_kn_g[j], da_lam[j], da_subln_g[j],
                                      lam_init, batch, rope)
        else:
            x_lat, x_ctx = odd_layer(x_lat, x_ctx, mod[i], norm_g[i], od_w_in[j], od_w_out[j],
                                     gla_wa1[j], gla_wa2[j], gla_ba[j], gla_norm_g[j], batch, with_ctx_out)
    return x_lat.reshape(batch, L, D_MODEL)
```

```python
import functools
import math

import jax
import jax.numpy as jnp
from jax import lax
from jax.experimental import pallas as pl
from jax.experimental.pallas import tpu as pltpu

F32 = jnp.float32
BF16 = jnp.bfloat16

D_MODEL = 2048
DEPTH = 4
GRID_W = 64
EPS = 1e-6
BRANCH = D_MODEL // 2
S5_GROUP = 16
S5_GROUPS = BRANCH // S5_GROUP
S5_STATE = 64
DA_HEAD = 64
DA_HEADS = BRANCH // (2 * DA_HEAD)
DA_VDIM = 2 * DA_HEAD
ROPE_BASE = 10000.0
GLA_HEADS = 4
GLA_KEY = D_MODEL // 2
GLA_VAL = D_MODEL
GLA_DK = GLA_KEY // GLA_HEADS
GLA_DV = GLA_VAL // GLA_HEADS
GLA_RANK = 16
GLA_TAU = 16.0
GLA_CHUNK = 64

LANES = 128
VMEM_LIMIT = 56 * 1024 * 1024

S5_T = 16
S5_GB = BRANCH // LANES
S5_GPB = LANES // S5_GROUP
S5_W = S5_T * LANES
S5_SW = S5_GPB * S5_STATE


def _cparams(sem):
    return pltpu.CompilerParams(dimension_semantics=sem, vmem_limit_bytes=VMEM_LIMIT)


def _mod_kernel(c_ref, w_ref, b_ref, o_ref):
    c = c_ref[...]
    s = (c * jax.nn.sigmoid(c)).astype(BF16)
    acc = jnp.dot(s, w_ref[...].astype(BF16), preferred_element_type=F32)
    o_ref[...] = acc + b_ref[...]


def modulation_all(cond, ada_w, ada_b):
    tn = 512
    n = 3 * D_MODEL
    return pl.pallas_call(
        _mod_kernel,
        grid=(DEPTH, n // tn),
        in_specs=[
            pl.BlockSpec((8, D_MODEL), lambda l, j: (0, 0)),
            pl.BlockSpec((None, D_MODEL, tn), lambda l, j: (l, 0, j)),
            pl.BlockSpec((None, 1, tn), lambda l, j: (l, 0, j)),
        ],
        out_specs=pl.BlockSpec((None, 8, tn), lambda l, j: (l, 0, j)),
        out_shape=jax.ShapeDtypeStruct((DEPTH, 8, n), F32),
        compiler_params=_cparams(("arbitrary", "arbitrary")),
        name="modulation",
    )(cond, ada_w, ada_b.reshape(DEPTH, 1, n))


def _mm_in_kernel(x_ref, gs_ref, sh_ref, w_ref, *rest, gb_out, has_aux):
    if has_aux:
        wa_ref, o_ref, aux_ref, hn_ref = rest
    else:
        o_ref, hn_ref = rest

    @pl.when(pl.program_id(1) == 0)
    def _():
        x = x_ref[...]
        ms = jnp.mean(x * x, axis=-1, keepdims=True)
        hn = x * lax.rsqrt(ms + EPS) * gs_ref[...] + sh_ref[...]
        hn_ref[...] = hn.astype(BF16)
        if has_aux:
            aux_ref[...] = jnp.dot(hn_ref[...], wa_ref[...], preferred_element_type=F32)

    acc = jnp.dot(hn_ref[...], w_ref[...].astype(BF16), preferred_element_type=F32)
    if gb_out:
        for q in range(acc.shape[1] // LANES):
            o_ref[q] = acc[:, q * LANES:(q + 1) * LANES].astype(o_ref.dtype)
    else:
        o_ref[...] = acc.astype(o_ref.dtype)


def mm_in(x, gs, sh, w, col0, ncols, *, rows_per_mod, out_dtype, gb_out=False, w_aux=None, name="mm_in"):
    m = x.shape[0]
    tm = min(1024, rows_per_mod)
    tn = 512
    assert m % tm == 0 and rows_per_mod % tm == 0 and ncols % tn == 0 and col0 % tn == 0
    tpm = rows_per_mod // tm
    jb = col0 // tn
    in_specs = [
        pl.BlockSpec((tm, D_MODEL), lambda i, j: (i, 0)),
        pl.BlockSpec((None, 1, D_MODEL), lambda i, j: (i // tpm, 0, 0)),
        pl.BlockSpec((None, 1, D_MODEL), lambda i, j: (i // tpm, 0, 0)),
        pl.BlockSpec((D_MODEL, tn), lambda i, j: (0, j + jb)),
    ]
    args = [x, gs, sh, w]
    if gb_out:
        out_shape = [jax.ShapeDtypeStruct((ncols // LANES, m, LANES), out_dtype)]
        out_specs = [pl.BlockSpec((tn // LANES, tm, LANES), lambda i, j: (j, i, 0))]
    else:
        out_shape = [jax.ShapeDtypeStruct((m, ncols), out_dtype)]
        out_specs = [pl.BlockSpec((tm, tn), lambda i, j: (i, j))]
    if w_aux is not None:
        in_specs.append(pl.BlockSpec((D_MODEL, LANES), lambda i, j: (0, 0)))
        args.append(w_aux)
        out_shape.append(jax.ShapeDtypeStruct((m, LANES), F32))
        out_specs.append(pl.BlockSpec((tm, LANES), lambda i, j: (i, 0)))
    outs = pl.pallas_call(
        functools.partial(_mm_in_kernel, gb_out=gb_out, has_aux=w_aux is not None),
        grid=(m // tm, ncols // tn),
        in_specs=in_specs,
        out_specs=out_specs,
        out_shape=out_shape,
        scratch_shapes=[pltpu.VMEM((tm, D_MODEL), BF16)],
        compiler_params=_cparams(("arbitrary", "arbitrary")),
        name=name,
    )(*args)
    return outs if w_aux is not None else outs[0]


def _shift_rows(h, s, up):
    n = h.shape[0]
    row = lax.broadcasted_iota(jnp.int32, h.shape, 0)
    if up:
        return jnp.where(row >= n - s, 0.0, pltpu.roll(h, n - s, 0))
    return jnp.where(row < s, 0.0, pltpu.roll(h, s, 0))


def _chunk_scan(zr, zi, pr, pi, reverse):
    n = zr.shape[0]
    ntab = pr.shape[0]
    hr, hi = zr, zi
    s = 1
    while s < n:
        idx = ntab - 1 - s if reverse else s
        ar, ai = pr[idx:idx + 1], pi[idx:idx + 1]
        sr, si = _shift_rows(hr, s, reverse), _shift_rows(hi, s, reverse)
        hr, hi = hr + ar * sr - ai * si, hi + ar * si + ai * sr
        s *= 2
    return hr, hi


def _group_mask(shape, row_shift, col_shift):
    rg = lax.broadcasted_iota(jnp.int32, shape, 0) >> row_shift
    cg = (lax.broadcasted_iota(jnp.int32, shape, 1) >> col_shift) & (S5_GPB - 1)
    return rg == cg


def _s5_state_kernel(xl_ref, xc_ref, ez_ref, p_ref, hl_ref, hc_ref, wz_ref):
    ncc = xc_ref.shape[0]

    @pl.when(pl.program_id(1) == 0)
    def _():
        mask = _group_mask((LANES, S5_SW), 4, 6)
        for t in range(S5_T):
            for q in range(4):
                e = jnp.concatenate([ez_ref[t, q]] * S5_GPB, axis=0)
                wz_ref[t * LANES:(t + 1) * LANES, q * S5_SW:(q + 1) * S5_SW] = jnp.where(
                    mask, e, jnp.zeros_like(e))

    x = jnp.concatenate([xc_ref[...], xl_ref[...]], axis=0)
    z = jnp.dot(x, wz_ref[...], preferred_element_type=F32)
    p = p_ref[...]
    outs_c, outs_l = [], []
    for d in range(2):
        rev = d == 1
        c0 = 2 * d * S5_SW
        zr, zi = z[:, c0:c0 + S5_SW], z[:, c0 + S5_SW:c0 + 2 * S5_SW]
        pr, pi = p[:, c0:c0 + S5_SW], p[:, c0 + S5_SW:c0 + 2 * S5_SW]
        cr, ci = _chunk_scan(zr[:ncc], zi[:ncc], pr, pi, rev)
        lr, li = _chunk_scan(zr[ncc:], zi[ncc:], pr, pi, rev)
        if rev:
            car_r, car_i = cr[0:1], ci[0:1]
        else:
            car_r, car_i = cr[ncc - 1:ncc], ci[ncc - 1:ncc]
        hcr, hci = _shift_rows(cr, 1, rev), _shift_rows(ci, 1, rev)
        hlr = _shift_rows(lr, 1, rev) + pr * car_r - pi * car_i
        hli = _shift_rows(li, 1, rev) + pr * car_i + pi * car_r
        outs_c += [hcr, hci]
        outs_l += [hlr, hli]
    hc_ref[...] = jnp.concatenate(outs_c, axis=1).astype(hc_ref.dtype)
    hl_ref[...] = jnp.concatenate(outs_l, axis=1).astype(hl_ref.dtype)


def s5_states(u3_lat, u3_ctx, ez, ptab, batch):
    ncl = u3_lat.shape[1] // batch
    ncc = u3_ctx.shape[1] // batch
    assert ptab.shape[1] == ncl
    return pl.pallas_call(
        _s5_state_kernel,
        grid=(S5_GB, batch),
        in_specs=[
            pl.BlockSpec((None, ncl, S5_W), lambda g, b: (g, b, 0)),
            pl.BlockSpec((None, ncc, S5_W), lambda g, b: (g, b, 0)),
            pl.BlockSpec((None, S5_T, 4, S5_GROUP, S5_SW), lambda g, b: (g, 0, 0, 0, 0)),
            pl.BlockSpec((None, ncl, 4 * S5_SW), lambda g, b: (g, 0, 0)),
        ],
        out_specs=[
            pl.BlockSpec((None, ncl, 4 * S5_SW), lambda g, b: (g, b, 0)),
            pl.BlockSpec((None, ncc, 4 * S5_SW), lambda g, b: (g, b, 0)),
        ],
        out_shape=[
            jax.ShapeDtypeStruct((S5_GB, u3_lat.shape[1], 4 * S5_SW), BF16),
            jax.ShapeDtypeStruct((S5_GB, u3_ctx.shape[1], 4 * S5_SW), BF16),
        ],
        scratch_shapes=[pltpu.VMEM((S5_W, 4 * S5_SW), BF16)],
        compiler_params=_cparams(("arbitrary", "arbitrary")),
        name="s5_states",
    )(u3_lat, u3_ctx, ez, ptab)


S5_TAPS = 2 * S5_T
S5_SEL_IN = LANES
S5_SEL_OUT = (LANES // S5_GROUP) * LANES


def _s5_out_kernel(xl_ref, xc_ref, hl_ref, hc_ref, kl_ref, cl_ref, sel_ref, yl_ref, yc_ref, wt_ref, wm_ref):
    nl = xl_ref.shape[0]
    sel = sel_ref[...]
    taps = []
    mask_t = _group_mask((LANES, S5_SEL_OUT), 4, 4)
    for a in range(S5_TAPS * S5_GROUP // S5_SEL_IN):
        t = jnp.dot(kl_ref[:, a * S5_SEL_IN:(a + 1) * S5_SEL_IN], sel, preferred_element_type=F32)
        taps.append(jnp.where(mask_t, t, 0.0).astype(BF16))
    taps = jnp.concatenate(taps, axis=1)
    for s in range(S5_T):
        lo = (S5_T - 1 - s) * LANES
        wt_ref[s * LANES:(s + 1) * LANES, :] = taps[:, lo:lo + S5_W]
    mask_m = _group_mask((S5_SW, S5_SEL_OUT), 6, 4)
    for q in range(4):
        for a in range(S5_T * S5_GROUP // S5_SEL_IN):
            m = jnp.dot(cl_ref[q, :, a * S5_SEL_IN:(a + 1) * S5_SEL_IN], sel, preferred_element_type=F32)
            wm_ref[q * S5_SW:(q + 1) * S5_SW, a * S5_SEL_OUT:(a + 1) * S5_SEL_OUT] = jnp.where(
                mask_m, m, 0.0).astype(BF16)
    x = jnp.concatenate([xl_ref[...], xc_ref[...]], axis=0)
    h = jnp.concatenate([hl_ref[...], hc_ref[...]], axis=0)
    y = jnp.dot(x, wt_ref[...], preferred_element_type=F32)
    y = y + jnp.dot(h, wm_ref[...], preferred_element_type=F32)
    yl_ref[...] = y[:nl].astype(yl_ref.dtype)
    yc_ref[...] = y[nl:].astype(yc_ref.dtype)


def s5_outputs(u3_lat, u3_ctx, h_lat, h_ctx, kl, cl, sel):
    nl, nc = u3_lat.shape[1], u3_ctx.shape[1]
    return pl.pallas_call(
        _s5_out_kernel,
        grid=(S5_GB,),
        in_specs=[
            pl.BlockSpec((None, nl, S5_W), lambda g: (g, 0, 0)),
            pl.BlockSpec((None, nc, S5_W), lambda g: (g, 0, 0)),
            pl.BlockSpec((None, nl, 4 * S5_SW), lambda g: (g, 0, 0)),
            pl.BlockSpec((None, nc, 4 * S5_SW), lambda g: (g, 0, 0)),
            pl.BlockSpec((None, LANES, S5_TAPS * S5_GROUP), lambda g: (g, 0, 0)),
            pl.BlockSpec((None, 4, S5_SW, S5_T * S5_GROUP), lambda g: (g, 0, 0, 0)),
            pl.BlockSpec((S5_SEL_IN, S5_SEL_OUT), lambda g: (0, 0)),
        ],
        out_specs=[
            pl.BlockSpec((None, nl, S5_W), lambda g: (g, 0, 0)),
            pl.BlockSpec((None, nc, S5_W), lambda g: (g, 0, 0)),
        ],
        out_shape=[
            jax.ShapeDtypeStruct((S5_GB, nl, S5_W), BF16),
            jax.ShapeDtypeStruct((S5_GB, nc, S5_W), BF16),
        ],
        scratch_shapes=[pltpu.VMEM((S5_W, S5_W), BF16), pltpu.VMEM((4 * S5_SW, S5_W), BF16)],
        compiler_params=_cparams(("arbitrary",)),
        name="s5_outputs",
    )(u3_lat, u3_ctx, h_lat, h_ctx, kl, cl, sel)


def _s5_finish_kernel(y_ref, wg_ref, zs_ref, o_ref):
    y = jnp.concatenate([y_ref[q] for q in range(S5_GB)], axis=1)
    g = jax.nn.gelu(y.astype(F32))
    t = jnp.dot(g.astype(BF16), wg_ref[...], preferred_element_type=F32)
    zs = zs_ref[...].astype(F32)
    o_ref[...] = (g * jax.nn.sigmoid(t) * (zs * jax.nn.sigmoid(zs))).astype(o_ref.dtype)


def s5_finish(y3, w_glu, hin, zs_block):
    m = y3.shape[1]
    tm = min(512, m)
    return pl.pallas_call(
        _s5_finish_kernel,
        grid=(m // tm,),
        in_specs=[
            pl.BlockSpec((S5_GB, tm, LANES), lambda i: (0, i, 0)),
            pl.BlockSpec((BRANCH, BRANCH), lambda i: (0, 0)),
            pl.BlockSpec((tm, BRANCH), lambda i: (i, zs_block)),
        ],
        out_specs=pl.BlockSpec((tm, BRANCH), lambda i: (i, 0)),
        out_shape=jax.ShapeDtypeStruct((m, BRANCH), BF16),
        compiler_params=_cparams(("arbitrary",)),
        name="s5_finish",
    )(y3, w_glu, hin)


def s5_prepare(a_re, a_im, log_dt, b_re, b_im, c_re, c_im, d_skip, ncl):
    T, G, P, C = S5_T, S5_GROUPS, S5_STATE, S5_GROUP
    GB, GPB, SW = S5_GB, S5_GPB, S5_SW
    hp = lax.Precision.HIGHEST
    a_re, a_im = a_re.astype(F32), a_im.astype(F32)
    dt = jnp.exp(log_dt.astype(F32))[..., None]
    la_re, la_im = a_re * dt, a_im * dt
    mag = jnp.exp(la_re)
    lb_re, lb_im = mag * jnp.cos(la_im), mag * jnp.sin(la_im)
    nr, ni = lb_re - 1.0, lb_im
    den = a_re * a_re + a_im * a_im
    f_re = (nr * a_re + ni * a_im) / den
    f_im = (ni * a_re - nr * a_im) / den
    bb_re = f_re[..., None] * b_re - f_im[..., None] * b_im
    bb_im = f_re[..., None] * b_im + f_im[..., None] * b_re

    def cpow(k, lr, li):
        m = jnp.exp(k * lr)
        return m * jnp.cos(k * li), m * jnp.sin(k * li)

    lad_re, lad_im = la_re.reshape(2, GB, 1, SW), la_im.reshape(2, GB, 1, SW)
    tt = jnp.arange(T, dtype=F32).reshape(1, 1, T, 1)
    pw_re, pw_im = cpow(tt, lad_re, lad_im)
    to_lanes = lambda w: w.reshape(2, GB, GPB, P, C).transpose(0, 1, 4, 2, 3).reshape(2, GB, C, SW)
    bt_re, bt_im = to_lanes(bb_re), to_lanes(bb_im)
    e_re = pw_re[:, :, :, None] * bt_re[:, :, None] - pw_im[:, :, :, None] * bt_im[:, :, None]
    e_im = pw_re[:, :, :, None] * bt_im[:, :, None] + pw_im[:, :, :, None] * bt_re[:, :, None]
    ez = jnp.stack([e_re[0][:, ::-1], e_im[0][:, ::-1], e_re[1], e_im[1]], axis=2)

    cr = c_re.astype(F32).reshape(2, GB, GPB, C, P)
    ci = c_im.astype(F32).reshape(2, GB, GPB, C, P)
    e6_re, e6_im = e_re.reshape(2, GB, T, C, GPB, P), e_im.reshape(2, GB, T, C, GPB, P)
    kk = (jnp.einsum("dbgop,dbtigp->dbgito", cr, e6_re, precision=hp)
          - jnp.einsum("dbgop,dbtigp->dbgito", ci, e6_im, precision=hp))
    skip = jnp.eye(C, dtype=F32) * d_skip.astype(F32).reshape(GB, GPB, C, 1)
    center = kk[0][..., 0, :] + kk[1][..., 0, :] + skip
    kl = jnp.concatenate([kk[1][..., :0:-1, :], center[..., None, :], kk[0][..., 1:, :],
                          jnp.zeros((GB, GPB, C, 1, C), F32)], axis=3)
    kl = kl.reshape(GB, LANES, S5_TAPS * C)

    lar_re, lar_im = la_re[..., None], la_im[..., None]
    steps = jnp.stack([jnp.arange(1, T + 1, dtype=F32), jnp.arange(T, 0, -1).astype(F32)])
    pr_re, pr_im = cpow(steps.reshape(2, 1, 1, T), lar_re, lar_im)
    cp_re = c_re.astype(F32).transpose(0, 1, 3, 2)[:, :, :, None, :]
    cp_im = c_im.astype(F32).transpose(0, 1, 3, 2)[:, :, :, None, :]
    cl_re = cp_re * pr_re[..., None] - cp_im * pr_im[..., None]
    cl_im = cp_re * pr_im[..., None] + cp_im * pr_re[..., None]
    cl = jnp.stack([cl_re[0], -cl_im[0], cl_re[1], -cl_im[1]], axis=0)
    cl = cl.reshape(4, GB, SW, T * C).transpose(1, 0, 2, 3)

    kf = jnp.arange(ncl, dtype=F32).reshape(1, ncl, 1) * float(T)
    af_re, af_im = cpow(kf, lad_re[0], lad_im[0])
    ab_re, ab_im = cpow(kf[:, ::-1], lad_re[1], lad_im[1])
    tab = jnp.concatenate([af_re, af_im, ab_re, ab_im], axis=-1)

    r = jnp.arange(S5_SEL_IN)
    q = jnp.arange(S5_SEL_OUT)
    sel = ((r[:, None] // C == q[None, :] // LANES) & (r[:, None] % C == q[None, :] % C)).astype(BF16)
    return ez.astype(BF16), kl.astype(BF16), cl.astype(BF16), tab, sel


MXU_N = 256
LOG2E = 1.4426950408889634


def _seg_mean_sq(x, ones_bd):
    sq = (x * x).astype(BF16)
    parts = [jnp.dot(sq[:, t:t + MXU_N], ones_bd, preferred_element_type=F32)
             for t in range(0, x.shape[1], MXU_N)]
    return jnp.concatenate(parts, axis=1) * (1.0 / DA_HEAD)


def _rope(x, cos, sin_signed):
    n = x.shape[1]
    lane = lax.broadcasted_iota(jnp.int32, x.shape, 1)
    first = (lane % (DA_HEAD // 2)) < (DA_HEAD // 4)
    rot = jnp.where(first, pltpu.roll(x, n - DA_HEAD // 4, 1), pltpu.roll(x, DA_HEAD // 4, 1))
    return x * cos + rot * sin_signed


def _qk_prep_kernel(q_ref, k_ref, *rest, rope):
    if rope:
        cos_ref, sin_ref, qg_ref, kg_ref, ones_ref, o_ref = rest
    else:
        qg_ref, kg_ref, ones_ref, o_ref = rest
    ones_bd = ones_ref[...]
    reps = BRANCH // LANES
    for x_ref, g_ref, scale, c0 in ((q_ref, qg_ref, DA_HEAD ** -0.5 * LOG2E, 0), (k_ref, kg_ref, 1.0, BRANCH)):
        x = x_ref[...].astype(F32)
        x = x * lax.rsqrt(_seg_mean_sq(x, ones_bd) + EPS) * jnp.concatenate([g_ref[...]] * reps, axis=1)
        if rope:
            x = _rope(x, jnp.concatenate([cos_ref[...]] * reps, axis=1),
                      jnp.concatenate([sin_ref[...]] * reps, axis=1))
        o_ref[:, c0:c0 + BRANCH] = (x * scale).astype(o_ref.dtype)


def qk_prep(hin, q_block, k_block, qg, kg, ones_bd, rope_tabs, seq_len):
    m = hin.shape[0]
    tm = min(512, m)
    rope = rope_tabs is not None
    in_specs = [pl.BlockSpec((tm, BRANCH), lambda i: (i, q_block)),
                pl.BlockSpec((tm, BRANCH), lambda i: (i, k_block))]
    args = [hin, hin]
    if rope:
        tps = seq_len // tm
        in_specs += [pl.BlockSpec((tm, LANES), lambda i: (i % tps, 0)),
                     pl.BlockSpec((tm, LANES), lambda i: (i % tps, 0))]
        args += list(rope_tabs)
    in_specs += [pl.BlockSpec((1, LANES), lambda i: (0, 0)), pl.BlockSpec((1, LANES), lambda i: (0, 0)),
                 pl.BlockSpec((MXU_N, MXU_N), lambda i: (0, 0))]
    args += [qg, kg, ones_bd]
    return pl.pallas_call(
        functools.partial(_qk_prep_kernel, rope=rope),
        grid=(m // tm,),
        in_specs=in_specs,
        out_specs=pl.BlockSpec((tm, 2 * BRANCH), lambda i: (i, 0)),
        out_shape=jax.ShapeDtypeStruct((m, 2 * BRANCH), BF16),
        compiler_params=_cparams(("arbitrary",)),
        name="qk_prep",
    )(*args)


def _stack_maps(q):
    lane = lax.broadcasted_iota(jnp.int32, q.shape, 1)
    return jnp.concatenate([jnp.where(lane < DA_HEAD, q, 0.0), jnp.where(lane >= DA_HEAD, q, 0.0)], axis=0)


def _diff_combine(pv, tq, lam, sg, lam_scale, zd):
    o = pv[:, :DA_VDIM] / pv[:, DA_VDIM:]
    o = o[:tq] - lam * o[tq:]
    o = o * lax.rsqrt(jnp.mean(o * o, axis=-1, keepdims=True) + EPS) * sg * lam_scale
    return o * (zd * jax.nn.sigmoid(zd))


def _diff_attn_ctx_kernel(q_ref, kc_ref, vc_ref, zd_ref, sg_ref, lam_ref, o_ref, *, lam_scale):
    v1 = jnp.concatenate([vc_ref[...], jnp.ones_like(vc_ref)], axis=1)
    tq = q_ref.shape[0]
    s = lax.dot_general(_stack_maps(q_ref[...]), kc_ref[...], (((1,), (1,)), ((), ())),
                        preferred_element_type=F32)
    p = jnp.exp2(s - jnp.max(s, axis=-1, keepdims=True))
    pv = jnp.dot(p.astype(BF16), v1, preferred_element_type=F32)
    o_ref[...] = _diff_combine(pv, tq, lam_ref[...], sg_ref[...], lam_scale,
                               zd_ref[...].astype(F32)).astype(o_ref.dtype)


def _diff_attn_lat_kernel(q_ref, kl_ref, vl_ref, kc_ref, vc_ref, zd_ref, sg_ref, lam_ref, o_ref,
                          kn_ref, v1_ref, s0_ref, s1_ref, m0_ref, m1_ref, *, tq, lam_scale):
    nctx = kc_ref.shape[0]
    nq = q_ref.shape[0] // tq

    kn_ref[0:nctx, :] = kc_ref[...]
    kn_ref[nctx:, :] = kl_ref[...]
    v1_ref[0:nctx, :] = jnp.concatenate([vc_ref[...], jnp.ones_like(vc_ref)], axis=1)
    v1_ref[nctx:, :] = jnp.concatenate([vl_ref[...], jnp.ones_like(vl_ref)], axis=1)

    bufs = ((s0_ref, m0_ref), (s1_ref, m1_ref))

    def scores(i, slot):
        s_ref, m_ref = bufs[slot]
        rows = pl.ds(pl.multiple_of(i * tq, tq), tq)
        s = lax.dot_general(_stack_maps(q_ref[rows, :]), kn_ref[...], (((1,), (1,)), ((), ())),
                            preferred_element_type=F32)
        s_ref[...] = s
        m_ref[...] = jnp.broadcast_to(jnp.max(s, axis=-1, keepdims=True), m_ref.shape)

    def finish(i, slot):
        s_ref, m_ref = bufs[slot]
        rows = pl.ds(pl.multiple_of(i * tq, tq), tq)
        p = jnp.exp2(s_ref[...] - m_ref[:, 0:1])
        pv = jnp.dot(p.astype(BF16), v1_ref[...], preferred_element_type=F32)
        o = _diff_combine(pv, tq, lam_ref[...], sg_ref[...], lam_scale, zd_ref[rows, :].astype(F32))
        o_ref[rows, :] = o.astype(o_ref.dtype)

    assert nq % 2 == 0
    scores(0, 0)

    def body(j, carry):
        scores(2 * j + 1, 1)
        finish(2 * j, 0)
        scores(2 * j + 2, 0)
        finish(2 * j + 1, 1)
        return carry

    lax.fori_loop(0, nq // 2 - 1, body, 0)
    scores(nq - 1, 1)
    finish(nq - 2, 0)
    finish(nq - 1, 1)


def diff_attention(hin_lat, hin_ctx, col_q, col_k, col_v, col_zd, batch, qg, kg, sg, lam_row, lam_init,
                   cos_t, sin_t, ones_bd):
    ml, mc = hin_lat.shape[0], hin_ctx.shape[0]
    L, nctx = ml // batch, mc // batch
    tq = min(256, L)
    qk_lat = qk_prep(hin_lat, col_q * LANES // BRANCH, col_k * LANES // BRANCH, qg, kg, ones_bd,
                     (cos_t, sin_t), L)
    qk_ctx = qk_prep(hin_ctx, col_q * LANES // BRANCH, col_k * LANES // BRANCH, qg, kg, ones_bd, None, nctx)
    kcol = BRANCH // LANES
    small = lambda b, h: (0, 0)
    common = [pl.BlockSpec((1, LANES), small), pl.BlockSpec((1, LANES), small)]
    cargs = [sg, lam_row]
    lam_scale = 1.0 - lam_init
    nk = nctx + L
    y_lat = pl.pallas_call(
        functools.partial(_diff_attn_lat_kernel, tq=tq, lam_scale=lam_scale),
        grid=(batch, DA_HEADS),
        in_specs=[
            pl.BlockSpec((L, LANES), lambda b, h: (b, h)),
            pl.BlockSpec((L, LANES), lambda b, h: (b, kcol + h)),
            pl.BlockSpec((L, LANES), lambda b, h: (b, col_v + h)),
            pl.BlockSpec((nctx, LANES), lambda b, h: (b, kcol + h)),
            pl.BlockSpec((nctx, LANES), lambda b, h: (b, col_v + h)),
            pl.BlockSpec((L, LANES), lambda b, h: (b, col_zd + h)),
        ] + common,
        out_specs=pl.BlockSpec((L, LANES), lambda b, h: (b, h)),
        out_shape=jax.ShapeDtypeStruct((ml, BRANCH), BF16),
        scratch_shapes=[
            pltpu.VMEM((nk, LANES), BF16),
            pltpu.VMEM((nk, 2 * DA_VDIM), BF16),
            pltpu.VMEM((2 * tq, nk), F32), pltpu.VMEM((2 * tq, nk), F32),
            pltpu.VMEM((2 * tq, LANES), F32), pltpu.VMEM((2 * tq, LANES), F32),
        ],
        compiler_params=_cparams(("arbitrary", "arbitrary")),
        name="diff_attn_lat",
    )(qk_lat, qk_lat, hin_lat, qk_ctx, hin_ctx, hin_lat, *cargs)
    y_ctx = pl.pallas_call(
        functools.partial(_diff_attn_ctx_kernel, lam_scale=lam_scale),
        grid=(batch, DA_HEADS),
        in_specs=[
            pl.BlockSpec((nctx, LANES), lambda b, h: (b, h)),
            pl.BlockSpec((nctx, LANES), lambda b, h: (b, kcol + h)),
            pl.BlockSpec((nctx, LANES), lambda b, h: (b, col_v + h)),
            pl.BlockSpec((nctx, LANES), lambda b, h: (b, col_zd + h)),
        ] + common,
        out_specs=pl.BlockSpec((nctx, LANES), lambda b, h: (b, h)),
        out_shape=jax.ShapeDtypeStruct((mc, BRANCH), BF16),
        compiler_params=_cparams(("arbitrary", "arbitrary")),
        name="diff_attn_ctx",
    )(qk_ctx, qk_ctx, hin_ctx, hin_ctx, *cargs)
    return y_lat, y_ctx


def seg_ones():
    seg = jnp.arange(MXU_N) // DA_HEAD
    return (seg[:, None] == seg[None, :]).astype(BF16)


def rope_tables(n_tokens):
    rows = n_tokens // GRID_W
    row = jnp.repeat(jnp.arange(rows, dtype=F32), GRID_W)
    col = jnp.tile(jnp.arange(GRID_W, dtype=F32), rows)
    n_freq = DA_HEAD // 4
    inv_freq = ROPE_BASE ** (-jnp.arange(n_freq, dtype=F32) / n_freq)
    ang_r = row[:, None] * inv_freq
    ang_c = col[:, None] * inv_freq
    ang = jnp.concatenate([ang_r, ang_r, ang_c, ang_c], axis=-1)
    sign = jnp.tile(jnp.concatenate([-jnp.ones(n_freq, F32), jnp.ones(n_freq, F32)]), 2)
    cos = jnp.tile(jnp.cos(ang), (1, 2))
    sin_signed = jnp.tile(jnp.sin(ang) * sign, (1, 2))
    return cos, sin_signed


def _mm_out_kernel(*refs, n_parts):
    a_refs, w_refs = refs[:n_parts], refs[n_parts:2 * n_parts]
    x_ref, g_ref, o_ref = refs[2 * n_parts:]
    acc = None
    for a_ref, w_ref in zip(a_refs, w_refs):
        t = jnp.dot(a_ref[...], w_ref[...].astype(BF16), preferred_element_type=F32)
        acc = t if acc is None else acc + t
    o_ref[...] = x_ref[...] + g_ref[...] * acc


def mm_out(a_parts, w_out, x, gate, rows_per_mod):
    m = x.shape[0]
    n_parts = len(a_parts)
    kp = w_out.shape[0] // n_parts
    tm = min(1024, rows_per_mod)
    tn = 512
    tpm = rows_per_mod // tm
    in_specs = [pl.BlockSpec((tm, kp), lambda i, j: (i, 0)) for _ in a_parts]
    in_specs += [pl.BlockSpec((kp, tn), functools.partial(lambda i, j, p: (p, j), p=p)) for p in range(n_parts)]
    in_specs += [pl.BlockSpec((tm, tn), lambda i, j: (i, j)),
                 pl.BlockSpec((None, 1, tn), lambda i, j: (i // tpm, 0, j))]
    return pl.pallas_call(
        functools.partial(_mm_out_kernel, n_parts=n_parts),
        grid=(m // tm, D_MODEL // tn),
        in_specs=in_specs,
        out_specs=pl.BlockSpec((tm, tn), lambda i, j: (i, j)),
        out_shape=jax.ShapeDtypeStruct((m, D_MODEL), F32),
        compiler_params=_cparams(("arbitrary", "arbitrary")),
        name="mm_out",
    )(*a_parts, *([w_out] * n_parts), x, gate)


def _seg_cumsum(x, reverse):
    n = x.shape[0]
    pos = lax.broadcasted_iota(jnp.int32, x.shape, 0) & (GLA_CHUNK - 1)
    s = 1
    while s < GLA_CHUNK:
        if reverse:
            x = x + jnp.where(pos < GLA_CHUNK - s, pltpu.roll(x, n - s, 0), 0.0)
        else:
            x = x + jnp.where(pos >= s, pltpu.roll(x, s, 0), 0.0)
        s *= 2
    return x


def _gla_kernel(ql_ref, kl_ref, vl_ref, rl_ref, zl_ref, qc_ref, kc_ref, vc_ref, rc_ref, zc_ref,
                wa_ref, ba_ref, ng_ref, *rest, ctx_out):
    if ctx_out:
        ol_ref, oc_ref = rest[:2]
        rest = rest[2:]
    else:
        ol_ref, oc_ref = rest[0], None
        rest = rest[1:]
    sf_ref, sb_ref, qd_ref, kd_ref, b_ref, vb_ref, ofl_ref, ofc_ref, obl_ref, obc_ref = rest
    C = GLA_CHUNK
    L, nctx = ql_ref.shape[0], kc_ref.shape[0]
    nl, nc = L // C, nctx // C
    row = lax.broadcasted_iota(jnp.int32, (C, C), 0)
    col = lax.broadcasted_iota(jnp.int32, (C, C), 1)

    for base, n, q_ref, k_ref, v_ref, r_ref in ((0, nctx, qc_ref, kc_ref, vc_ref, rc_ref),
                                                (nctx, L, ql_ref, kl_ref, vl_ref, rl_ref)):
        rows = slice(base, base + n)
        vb_ref[rows, :] = v_ref[...].astype(BF16)
        r = r_ref[...].astype(BF16)
        k = k_ref[...].astype(F32)
        with_q = ctx_out or base > 0
        for d in range(2):
            logits = jnp.dot(r, wa_ref[d], preferred_element_type=F32) + ba_ref[d]
            ls = jnp.minimum(logits, 0.0) - jnp.log(1.0 + jnp.exp(-jnp.abs(logits)))
            b = _seg_cumsum(ls * (1.0 / GLA_TAU), reverse=d == 1)
            b_ref[d, rows, :] = b
            kd_ref[d, rows, :] = (k * jnp.exp(-b)).astype(BF16)
            if with_q:
                qd_ref[d, rows, :] = (q_ref[...].astype(F32) * (GLA_DK ** -0.5) * jnp.exp(b)).astype(BF16)

    sf_ref[...] = jnp.zeros_like(sf_ref)
    sb_ref[...] = jnp.zeros_like(sb_ref)

    def chunk(d, row0, o_ref, orow0):
        rev = d == 1
        s_ref = sb_ref if rev else sf_ref
        incl = (col >= row) if rev else (col <= row)
        rows = pl.ds(row0, C)
        kd = kd_ref[d, rows, :]
        v = vb_ref[rows, :]
        dec = jnp.exp(b_ref[d, pl.ds(row0 if rev else row0 + C - 1, 1), :])
        k_end = (kd.astype(F32) * dec).astype(BF16)
        s_old = s_ref[...]
        if o_ref is not None:
            qd = qd_ref[d, rows, :]
            sc = lax.dot_general(qd, kd, (((1,), (1,)), ((), ())), preferred_element_type=F32)
            sc = jnp.where(incl, sc, 0.0).astype(BF16)
            o = jnp.dot(sc, v, preferred_element_type=F32)
            o = o + jnp.dot(qd, s_old.astype(BF16), preferred_element_type=F32)
            o_ref[pl.ds(orow0, C), :] = o
        dec_col = jnp.transpose(jnp.broadcast_to(dec, (LANES, GLA_DK)))[:, 0:1]
        s_ref[...] = dec_col * s_old + lax.dot_general(k_end, v, (((0,), (0,)), ((), ())),
                                                       preferred_element_type=F32)

    def segment(base, n_chunks, of_ref, ob_ref):
        unroll = 2
        assert n_chunks % unroll == 0

        def body(i, carry):
            for u in range(unroll):
                cf = i * unroll + u
                cb = n_chunks - 1 - cf
                chunk(0, pl.multiple_of(base + cf * C, C), of_ref, pl.multiple_of(cf * C, C))
                chunk(1, pl.multiple_of(base + cb * C, C), ob_ref, pl.multiple_of(cb * C, C))
            return carry

        lax.fori_loop(0, n_chunks // unroll, body, 0)

    segment(0, nc, ofc_ref if ctx_out else None, obc_ref if ctx_out else None)
    segment(nctx, nl, ofl_ref, obl_ref)
    def finish(of_ref, ob_ref, z_ref, o_ref):
        o = of_ref[...] + ob_ref[...]
        z = z_ref[...].astype(F32)
        y = o * lax.rsqrt(jnp.mean(o * o, axis=-1, keepdims=True) + EPS) * ng_ref[...]
        o_ref[...] = (y * (z * jax.nn.sigmoid(z))).astype(o_ref.dtype)

    finish(ofl_ref, obl_ref, zl_ref, ol_ref)
    if ctx_out:
        finish(ofc_ref, obc_ref, zc_ref, oc_ref)


def gla_mix(hin_lat, hin_ctx, r_lat, r_ctx, wa2p, ba, norm_g, batch, ctx_out):
    ml, mc = hin_lat.shape[0], hin_ctx.shape[0]
    L, nctx = ml // batch, mc // batch
    kb = GLA_KEY // GLA_DK
    vb = 2 * GLA_KEY // GLA_DV
    zb = (2 * GLA_KEY + GLA_VAL) // GLA_DV

    def seg_specs(n):
        return [pl.BlockSpec((n, GLA_DK), lambda b, h: (b, h)),
                pl.BlockSpec((n, GLA_DK), lambda b, h: (b, kb + h)),
                pl.BlockSpec((n, GLA_DV), lambda b, h: (b, vb + h)),
                pl.BlockSpec((n, LANES), lambda b, h: (b, 0)),
                pl.BlockSpec((n, GLA_DV), lambda b, h: (b, zb + h))]

    in_specs = seg_specs(L) + seg_specs(nctx) + [
        pl.BlockSpec((2, LANES, GLA_DK), lambda b, h: (0, 0, h)),
        pl.BlockSpec((2, 1, GLA_DK), lambda b, h: (0, 0, h)),
        pl.BlockSpec((1, GLA_DV), lambda b, h: (0, 0)),
    ]
    out_specs = [pl.BlockSpec((L, GLA_DV), lambda b, h: (b, h)),
                 pl.BlockSpec((nctx, GLA_DV), lambda b, h: (b, h))]
    out_shape = [jax.ShapeDtypeStruct((ml, GLA_VAL), BF16), jax.ShapeDtypeStruct((mc, GLA_VAL), BF16)]
    if not ctx_out:
        out_specs, out_shape = out_specs[:1], out_shape[:1]
    outs = pl.pallas_call(
        functools.partial(_gla_kernel, ctx_out=ctx_out),
        grid=(batch, GLA_HEADS),
        in_specs=in_specs,
        out_specs=out_specs,
        out_shape=out_shape,
        scratch_shapes=[
            pltpu.VMEM((GLA_DK, GLA_DV), F32), pltpu.VMEM((GLA_DK, GLA_DV), F32),
            pltpu.VMEM((2, nctx + L, GLA_DK), BF16),
            pltpu.VMEM((2, nctx + L, GLA_DK), BF16),
            pltpu.VMEM((2, nctx + L, GLA_DK), F32),
            pltpu.VMEM((nctx + L, GLA_DV), BF16),
            pltpu.VMEM((L, GLA_DV), F32), pltpu.VMEM((nctx, GLA_DV), F32),
            pltpu.VMEM((L, GLA_DV), F32), pltpu.VMEM((nctx, GLA_DV), F32),
        ],
        compiler_params=_cparams(("arbitrary", "arbitrary")),
        name="gla_mix",
    )(hin_lat, hin_lat, hin_lat, r_lat, hin_lat, hin_ctx, hin_ctx, hin_ctx, r_ctx, hin_ctx, wa2p, ba, norm_g)
    return (outs[0], outs[1]) if ctx_out else (outs[0], None)


def _mods(mod_l, norm_g, batch):
    shift, scale, gate = mod_l[:, :D_MODEL], mod_l[:, D_MODEL:2 * D_MODEL], mod_l[:, 2 * D_MODEL:]
    gs = (norm_g.astype(F32)[None, :] * (1.0 + scale))[:, None, :]
    sh = shift[:, None, :]
    gt = gate[:, None, :]
    lat = (gs[:batch], sh[:batch], gt[:batch])
    ctx = (gs[batch:batch + 1], sh[batch:batch + 1], gt[batch:batch + 1])
    return lat, ctx


def even_layer(x_lat, x_ctx, mod_l, norm_g, w_in, w_out, s5p, w_glu, qn_g, kn_g, lam_vecs, subln_g,
               lam_init, batch, rope):
    L, nctx = x_lat.shape[0] // batch, x_ctx.shape[0] // batch
    (gs_l, sh_l, gt_l), (gs_c, sh_c, gt_c) = _mods(mod_l, norm_g, batch)
    w_in_b = w_in
    rest = 5 * BRANCH
    u3_l = mm_in(x_lat, gs_l, sh_l, w_in_b, 0, BRANCH, rows_per_mod=L, out_dtype=BF16, gb_out=True, name="mm_in_u")
    u3_c = mm_in(x_ctx, gs_c, sh_c, w_in_b, 0, BRANCH, rows_per_mod=x_ctx.shape[0], out_dtype=BF16,
                 gb_out=True, name="mm_in_u_ctx")
    hin_l = mm_in(x_lat, gs_l, sh_l, w_in_b, BRANCH, rest, rows_per_mod=L, out_dtype=BF16, name="mm_in_rest")
    hin_c = mm_in(x_ctx, gs_c, sh_c, w_in_b, BRANCH, rest, rows_per_mod=x_ctx.shape[0], out_dtype=BF16,
                  name="mm_in_rest_ctx")

    ez, kl, cl, ptab, sel = s5p
    u3_l = u3_l.reshape(S5_GB, x_lat.shape[0] // S5_T, S5_W)
    u3_c = u3_c.reshape(S5_GB, x_ctx.shape[0] // S5_T, S5_W)
    h_l, h_c = s5_states(u3_l, u3_c, ez, ptab, batch)
    y_l, y_c = s5_outputs(u3_l, u3_c, h_l, h_c, kl, cl, sel)
    wg = w_glu.astype(BF16)
    a_s5_l = s5_finish(y_l.reshape(S5_GB, x_lat.shape[0], LANES), wg, hin_l, 0)
    a_s5_c = s5_finish(y_c.reshape(S5_GB, x_ctx.shape[0], LANES), wg, hin_c, 0)

    lv = lam_vecs.astype(F32)
    lam = jnp.exp(jnp.sum(lv[0] * lv[1])) - jnp.exp(jnp.sum(lv[2] * lv[3])) + lam_init
    lam_row = jnp.full((1, LANES), lam, F32)
    qg = jnp.tile(qn_g.astype(F32), 2)[None, :]
    kg = jnp.tile(kn_g.astype(F32), 2)[None, :]
    sg = subln_g.astype(F32)[None, :]
    cb = BRANCH // LANES
    cos_t, sin_t, ones_bd = rope
    a_da_l, a_da_c = diff_attention(hin_l, hin_c, cb, 2 * cb, 3 * cb, 4 * cb, batch, qg, kg, sg, lam_row,
                                    lam_init, cos_t, sin_t, ones_bd)

    x_lat = mm_out([a_s5_l, a_da_l], w_out, x_lat, gt_l, L)
    x_ctx = mm_out([a_s5_c, a_da_c], w_out, x_ctx, gt_c, x_ctx.shape[0])
    return x_lat, x_ctx


def odd_layer(x_lat, x_ctx, mod_l, norm_g, w_in, w_out, wa1, wa2, ba, gla_norm_g, batch, with_ctx_out):
    L = x_lat.shape[0] // batch
    (gs_l, sh_l, gt_l), (gs_c, sh_c, gt_c) = _mods(mod_l, norm_g, batch)
    w_in_b = w_in
    n = 3 * D_MODEL
    w_aux = jnp.zeros((D_MODEL, LANES), F32).at[:, :GLA_RANK].set(wa1[0]).at[:, GLA_RANK:2 * GLA_RANK].set(wa1[1])
    w_aux = w_aux.astype(BF16)
    hin_l, r_l = mm_in(x_lat, gs_l, sh_l, w_in_b, 0, n, rows_per_mod=L, out_dtype=BF16, w_aux=w_aux,
                       name="mm_in_odd")
    hin_c, r_c = mm_in(x_ctx, gs_c, sh_c, w_in_b, 0, n, rows_per_mod=x_ctx.shape[0], out_dtype=BF16,
                       w_aux=w_aux, name="mm_in_odd_ctx")
    wa2p = jnp.zeros((2, LANES, GLA_KEY), F32)
    wa2p = wa2p.at[0, :GLA_RANK].set(wa2[0]).at[1, GLA_RANK:2 * GLA_RANK].set(wa2[1]).astype(BF16)
    ng = gla_norm_g.astype(F32)[None, :]
    a_l, a_c = gla_mix(hin_l, hin_c, r_l, r_c, wa2p, ba.astype(F32).reshape(2, 1, GLA_KEY), ng, batch,
                       with_ctx_out)
    x_lat = mm_out([a_l], w_out, x_lat, gt_l, L)
    if with_ctx_out:
        x_ctx = mm_out([a_c], w_out, x_ctx, gt_c, x_ctx.shape[0])
    return x_lat, x_ctx


def kernel(x, c, ctx, c_ctx, ada_w, ada_b, norm_g, ev_w_in, ev_w_out, s5_a_re, s5_a_im, s5_log_dt, s5_b_re, s5_b_im, s5_c_re, s5_c_im, s5_d, s5_w_glu, da_qn_g, da_kn_g, da_lam, da_subln_g, od_w_in, od_w_out, gla_wa1, gla_wa2, gla_ba, gla_norm_g):
    batch, L, _ = x.shape
    nctx = ctx.shape[1]
    x_lat = x.reshape(batch * L, D_MODEL)
    x_ctx = ctx.reshape(batch * nctx, D_MODEL)
    cond = jnp.zeros((8, D_MODEL), F32).at[:batch].set(c).at[batch].set(c_ctx)
    mod = modulation_all(cond, ada_w, ada_b)

    cos_t, sin_t = rope_tables(L)
    rope = (cos_t, sin_t, seg_ones())

    for i in range(DEPTH):
        j = i // 2
        with_ctx_out = i < DEPTH - 1
        if i % 2 == 0:
            lam_init = 0.8 - 0.6 * math.exp(-0.3 * i)
            s5p = s5_prepare(s5_a_re[j], s5_a_im[j], s5_log_dt[j], s5_b_re[j], s5_b_im[j], s5_c_re[j],
                             s5_c_im[j], s5_d[j], L // S5_T)
            x_lat, x_ctx = even_layer(x_lat, x_ctx, mod[i], norm_g[i], ev_w_in[j], ev_w_out[j], s5p,
                                      s5_w_glu[j], da_qn_g[j], da_kn_g[j], da_lam[j], da_subln_g[j],
                                      lam_init, batch, rope)
        else:
            x_lat, x_ctx = odd_layer(x_lat, x_ctx, mod[i], norm_g[i], od_w_in[j], od_w_out[j],
                                     gla_wa1[j], gla_wa2[j], gla_ba[j], gla_norm_g[j], batch, with_ctx_out)
    return x_lat.reshape(batch, L, D_MODEL)
```

```python
import functools
import math

import jax
import jax.numpy as jnp
from jax import lax
from jax.experimental import pallas as pl
from jax.experimental.pallas import tpu as pltpu

F32 = jnp.float32
BF16 = jnp.bfloat16

D_MODEL = 2048
DEPTH = 4
GRID_W = 64
EPS = 1e-6
BRANCH = D_MODEL // 2
S5_GROUP = 16
S5_GROUPS = BRANCH // S5_GROUP
S5_STATE = 64
DA_HEAD = 64
DA_HEADS = BRANCH // (2 * DA_HEAD)
DA_VDIM = 2 * DA_HEAD
ROPE_BASE = 10000.0
GLA_HEADS = 4
GLA_KEY = D_MODEL // 2
GLA_VAL = D_MODEL
GLA_DK = GLA_KEY // GLA_HEADS
GLA_DV = GLA_VAL // GLA_HEADS
GLA_RANK = 16
GLA_TAU = 16.0
GLA_CHUNK = 64

LANES = 128
VMEM_LIMIT = 56 * 1024 * 1024

S5_T = 16
S5_GB = BRANCH // LANES
S5_GPB = LANES // S5_GROUP
S5_W = S5_T * LANES
S5_SW = S5_GPB * S5_STATE


def _cparams(sem):
    return pltpu.CompilerParams(dimension_semantics=sem, vmem_limit_bytes=VMEM_LIMIT)


def _mod_kernel(c_ref, w_ref, b_ref, o_ref):
    c = c_ref[...]
    s = (c * jax.nn.sigmoid(c)).astype(BF16)
    acc = jnp.dot(s, w_ref[...].astype(BF16), preferred_element_type=F32)
    o_ref[...] = acc + b_ref[...]


def modulation_all(cond, ada_w, ada_b):
    tn = 512
    n = 3 * D_MODEL
    return pl.pallas_call(
        _mod_kernel,
        grid=(DEPTH, n // tn),
        in_specs=[
            pl.BlockSpec((8, D_MODEL), lambda l, j: (0, 0)),
            pl.BlockSpec((None, D_MODEL, tn), lambda l, j: (l, 0, j)),
            pl.BlockSpec((None, 1, tn), lambda l, j: (l, 0, j)),
        ],
        out_specs=pl.BlockSpec((None, 8, tn), lambda l, j: (l, 0, j)),
        out_shape=jax.ShapeDtypeStruct((DEPTH, 8, n), F32),
        compiler_params=_cparams(("arbitrary", "arbitrary")),
        name="modulation",
    )(cond, ada_w, ada_b.reshape(DEPTH, 1, n))


def _modulate(x, gs, sh):
    ms = jnp.mean(x * x, axis=-1, keepdims=True)
    return x * lax.rsqrt(ms + EPS) * gs + sh


def _prenorm_kernel(x_ref, gs_ref, sh_ref, o_ref):
    o_ref[...] = _modulate(x_ref[...], gs_ref[...], sh_ref[...]).astype(o_ref.dtype)


def prenorm(x, gs, sh, rows_per_mod):
    m = x.shape[0]
    tm = min(512, rows_per_mod)
    tpm = rows_per_mod // tm
    return pl.pallas_call(
        _prenorm_kernel,
        grid=(m // tm,),
        in_specs=[
            pl.BlockSpec((tm, D_MODEL), lambda i: (i, 0)),
            pl.BlockSpec((None, 1, D_MODEL), lambda i: (i // tpm, 0, 0)),
            pl.BlockSpec((None, 1, D_MODEL), lambda i: (i // tpm, 0, 0)),
        ],
        out_specs=pl.BlockSpec((tm, D_MODEL), lambda i: (i, 0)),
        out_shape=jax.ShapeDtypeStruct((m, D_MODEL), BF16),
        compiler_params=_cparams(("arbitrary",)),
        name="prenorm",
    )(x, gs, sh)


def _mm_in_kernel(a_ref, w_ref, *rest, gb_out, has_aux):
    if has_aux:
        wa_ref, o_ref, aux_ref = rest

        @pl.when(pl.program_id(1) == 0)
        def _():
            aux_ref[...] = jnp.dot(a_ref[...], wa_ref[...], preferred_element_type=F32)
    else:
        (o_ref,) = rest

    acc = jnp.dot(a_ref[...], w_ref[...].astype(BF16), preferred_element_type=F32)
    if gb_out:
        for q in range(acc.shape[1] // LANES):
            o_ref[q] = acc[:, q * LANES:(q + 1) * LANES].astype(o_ref.dtype)
    else:
        o_ref[...] = acc.astype(o_ref.dtype)


def mm_in(a, w_stack, layer, col0, ncols, *, out_dtype, gb_out=False, w_aux=None, name="mm_in"):
    m = a.shape[0]
    tm = min(2048, m)
    tn = 512
    assert m % tm == 0 and ncols % tn == 0 and col0 % tn == 0
    jb = col0 // tn
    in_specs = [
        pl.BlockSpec((tm, D_MODEL), lambda i, j: (i, 0)),
        pl.BlockSpec((None, D_MODEL, tn), lambda i, j: (layer, 0, j + jb)),
    ]
    args = [a, w_stack]
    if gb_out:
        out_shape = [jax.ShapeDtypeStruct((ncols // LANES, m, LANES), out_dtype)]
        out_specs = [pl.BlockSpec((tn // LANES, tm, LANES), lambda i, j: (j, i, 0))]
    else:
        out_shape = [jax.ShapeDtypeStruct((m, ncols), out_dtype)]
        out_specs = [pl.BlockSpec((tm, tn), lambda i, j: (i, j))]
    if w_aux is not None:
        in_specs.append(pl.BlockSpec((D_MODEL, LANES), lambda i, j: (0, 0)))
        args.append(w_aux)
        out_shape.append(jax.ShapeDtypeStruct((m, LANES), F32))
        out_specs.append(pl.BlockSpec((tm, LANES), lambda i, j: (i, 0)))
    outs = pl.pallas_call(
        functools.partial(_mm_in_kernel, gb_out=gb_out, has_aux=w_aux is not None),
        grid=(m // tm, ncols // tn),
        in_specs=in_specs,
        out_specs=out_specs,
        out_shape=out_shape,
        compiler_params=_cparams(("arbitrary", "arbitrary")),
        name=name,
    )(*args)
    return outs if w_aux is not None else outs[0]


def _shift_rows(h, s, up):
    n = h.shape[0]
    row = lax.broadcasted_iota(jnp.int32, h.shape, 0)
    if up:
        return jnp.where(row >= n - s, 0.0, pltpu.roll(h, n - s, 0))
    return jnp.where(row < s, 0.0, pltpu.roll(h, s, 0))


def _chunk_scan(zr, zi, pr, pi, reverse):
    n = zr.shape[0]
    ntab = pr.shape[0]
    hr, hi = zr, zi
    s = 1
    while s < n:
        idx = ntab - 1 - s if reverse else s
        ar, ai = pr[idx:idx + 1], pi[idx:idx + 1]
        sr, si = _shift_rows(hr, s, reverse), _shift_rows(hi, s, reverse)
        hr, hi = hr + ar * sr - ai * si, hi + ar * si + ai * sr
        s *= 2
    return hr, hi


def _group_mask(shape, row_shift, col_shift):
    rg = lax.broadcasted_iota(jnp.int32, shape, 0) >> row_shift
    cg = (lax.broadcasted_iota(jnp.int32, shape, 1) >> col_shift) & (S5_GPB - 1)
    return rg == cg


def _s5_state_kernel(xl_ref, xc_ref, ez_ref, p_ref, hl_ref, hc_ref, wz_ref):
    ncc = xc_ref.shape[0]

    @pl.when(pl.program_id(1) == 0)
    def _():
        mask = _group_mask((LANES, S5_SW), 4, 6)
        for t in range(S5_T):
            for q in range(4):
                e = jnp.concatenate([ez_ref[t, q]] * S5_GPB, axis=0)
                wz_ref[t * LANES:(t + 1) * LANES, q * S5_SW:(q + 1) * S5_SW] = jnp.where(
                    mask, e, jnp.zeros_like(e))

    x = jnp.concatenate([xc_ref[...], xl_ref[...]], axis=0)
    z = jnp.dot(x, wz_ref[...], preferred_element_type=F32)
    p = p_ref[...]
    outs_c, outs_l = [], []
    for d in range(2):
        rev = d == 1
        c0 = 2 * d * S5_SW
        zr, zi = z[:, c0:c0 + S5_SW], z[:, c0 + S5_SW:c0 + 2 * S5_SW]
        pr, pi = p[:, c0:c0 + S5_SW], p[:, c0 + S5_SW:c0 + 2 * S5_SW]
        cr, ci = _chunk_scan(zr[:ncc], zi[:ncc], pr, pi, rev)
        lr, li = _chunk_scan(zr[ncc:], zi[ncc:], pr, pi, rev)
        if rev:
            car_r, car_i = cr[0:1], ci[0:1]
        else:
            car_r, car_i = cr[ncc - 1:ncc], ci[ncc - 1:ncc]
        hcr, hci = _shift_rows(cr, 1, rev), _shift_rows(ci, 1, rev)
        hlr = _shift_rows(lr, 1, rev) + pr * car_r - pi * car_i
        hli = _shift_rows(li, 1, rev) + pr * car_i + pi * car_r
        outs_c += [hcr, hci]
        outs_l += [hlr, hli]
    hc_ref[...] = jnp.concatenate(outs_c, axis=1).astype(hc_ref.dtype)
    hl_ref[...] = jnp.concatenate(outs_l, axis=1).astype(hl_ref.dtype)


def s5_states(u3_lat, u3_ctx, ez, ptab, batch):
    ncl = u3_lat.shape[1] // batch
    ncc = u3_ctx.shape[1] // batch
    assert ptab.shape[1] == ncl
    return pl.pallas_call(
        _s5_state_kernel,
        grid=(S5_GB, batch),
        in_specs=[
            pl.BlockSpec((None, ncl, S5_W), lambda g, b: (g, b, 0)),
            pl.BlockSpec((None, ncc, S5_W), lambda g, b: (g, b, 0)),
            pl.BlockSpec((None, S5_T, 4, S5_GROUP, S5_SW), lambda g, b: (g, 0, 0, 0, 0)),
            pl.BlockSpec((None, ncl, 4 * S5_SW), lambda g, b: (g, 0, 0)),
        ],
        out_specs=[
            pl.BlockSpec((None, ncl, 4 * S5_SW), lambda g, b: (g, b, 0)),
            pl.BlockSpec((None, ncc, 4 * S5_SW), lambda g, b: (g, b, 0)),
        ],
        out_shape=[
            jax.ShapeDtypeStruct((S5_GB, u3_lat.shape[1], 4 * S5_SW), BF16),
            jax.ShapeDtypeStruct((S5_GB, u3_ctx.shape[1], 4 * S5_SW), BF16),
        ],
        scratch_shapes=[pltpu.VMEM((S5_W, 4 * S5_SW), BF16)],
        compiler_params=_cparams(("arbitrary", "arbitrary")),
        name="s5_states",
    )(u3_lat, u3_ctx, ez, ptab)


S5_TAPS = 2 * S5_T
S5_SEL_IN = LANES
S5_SEL_OUT = (LANES // S5_GROUP) * LANES


def _s5_out_kernel(xl_ref, xc_ref, hl_ref, hc_ref, kl_ref, cl_ref, sel_ref, yl_ref, yc_ref, wt_ref, wm_ref):
    nl = xl_ref.shape[0]
    sel = sel_ref[...]
    taps = []
    mask_t = _group_mask((LANES, S5_SEL_OUT), 4, 4)
    for a in range(S5_TAPS * S5_GROUP // S5_SEL_IN):
        t = jnp.dot(kl_ref[:, a * S5_SEL_IN:(a + 1) * S5_SEL_IN], sel, preferred_element_type=F32)
        taps.append(jnp.where(mask_t, t, 0.0).astype(BF16))
    taps = jnp.concatenate(taps, axis=1)
    for s in range(S5_T):
        lo = (S5_T - 1 - s) * LANES
        wt_ref[s * LANES:(s + 1) * LANES, :] = taps[:, lo:lo + S5_W]
    mask_m = _group_mask((S5_SW, S5_SEL_OUT), 6, 4)
    for q in range(4):
        for a in range(S5_T * S5_GROUP // S5_SEL_IN):
            m = jnp.dot(cl_ref[q, :, a * S5_SEL_IN:(a + 1) * S5_SEL_IN], sel, preferred_element_type=F32)
            wm_ref[q * S5_SW:(q + 1) * S5_SW, a * S5_SEL_OUT:(a + 1) * S5_SEL_OUT] = jnp.where(
                mask_m, m, 0.0).astype(BF16)
    x = jnp.concatenate([xl_ref[...], xc_ref[...]], axis=0)
    h = jnp.concatenate([hl_ref[...], hc_ref[...]], axis=0)
    y = jnp.dot(x, wt_ref[...], preferred_element_type=F32)
    y = y + jnp.dot(h, wm_ref[...], preferred_element_type=F32)
    yl_ref[...] = y[:nl].astype(yl_ref.dtype)
    yc_ref[...] = y[nl:].astype(yc_ref.dtype)


def s5_outputs(u3_lat, u3_ctx, h_lat, h_ctx, kl, cl, sel):
    nl, nc = u3_lat.shape[1], u3_ctx.shape[1]
    return pl.pallas_call(
        _s5_out_kernel,
        grid=(S5_GB,),
        in_specs=[
            pl.BlockSpec((None, nl, S5_W), lambda g: (g, 0, 0)),
            pl.BlockSpec((None, nc, S5_W), lambda g: (g, 0, 0)),
            pl.BlockSpec((None, nl, 4 * S5_SW), lambda g: (g, 0, 0)),
            pl.BlockSpec((None, nc, 4 * S5_SW), lambda g: (g, 0, 0)),
            pl.BlockSpec((None, LANES, S5_TAPS * S5_GROUP), lambda g: (g, 0, 0)),
            pl.BlockSpec((None, 4, S5_SW, S5_T * S5_GROUP), lambda g: (g, 0, 0, 0)),
            pl.BlockSpec((S5_SEL_IN, S5_SEL_OUT), lambda g: (0, 0)),
        ],
        out_specs=[
            pl.BlockSpec((None, nl, S5_W), lambda g: (g, 0, 0)),
            pl.BlockSpec((None, nc, S5_W), lambda g: (g, 0, 0)),
        ],
        out_shape=[
            jax.ShapeDtypeStruct((S5_GB, nl, S5_W), BF16),
            jax.ShapeDtypeStruct((S5_GB, nc, S5_W), BF16),
        ],
        scratch_shapes=[pltpu.VMEM((S5_W, S5_W), BF16), pltpu.VMEM((4 * S5_SW, S5_W), BF16)],
        compiler_params=_cparams(("arbitrary",)),
        name="s5_outputs",
    )(u3_lat, u3_ctx, h_lat, h_ctx, kl, cl, sel)


def _s5_finish_kernel(y_ref, wg_ref, zs_ref, o_ref):
    y = jnp.concatenate([y_ref[q] for q in range(S5_GB)], axis=1)
    g = jax.nn.gelu(y.astype(F32))
    t = jnp.dot(g.astype(BF16), wg_ref[...], preferred_element_type=F32)
    zs = zs_ref[...].astype(F32)
    o_ref[...] = (g * jax.nn.sigmoid(t) * (zs * jax.nn.sigmoid(zs))).astype(o_ref.dtype)


def s5_finish(y3, w_glu, hin, zs_block):
    m = y3.shape[1]
    tm = min(512, m)
    return pl.pallas_call(
        _s5_finish_kernel,
        grid=(m // tm,),
        in_specs=[
            pl.BlockSpec((S5_GB, tm, LANES), lambda i: (0, i, 0)),
            pl.BlockSpec((BRANCH, BRANCH), lambda i: (0, 0)),
            pl.BlockSpec((tm, BRANCH), lambda i: (i, zs_block)),
        ],
        out_specs=pl.BlockSpec((tm, BRANCH), lambda i: (i, 0)),
        out_shape=jax.ShapeDtypeStruct((m, BRANCH), BF16),
        compiler_params=_cparams(("arbitrary",)),
        name="s5_finish",
    )(y3, w_glu, hin)


def s5_prepare(a_re, a_im, log_dt, b_re, b_im, c_re, c_im, d_skip, ncl):
    T, G, P, C = S5_T, S5_GROUPS, S5_STATE, S5_GROUP
    GB, GPB, SW = S5_GB, S5_GPB, S5_SW
    hp = lax.Precision.HIGHEST
    a_re, a_im = a_re.astype(F32), a_im.astype(F32)
    dt = jnp.exp(log_dt.astype(F32))[..., None]
    la_re, la_im = a_re * dt, a_im * dt
    mag = jnp.exp(la_re)
    lb_re, lb_im = mag * jnp.cos(la_im), mag * jnp.sin(la_im)
    nr, ni = lb_re - 1.0, lb_im
    den = a_re * a_re + a_im * a_im
    f_re = (nr * a_re + ni * a_im) / den
    f_im = (ni * a_re - nr * a_im) / den
    bb_re = f_re[..., None] * b_re - f_im[..., None] * b_im
    bb_im = f_re[..., None] * b_im + f_im[..., None] * b_re

    def cpow(k, lr, li):
        m = jnp.exp(k * lr)
        return m * jnp.cos(k * li), m * jnp.sin(k * li)

    lad_re, lad_im = la_re.reshape(2, GB, 1, SW), la_im.reshape(2, GB, 1, SW)
    tt = jnp.arange(T, dtype=F32).reshape(1, 1, T, 1)
    pw_re, pw_im = cpow(tt, lad_re, lad_im)
    to_lanes = lambda w: w.reshape(2, GB, GPB, P, C).transpose(0, 1, 4, 2, 3).reshape(2, GB, C, SW)
    bt_re, bt_im = to_lanes(bb_re), to_lanes(bb_im)
    e_re = pw_re[:, :, :, None] * bt_re[:, :, None] - pw_im[:, :, :, None] * bt_im[:, :, None]
    e_im = pw_re[:, :, :, None] * bt_im[:, :, None] + pw_im[:, :, :, None] * bt_re[:, :, None]
    ez = jnp.stack([e_re[0][:, ::-1], e_im[0][:, ::-1], e_re[1], e_im[1]], axis=2)

    cr = c_re.astype(F32).reshape(2, GB, GPB, C, P)
    ci = c_im.astype(F32).reshape(2, GB, GPB, C, P)
    e6_re, e6_im = e_re.reshape(2, GB, T, C, GPB, P), e_im.reshape(2, GB, T, C, GPB, P)
    kk = (jnp.einsum("dbgop,dbtigp->dbgito", cr, e6_re, precision=hp)
          - jnp.einsum("dbgop,dbtigp->dbgito", ci, e6_im, precision=hp))
    skip = jnp.eye(C, dtype=F32) * d_skip.astype(F32).reshape(GB, GPB, C, 1)
    center = kk[0][..., 0, :] + kk[1][..., 0, :] + skip
    kl = jnp.concatenate([kk[1][..., :0:-1, :], center[..., None, :], kk[0][..., 1:, :],
                          jnp.zeros((GB, GPB, C, 1, C), F32)], axis=3)
    kl = kl.reshape(GB, LANES, S5_TAPS * C)

    lar_re, lar_im = la_re[..., None], la_im[..., None]
    steps = jnp.stack([jnp.arange(1, T + 1, dtype=F32), jnp.arange(T, 0, -1).astype(F32)])
    pr_re, pr_im = cpow(steps.reshape(2, 1, 1, T), lar_re, lar_im)
    cp_re = c_re.astype(F32).transpose(0, 1, 3, 2)[:, :, :, None, :]
    cp_im = c_im.astype(F32).transpose(0, 1, 3, 2)[:, :, :, None, :]
    cl_re = cp_re * pr_re[..., None] - cp_im * pr_im[..., None]
    cl_im = cp_re * pr_im[..., None] + cp_im * pr_re[..., None]
    cl = jnp.stack([cl_re[0], -cl_im[0], cl_re[1], -cl_im[1]], axis=0)
    cl = cl.reshape(4, GB, SW, T * C).transpose(1, 0, 2, 3)

    kf = jnp.arange(ncl, dtype=F32).reshape(1, ncl, 1) * float(T)
    af_re, af_im = cpow(kf, lad_re[0], lad_im[0])
    ab_re, ab_im = cpow(kf[:, ::-1], lad_re[1], lad_im[1])
    tab = jnp.concatenate([af_re, af_im, ab_re, ab_im], axis=-1)

    r = jnp.arange(S5_SEL_IN)
    q = jnp.arange(S5_SEL_OUT)
    sel = ((r[:, None] // C == q[None, :] // LANES) & (r[:, None] % C == q[None, :] % C)).astype(BF16)
    return ez.astype(BF16), kl.astype(BF16), cl.astype(BF16), tab, sel


MXU_N = 256
LOG2E = 1.4426950408889634


def _seg_mean_sq(x, ones_bd):
    sq = (x * x).astype(BF16)
    parts = [jnp.dot(sq[:, t:t + MXU_N], ones_bd, preferred_element_type=F32)
             for t in range(0, x.shape[1], MXU_N)]
    return jnp.concatenate(parts, axis=1) * (1.0 / DA_HEAD)


def _rope(x, cos, sin_signed):
    n = x.shape[1]
    lane = lax.broadcasted_iota(jnp.int32, x.shape, 1)
    first = (lane % (DA_HEAD // 2)) < (DA_HEAD // 4)
    rot = jnp.where(first, pltpu.roll(x, n - DA_HEAD // 4, 1), pltpu.roll(x, DA_HEAD // 4, 1))
    return x * cos + rot * sin_signed


def _qk_prep_kernel(q_ref, k_ref, *rest, rope):
    if rope:
        cos_ref, sin_ref, qg_ref, kg_ref, ones_ref, o_ref = rest
    else:
        qg_ref, kg_ref, ones_ref, o_ref = rest
    ones_bd = ones_ref[...]
    reps = BRANCH // LANES
    for x_ref, g_ref, scale, c0 in ((q_ref, qg_ref, DA_HEAD ** -0.5 * LOG2E, 0), (k_ref, kg_ref, 1.0, BRANCH)):
        x = x_ref[...].astype(F32)
        x = x * lax.rsqrt(_seg_mean_sq(x, ones_bd) + EPS) * jnp.concatenate([g_ref[...]] * reps, axis=1)
        if rope:
            x = _rope(x, jnp.concatenate([cos_ref[...]] * reps, axis=1),
                      jnp.concatenate([sin_ref[...]] * reps, axis=1))
        o_ref[:, c0:c0 + BRANCH] = (x * scale).astype(o_ref.dtype)


def qk_prep(hin, q_block, k_block, qg, kg, ones_bd, rope_tabs, seq_len):
    m = hin.shape[0]
    tm = min(512, m)
    rope = rope_tabs is not None
    in_specs = [pl.BlockSpec((tm, BRANCH), lambda i: (i, q_block)),
                pl.BlockSpec((tm, BRANCH), lambda i: (i, k_block))]
    args = [hin, hin]
    if rope:
        tps = seq_len // tm
        in_specs += [pl.BlockSpec((tm, LANES), lambda i: (i % tps, 0)),
                     pl.BlockSpec((tm, LANES), lambda i: (i % tps, 0))]
        args += list(rope_tabs)
    in_specs += [pl.BlockSpec((1, LANES), lambda i: (0, 0)), pl.BlockSpec((1, LANES), lambda i: (0, 0)),
                 pl.BlockSpec((MXU_N, MXU_N), lambda i: (0, 0))]
    args += [qg, kg, ones_bd]
    return pl.pallas_call(
        functools.partial(_qk_prep_kernel, rope=rope),
        grid=(m // tm,),
        in_specs=in_specs,
        out_specs=pl.BlockSpec((tm, 2 * BRANCH), lambda i: (i, 0)),
        out_shape=jax.ShapeDtypeStruct((m, 2 * BRANCH), BF16),
        compiler_params=_cparams(("arbitrary",)),
        name="qk_prep",
    )(*args)


def _stack_maps(q):
    lane = lax.broadcasted_iota(jnp.int32, q.shape, 1)
    return jnp.concatenate([jnp.where(lane < DA_HEAD, q, 0.0), jnp.where(lane >= DA_HEAD, q, 0.0)], axis=0)


def _diff_combine(pv, tq, lam, sg, lam_scale, zd):
    o = pv[:, :DA_VDIM] / pv[:, DA_VDIM:]
    o = o[:tq] - lam * o[tq:]
    o = o * lax.rsqrt(jnp.mean(o * o, axis=-1, keepdims=True) + EPS) * sg * lam_scale
    return o * (zd * jax.nn.sigmoid(zd))


def _diff_attn_ctx_kernel(q_ref, kc_ref, vc_ref, zd_ref, sg_ref, lam_ref, o_ref, *, lam_scale):
    v1 = jnp.concatenate([vc_ref[...], jnp.ones_like(vc_ref)], axis=1)
    tq = q_ref.shape[0]
    s = lax.dot_general(_stack_maps(q_ref[...]), kc_ref[...], (((1,), (1,)), ((), ())),
                        preferred_element_type=F32)
    p = jnp.exp2(s - jnp.max(s, axis=-1, keepdims=True))
    pv = jnp.dot(p.astype(BF16), v1, preferred_element_type=F32)
    o_ref[...] = _diff_combine(pv, tq, lam_ref[...], sg_ref[...], lam_scale,
                               zd_ref[...].astype(F32)).astype(o_ref.dtype)


def _diff_attn_lat_kernel(q_ref, kl_ref, vl_ref, kc_ref, vc_ref, zd_ref, sg_ref, lam_ref, o_ref,
                          kn_ref, v1_ref, s0_ref, s1_ref, m0_ref, m1_ref, *, tq, lam_scale):
    nctx = kc_ref.shape[0]
    nq = q_ref.shape[0] // tq

    kn_ref[0:nctx, :] = kc_ref[...]
    kn_ref[nctx:, :] = kl_ref[...]
    v1_ref[0:nctx, :] = jnp.concatenate([vc_ref[...], jnp.ones_like(vc_ref)], axis=1)
    v1_ref[nctx:, :] = jnp.concatenate([vl_ref[...], jnp.ones_like(vl_ref)], axis=1)

    bufs = ((s0_ref, m0_ref), (s1_ref, m1_ref))

    def scores(i, slot):
        s_ref, m_ref = bufs[slot]
        rows = pl.ds(pl.multiple_of(i * tq, tq), tq)
        s = lax.dot_general(_stack_maps(q_ref[rows, :]), kn_ref[...], (((1,), (1,)), ((), ())),
                            preferred_element_type=F32)
        s_ref[...] = s
        m_ref[...] = jnp.broadcast_to(jnp.max(s, axis=-1, keepdims=True), m_ref.shape)

    def finish(i, slot):
        s_ref, m_ref = bufs[slot]
        rows = pl.ds(pl.multiple_of(i * tq, tq), tq)
        p = jnp.exp2(s_ref[...] - m_ref[:, 0:1])
        pv = jnp.dot(p.astype(BF16), v1_ref[...], preferred_element_type=F32)
        o = _diff_combine(pv, tq, lam_ref[...], sg_ref[...], lam_scale, zd_ref[rows, :].astype(F32))
        o_ref[rows, :] = o.astype(o_ref.dtype)

    assert nq % 2 == 0
    scores(0, 0)

    def body(j, carry):
        scores(2 * j + 1, 1)
        finish(2 * j, 0)
        scores(2 * j + 2, 0)
        finish(2 * j + 1, 1)
        return carry

    lax.fori_loop(0, nq // 2 - 1, body, 0)
    scores(nq - 1, 1)
    finish(nq - 2, 0)
    finish(nq - 1, 1)


def diff_attention(hin_lat, hin_ctx, col_q, col_k, col_v, col_zd, batch, qg, kg, sg, lam_row, lam_init,
                   cos_t, sin_t, ones_bd):
    ml, mc = hin_lat.shape[0], hin_ctx.shape[0]
    L, nctx = ml // batch, mc // batch
    tq = min(256, L)
    qk_lat = qk_prep(hin_lat, col_q * LANES // BRANCH, col_k * LANES // BRANCH, qg, kg, ones_bd,
                     (cos_t, sin_t), L)
    qk_ctx = qk_prep(hin_ctx, col_q * LANES // BRANCH, col_k * LANES // BRANCH, qg, kg, ones_bd, None, nctx)
    kcol = BRANCH // LANES
    small = lambda b, h: (0, 0)
    common = [pl.BlockSpec((1, LANES), small), pl.BlockSpec((1, LANES), small)]
    cargs = [sg, lam_row]
    lam_scale = 1.0 - lam_init
    nk = nctx + L
    y_lat = pl.pallas_call(
        functools.partial(_diff_attn_lat_kernel, tq=tq, lam_scale=lam_scale),
        grid=(batch, DA_HEADS),
        in_specs=[
            pl.BlockSpec((L, LANES), lambda b, h: (b, h)),
            pl.BlockSpec((L, LANES), lambda b, h: (b, kcol + h)),
            pl.BlockSpec((L, LANES), lambda b, h: (b, col_v + h)),
            pl.BlockSpec((nctx, LANES), lambda b, h: (b, kcol + h)),
            pl.BlockSpec((nctx, LANES), lambda b, h: (b, col_v + h)),
            pl.BlockSpec((L, LANES), lambda b, h: (b, col_zd + h)),
        ] + common,
        out_specs=pl.BlockSpec((L, LANES), lambda b, h: (b, h)),
        out_shape=jax.ShapeDtypeStruct((ml, BRANCH), BF16),
        scratch_shapes=[
            pltpu.VMEM((nk, LANES), BF16),
            pltpu.VMEM((nk, 2 * DA_VDIM), BF16),
            pltpu.VMEM((2 * tq, nk), F32), pltpu.VMEM((2 * tq, nk), F32),
            pltpu.VMEM((2 * tq, LANES), F32), pltpu.VMEM((2 * tq, LANES), F32),
        ],
        compiler_params=_cparams(("arbitrary", "arbitrary")),
        name="diff_attn_lat",
    )(qk_lat, qk_lat, hin_lat, qk_ctx, hin_ctx, hin_lat, *cargs)
    y_ctx = pl.pallas_call(
        functools.partial(_diff_attn_ctx_kernel, lam_scale=lam_scale),
        grid=(batch, DA_HEADS),
        in_specs=[
            pl.BlockSpec((nctx, LANES), lambda b, h: (b, h)),
            pl.BlockSpec((nctx, LANES), lambda b, h: (b, kcol + h)),
            pl.BlockSpec((nctx, LANES), lambda b, h: (b, col_v + h)),
            pl.BlockSpec((nctx, LANES), lambda b, h: (b, col_zd + h)),
        ] + common,
        out_specs=pl.BlockSpec((nctx, LANES), lambda b, h: (b, h)),
        out_shape=jax.ShapeDtypeStruct((mc, BRANCH), BF16),
        compiler_params=_cparams(("arbitrary", "arbitrary")),
        name="diff_attn_ctx",
    )(qk_ctx, qk_ctx, hin_ctx, hin_ctx, *cargs)
    return y_lat, y_ctx


def seg_ones():
    seg = jnp.arange(MXU_N) // DA_HEAD
    return (seg[:, None] == seg[None, :]).astype(BF16)


def rope_tables(n_tokens):
    rows = n_tokens // GRID_W
    row = jnp.repeat(jnp.arange(rows, dtype=F32), GRID_W)
    col = jnp.tile(jnp.arange(GRID_W, dtype=F32), rows)
    n_freq = DA_HEAD // 4
    inv_freq = ROPE_BASE ** (-jnp.arange(n_freq, dtype=F32) / n_freq)
    ang_r = row[:, None] * inv_freq
    ang_c = col[:, None] * inv_freq
    ang = jnp.concatenate([ang_r, ang_r, ang_c, ang_c], axis=-1)
    sign = jnp.tile(jnp.concatenate([-jnp.ones(n_freq, F32), jnp.ones(n_freq, F32)]), 2)
    cos = jnp.tile(jnp.cos(ang), (1, 2))
    sin_signed = jnp.tile(jnp.sin(ang) * sign, (1, 2))
    return cos, sin_signed


def _mm_out_kernel(*refs, n_parts, with_next):
    a_refs, w_refs = refs[:n_parts], refs[n_parts:2 * n_parts]
    rest = refs[2 * n_parts:]
    if with_next:
        x_ref, g_ref, gs_ref, sh_ref, o_ref, hn_ref = rest
    else:
        x_ref, g_ref, o_ref = rest
    acc = None
    for a_ref, w_ref in zip(a_refs, w_refs):
        t = jnp.dot(a_ref[...], w_ref[...], preferred_element_type=F32)
        acc = t if acc is None else acc + t
    xn = x_ref[...] + g_ref[...] * acc
    o_ref[...] = xn
    if with_next:
        hn_ref[...] = _modulate(xn, gs_ref[...], sh_ref[...]).astype(hn_ref.dtype)


def mm_out(a_parts, w_stack, layer, x, gate, next_mod, rows_per_mod):
    m = x.shape[0]
    n_parts = len(a_parts)
    kp = w_stack.shape[1] // n_parts
    tm = min(512, rows_per_mod)
    tpm = rows_per_mod // tm
    mod_spec = pl.BlockSpec((None, 1, D_MODEL), lambda i: (i // tpm, 0, 0))
    in_specs = [pl.BlockSpec((tm, kp), lambda i: (i, 0)) for _ in a_parts]
    in_specs += [pl.BlockSpec((None, kp, D_MODEL), functools.partial(lambda i, p: (layer, p, 0), p=p))
                 for p in range(n_parts)]
    in_specs += [pl.BlockSpec((tm, D_MODEL), lambda i: (i, 0)), mod_spec]
    args = [*a_parts, *([w_stack] * n_parts), x, gate]
    out_specs = [pl.BlockSpec((tm, D_MODEL), lambda i: (i, 0))]
    out_shape = [jax.ShapeDtypeStruct((m, D_MODEL), F32)]
    if next_mod is not None:
        in_specs += [mod_spec, mod_spec]
        args += list(next_mod)
        out_specs.append(pl.BlockSpec((tm, D_MODEL), lambda i: (i, 0)))
        out_shape.append(jax.ShapeDtypeStruct((m, D_MODEL), BF16))
    outs = pl.pallas_call(
        functools.partial(_mm_out_kernel, n_parts=n_parts, with_next=next_mod is not None),
        grid=(m // tm,),
        in_specs=in_specs,
        out_specs=out_specs,
        out_shape=out_shape,
        compiler_params=_cparams(("arbitrary",)),
        name="mm_out",
    )(*args)
    return (outs[0], outs[1]) if next_mod is not None else (outs[0], None)


def _seg_cumsum(x, reverse):
    n = x.shape[0]
    pos = lax.broadcasted_iota(jnp.int32, x.shape, 0) & (GLA_CHUNK - 1)
    s = 1
    while s < GLA_CHUNK:
        if reverse:
            x = x + jnp.where(pos < GLA_CHUNK - s, pltpu.roll(x, n - s, 0), 0.0)
        else:
            x = x + jnp.where(pos >= s, pltpu.roll(x, s, 0), 0.0)
        s *= 2
    return x


def _gla_kernel(ql_ref, kl_ref, vl_ref, rl_ref, zl_ref, qc_ref, kc_ref, vc_ref, rc_ref, zc_ref,
                wa_ref, ba_ref, ng_ref, *rest, ctx_out):
    if ctx_out:
        ol_ref, oc_ref = rest[:2]
        rest = rest[2:]
    else:
        ol_ref, oc_ref = rest[0], None
        rest = rest[1:]
    sf_ref, sb_ref, qd_ref, kd_ref, b_ref, vb_ref, ofl_ref, ofc_ref, obl_ref, obc_ref = rest
    C = GLA_CHUNK
    L, nctx = ql_ref.shape[0], kc_ref.shape[0]
    nl, nc = L // C, nctx // C
    row = lax.broadcasted_iota(jnp.int32, (C, C), 0)
    col = lax.broadcasted_iota(jnp.int32, (C, C), 1)

    for base, n, q_ref, k_ref, v_ref, r_ref in ((0, nctx, qc_ref, kc_ref, vc_ref, rc_ref),
                                                (nctx, L, ql_ref, kl_ref, vl_ref, rl_ref)):
        rows = slice(base, base + n)
        vb_ref[rows, :] = v_ref[...].astype(BF16)
        r = r_ref[...].astype(BF16)
        k = k_ref[...].astype(F32)
        with_q = ctx_out or base > 0
        for d in range(2):
            logits = jnp.dot(r, wa_ref[d], preferred_element_type=F32) + ba_ref[d]
            ls = jnp.minimum(logits, 0.0) - jnp.log(1.0 + jnp.exp(-jnp.abs(logits)))
            b = _seg_cumsum(ls * (1.0 / GLA_TAU), reverse=d == 1)
            b_ref[d, rows, :] = b
            kd_ref[d, rows, :] = (k * jnp.exp(-b)).astype(BF16)
            if with_q:
                qd_ref[d, rows, :] = (q_ref[...].astype(F32) * (GLA_DK ** -0.5) * jnp.exp(b)).astype(BF16)

    sf_ref[...] = jnp.zeros_like(sf_ref)
    sb_ref[...] = jnp.zeros_like(sb_ref)

    def chunk(d, row0, o_ref, orow0):
        rev = d == 1
        s_ref = sb_ref if rev else sf_ref
        incl = (col >= row) if rev else (col <= row)
        rows = pl.ds(row0, C)
        kd = kd_ref[d, rows, :]
        v = vb_ref[rows, :]
        dec = jnp.exp(b_ref[d, pl.ds(row0 if rev else row0 + C - 1, 1), :])
        k_end = (kd.astype(F32) * dec).astype(BF16)
        s_old = s_ref[...]
        if o_ref is not None:
            qd = qd_ref[d, rows, :]
            sc = lax.dot_general(qd, kd, (((1,), (1,)), ((), ())), preferred_element_type=F32)
            sc = jnp.where(incl, sc, 0.0).astype(BF16)
            o = jnp.dot(sc, v, preferred_element_type=F32)
            o = o + jnp.dot(qd, s_old.astype(BF16), preferred_element_type=F32)
            o_ref[pl.ds(orow0, C), :] = o
        dec_col = jnp.transpose(jnp.broadcast_to(dec, (LANES, GLA_DK)))[:, 0:1]
        s_ref[...] = dec_col * s_old + lax.dot_general(k_end, v, (((0,), (0,)), ((), ())),
                                                       preferred_element_type=F32)

    def segment(base, n_chunks, of_ref, ob_ref):
        unroll = 2
        assert n_chunks % unroll == 0

        def body(i, carry):
            for u in range(unroll):
                cf = i * unroll + u
                cb = n_chunks - 1 - cf
                chunk(0, pl.multiple_of(base + cf * C, C), of_ref, pl.multiple_of(cf * C, C))
                chunk(1, pl.multiple_of(base + cb * C, C), ob_ref, pl.multiple_of(cb * C, C))
            return carry

        lax.fori_loop(0, n_chunks // unroll, body, 0)

    segment(0, nc, ofc_ref if ctx_out else None, obc_ref if ctx_out else None)
    segment(nctx, nl, ofl_ref, obl_ref)
    def finish(of_ref, ob_ref, z_ref, o_ref):
        o = of_ref[...] + ob_ref[...]
        z = z_ref[...].astype(F32)
        y = o * lax.rsqrt(jnp.mean(o * o, axis=-1, keepdims=True) + EPS) * ng_ref[...]
        o_ref[...] = (y * (z * jax.nn.sigmoid(z))).astype(o_ref.dtype)

    finish(ofl_ref, obl_ref, zl_ref, ol_ref)
    if ctx_out:
        finish(ofc_ref, obc_ref, zc_ref, oc_ref)


def gla_mix(hin_lat, hin_ctx, r_lat, r_ctx, wa2p, ba, norm_g, batch, ctx_out):
    ml, mc = hin_lat.shape[0], hin_ctx.shape[0]
    L, nctx = ml // batch, mc // batch
    kb = GLA_KEY // GLA_DK
    vb = 2 * GLA_KEY // GLA_DV
    zb = (2 * GLA_KEY + GLA_VAL) // GLA_DV

    def seg_specs(n):
        return [pl.BlockSpec((n, GLA_DK), lambda b, h: (b, h)),
                pl.BlockSpec((n, GLA_DK), lambda b, h: (b, kb + h)),
                pl.BlockSpec((n, GLA_DV), lambda b, h: (b, vb + h)),
                pl.BlockSpec((n, LANES), lambda b, h: (b, 0)),
                pl.BlockSpec((n, GLA_DV), lambda b, h: (b, zb + h))]

    in_specs = seg_specs(L) + seg_specs(nctx) + [
        pl.BlockSpec((2, LANES, GLA_DK), lambda b, h: (0, 0, h)),
        pl.BlockSpec((2, 1, GLA_DK), lambda b, h: (0, 0, h)),
        pl.BlockSpec((1, GLA_DV), lambda b, h: (0, 0)),
    ]
    out_specs = [pl.BlockSpec((L, GLA_DV), lambda b, h: (b, h)),
                 pl.BlockSpec((nctx, GLA_DV), lambda b, h: (b, h))]
    out_shape = [jax.ShapeDtypeStruct((ml, GLA_VAL), BF16), jax.ShapeDtypeStruct((mc, GLA_VAL), BF16)]
    if not ctx_out:
        out_specs, out_shape = out_specs[:1], out_shape[:1]
    outs = pl.pallas_call(
        functools.partial(_gla_kernel, ctx_out=ctx_out),
        grid=(batch, GLA_HEADS),
        in_specs=in_specs,
        out_specs=out_specs,
        out_shape=out_shape,
        scratch_shapes=[
            pltpu.VMEM((GLA_DK, GLA_DV), F32), pltpu.VMEM((GLA_DK, GLA_DV), F32),
            pltpu.VMEM((2, nctx + L, GLA_DK), BF16),
            pltpu.VMEM((2, nctx + L, GLA_DK), BF16),
            pltpu.VMEM((2, nctx + L, GLA_DK), F32),
            pltpu.VMEM((nctx + L, GLA_DV), BF16),
            pltpu.VMEM((L, GLA_DV), F32), pltpu.VMEM((nctx, GLA_DV), F32),
            pltpu.VMEM((L, GLA_DV), F32), pltpu.VMEM((nctx, GLA_DV), F32),
        ],
        compiler_params=_cparams(("arbitrary", "arbitrary")),
        name="gla_mix",
    )(hin_lat, hin_lat, hin_lat, r_lat, hin_lat, hin_ctx, hin_ctx, hin_ctx, r_ctx, hin_ctx, wa2p, ba, norm_g)
    return (outs[0], outs[1]) if ctx_out else (outs[0], None)


def _mods(mod_l, norm_g, batch):
    shift, scale, gate = mod_l[:, :D_MODEL], mod_l[:, D_MODEL:2 * D_MODEL], mod_l[:, 2 * D_MODEL:]
    gs = (norm_g.astype(F32)[None, :] * (1.0 + scale))[:, None, :]
    sh = shift[:, None, :]
    gt = gate[:, None, :]
    lat = (gs[:batch], sh[:batch], gt[:batch])
    ctx = (gs[batch:batch + 1], sh[batch:batch + 1], gt[batch:batch + 1])
    return lat, ctx


def even_layer(x_lat, x_ctx, hn_l, hn_c, gates, next_mods, w_in, w_out, j, s5p, w_glu, qn_g, kn_g,
               lam_vecs, subln_g, lam_init, batch, rope):
    L = x_lat.shape[0] // batch
    gt_l, gt_c = gates
    rest = 5 * BRANCH
    u3_l = mm_in(hn_l, w_in, j, 0, BRANCH, out_dtype=BF16, gb_out=True, name="mm_in_u")
    u3_c = mm_in(hn_c, w_in, j, 0, BRANCH, out_dtype=BF16, gb_out=True, name="mm_in_u_ctx")
    hin_l = mm_in(hn_l, w_in, j, BRANCH, rest, out_dtype=BF16, name="mm_in_rest")
    hin_c = mm_in(hn_c, w_in, j, BRANCH, rest, out_dtype=BF16, name="mm_in_rest_ctx")

    ez, kl, cl, ptab, sel = s5p
    u3_l = u3_l.reshape(S5_GB, x_lat.shape[0] // S5_T, S5_W)
    u3_c = u3_c.reshape(S5_GB, x_ctx.shape[0] // S5_T, S5_W)
    h_l, h_c = s5_states(u3_l, u3_c, ez, ptab, batch)
    y_l, y_c = s5_outputs(u3_l, u3_c, h_l, h_c, kl, cl, sel)
    wg = w_glu.astype(BF16)
    a_s5_l = s5_finish(y_l.reshape(S5_GB, x_lat.shape[0], LANES), wg, hin_l, 0)
    a_s5_c = s5_finish(y_c.reshape(S5_GB, x_ctx.shape[0], LANES), wg, hin_c, 0)

    lv = lam_vecs.astype(F32)
    lam = jnp.exp(jnp.sum(lv[0] * lv[1])) - jnp.exp(jnp.sum(lv[2] * lv[3])) + lam_init
    lam_row = jnp.full((1, LANES), lam, F32)
    qg = jnp.tile(qn_g.astype(F32), 2)[None, :]
    kg = jnp.tile(kn_g.astype(F32), 2)[None, :]
    sg = subln_g.astype(F32)[None, :]
    cb = BRANCH // LANES
    cos_t, sin_t, ones_bd = rope
    a_da_l, a_da_c = diff_attention(hin_l, hin_c, cb, 2 * cb, 3 * cb, 4 * cb, batch, qg, kg, sg, lam_row,
                                    lam_init, cos_t, sin_t, ones_bd)

    nm_l, nm_c = next_mods
    x_lat, hn_l = mm_out([a_s5_l, a_da_l], w_out, j, x_lat, gt_l, nm_l, L)
    x_ctx, hn_c = mm_out([a_s5_c, a_da_c], w_out, j, x_ctx, gt_c, nm_c, x_ctx.shape[0])
    return x_lat, x_ctx, hn_l, hn_c


def odd_layer(x_lat, x_ctx, hn_l, hn_c, gates, next_mods, w_in, w_out, j, wa1, wa2, ba, gla_norm_g, batch,
              with_ctx_out):
    L = x_lat.shape[0] // batch
    gt_l, gt_c = gates
    n = 3 * D_MODEL
    w_aux = jnp.zeros((D_MODEL, LANES), F32).at[:, :GLA_RANK].set(wa1[0]).at[:, GLA_RANK:2 * GLA_RANK].set(wa1[1])
    w_aux = w_aux.astype(BF16)
    hin_l, r_l = mm_in(hn_l, w_in, j, 0, n, out_dtype=BF16, w_aux=w_aux, name="mm_in_odd")
    hin_c, r_c = mm_in(hn_c, w_in, j, 0, n, out_dtype=BF16, w_aux=w_aux, name="mm_in_odd_ctx")
    wa2p = jnp.zeros((2, LANES, GLA_KEY), F32)
    wa2p = wa2p.at[0, :GLA_RANK].set(wa2[0]).at[1, GLA_RANK:2 * GLA_RANK].set(wa2[1]).astype(BF16)
    ng = gla_norm_g.astype(F32)[None, :]
    a_l, a_c = gla_mix(hin_l, hin_c, r_l, r_c, wa2p, ba.astype(F32).reshape(2, 1, GLA_KEY), ng, batch,
                       with_ctx_out)
    nm_l, nm_c = next_mods
    x_lat, hn_l = mm_out([a_l], w_out, j, x_lat, gt_l, nm_l, L)
    if with_ctx_out:
        x_ctx, hn_c = mm_out([a_c], w_out, j, x_ctx, gt_c, nm_c, x_ctx.shape[0])
    return x_lat, x_ctx, hn_l, hn_c


def kernel(x, c, ctx, c_ctx, ada_w, ada_b, norm_g, ev_w_in, ev_w_out, s5_a_re, s5_a_im, s5_log_dt, s5_b_re, s5_b_im, s5_c_re, s5_c_im, s5_d, s5_w_glu, da_qn_g, da_kn_g, da_lam, da_subln_g, od_w_in, od_w_out, gla_wa1, gla_wa2, gla_ba, gla_norm_g):
    batch, L, _ = x.shape
    nctx = ctx.shape[1]
    x_lat = x.reshape(batch * L, D_MODEL)
    x_ctx = ctx.reshape(batch * nctx, D_MODEL)
    cond = jnp.zeros((8, D_MODEL), F32).at[:batch].set(c).at[batch].set(c_ctx)
    mod = modulation_all(cond, ada_w, ada_b)

    cos_t, sin_t = rope_tables(L)
    rope = (cos_t, sin_t, seg_ones())
    mods = [_mods(mod[i], norm_g[i], batch) for i in range(DEPTH)]
    ev_w_out_b, od_w_out_b = ev_w_out.astype(BF16), od_w_out.astype(BF16)

    (gs_l, sh_l, _), (gs_c, sh_c, _) = mods[0]
    hn_l = prenorm(x_lat, gs_l, sh_l, L)
    hn_c = prenorm(x_ctx, gs_c, sh_c, x_ctx.shape[0])
    for i in range(DEPTH):
        j = i // 2
        with_ctx_out = i < DEPTH - 1
        gates = (mods[i][0][2], mods[i][1][2])
        next_mods = (mods[i + 1][0][:2], mods[i + 1][1][:2]) if with_ctx_out else (None, None)
        if i % 2 == 0:
            lam_init = 0.8 - 0.6 * math.exp(-0.3 * i)
            s5p = s5_prepare(s5_a_re[j], s5_a_im[j], s5_log_dt[j], s5_b_re[j], s5_b_im[j], s5_c_re[j],
                             s5_c_im[j], s5_d[j], L // S5_T)
            x_lat, x_ctx, hn_l, hn_c = even_layer(
                x_lat, x_ctx, hn_l, hn_c, gates, next_mods, ev_w_in, ev_w_out_b, j, s5p, s5_w_glu[j],
                da_qn_g[j], da_kn_g[j], da_lam[j], da_subln_g[j], lam_init, batch, rope)
        else:
            x_lat, x_ctx, hn_l, hn_c = odd_layer(
                x_lat, x_ctx, hn_l, hn_c, gates, next_mods, od_w_in, od_w_out_b, j, gla_wa1[j], gla_wa2[j],
                gla_ba[j], gla_norm_g[j], batch, with_ctx_out)
    return x_lat.reshape(batch, L, D_MODEL)
```

```python
import functools
import math

import jax
import jax.numpy as jnp
from jax import lax
from jax.experimental import pallas as pl
from jax.experimental.pallas import tpu as pltpu

F32 = jnp.float32
BF16 = jnp.bfloat16

D_MODEL = 2048
DEPTH = 4
GRID_W = 64
EPS = 1e-6
BRANCH = D_MODEL // 2
S5_GROUP = 16
S5_GROUPS = BRANCH // S5_GROUP
S5_STATE = 64
DA_HEAD = 64
DA_HEADS = BRANCH // (2 * DA_HEAD)
DA_VDIM = 2 * DA_HEAD
ROPE_BASE = 10000.0
GLA_HEADS = 4
GLA_KEY = D_MODEL // 2
GLA_VAL = D_MODEL
GLA_DK = GLA_KEY // GLA_HEADS
GLA_DV = GLA_VAL // GLA_HEADS
GLA_RANK = 16
GLA_TAU = 16.0
GLA_CHUNK = 64

LANES = 128
VMEM_LIMIT = 56 * 1024 * 1024

S5_T = 16
S5_GB = BRANCH // LANES
S5_GPB = LANES // S5_GROUP
S5_W = S5_T * LANES
S5_SW = S5_GPB * S5_STATE


def _cparams(sem):
    return pltpu.CompilerParams(dimension_semantics=sem, vmem_limit_bytes=VMEM_LIMIT)


def _mod_kernel(c_ref, w_ref, b_ref, o_ref):
    c = c_ref[...]
    s = (c * jax.nn.sigmoid(c)).astype(BF16)
    acc = jnp.dot(s, w_ref[...].astype(BF16), preferred_element_type=F32)
    o_ref[...] = acc + b_ref[...]


def modulation_all(cond, ada_w, ada_b):
    tn = 512
    n = 3 * D_MODEL
    return pl.pallas_call(
        _mod_kernel,
        grid=(DEPTH, n // tn),
        in_specs=[
            pl.BlockSpec((8, D_MODEL), lambda l, j: (0, 0)),
            pl.BlockSpec((None, D_MODEL, tn), lambda l, j: (l, 0, j)),
            pl.BlockSpec((None, 1, tn), lambda l, j: (l, 0, j)),
        ],
        out_specs=pl.BlockSpec((None, 8, tn), lambda l, j: (l, 0, j)),
        out_shape=jax.ShapeDtypeStruct((DEPTH, 8, n), F32),
        compiler_params=_cparams(("arbitrary", "arbitrary")),
        name="modulation",
    )(cond, ada_w, ada_b.reshape(DEPTH, 1, n))


def _modulate(x, gs, sh):
    ms = jnp.mean(x * x, axis=-1, keepdims=True)
    return x * lax.rsqrt(ms + EPS) * gs + sh


def _prenorm_kernel(x_ref, gs_ref, sh_ref, o_ref):
    o_ref[...] = _modulate(x_ref[...], gs_ref[...], sh_ref[...]).astype(o_ref.dtype)


def prenorm(x, gs, sh, rows_per_mod):
    m = x.shape[0]
    tm = min(512, rows_per_mod)
    tpm = rows_per_mod // tm
    return pl.pallas_call(
        _prenorm_kernel,
        grid=(m // tm,),
        in_specs=[
            pl.BlockSpec((tm, D_MODEL), lambda i: (i, 0)),
            pl.BlockSpec((None, 1, D_MODEL), lambda i: (i // tpm, 0, 0)),
            pl.BlockSpec((None, 1, D_MODEL), lambda i: (i // tpm, 0, 0)),
        ],
        out_specs=pl.BlockSpec((tm, D_MODEL), lambda i: (i, 0)),
        out_shape=jax.ShapeDtypeStruct((m, D_MODEL), BF16),
        compiler_params=_cparams(("arbitrary",)),
        name="prenorm",
    )(x, gs, sh)


def _mm_in_kernel(a_ref, w_ref, *rest, gb_out, has_aux):
    if has_aux:
        wa_ref, o_ref, aux_ref = rest

        @pl.when(pl.program_id(1) == 0)
        def _():
            aux_ref[...] = jnp.dot(a_ref[...], wa_ref[...], preferred_element_type=F32)
    else:
        (o_ref,) = rest

    acc = jnp.dot(a_ref[...], w_ref[...].astype(BF16), preferred_element_type=F32)
    if gb_out:
        for q in range(acc.shape[1] // LANES):
            o_ref[q] = acc[:, q * LANES:(q + 1) * LANES].astype(o_ref.dtype)
    else:
        o_ref[...] = acc.astype(o_ref.dtype)


def mm_in(a, w_stack, layer, col0, ncols, *, out_dtype, gb_out=False, w_aux=None, name="mm_in"):
    m = a.shape[0]
    tm = min(2048, m)
    tn = 512
    assert m % tm == 0 and ncols % tn == 0 and col0 % tn == 0
    jb = col0 // tn
    in_specs = [
        pl.BlockSpec((tm, D_MODEL), lambda i, j: (i, 0)),
        pl.BlockSpec((None, D_MODEL, tn), lambda i, j: (layer, 0, j + jb)),
    ]
    args = [a, w_stack]
    if gb_out:
        out_shape = [jax.ShapeDtypeStruct((ncols // LANES, m, LANES), out_dtype)]
        out_specs = [pl.BlockSpec((tn // LANES, tm, LANES), lambda i, j: (j, i, 0))]
    else:
        out_shape = [jax.ShapeDtypeStruct((m, ncols), out_dtype)]
        out_specs = [pl.BlockSpec((tm, tn), lambda i, j: (i, j))]
    if w_aux is not None:
        in_specs.append(pl.BlockSpec((D_MODEL, LANES), lambda i, j: (0, 0)))
        args.append(w_aux)
        out_shape.append(jax.ShapeDtypeStruct((m, LANES), F32))
        out_specs.append(pl.BlockSpec((tm, LANES), lambda i, j: (i, 0)))
    outs = pl.pallas_call(
        functools.partial(_mm_in_kernel, gb_out=gb_out, has_aux=w_aux is not None),
        grid=(m // tm, ncols // tn),
        in_specs=in_specs,
        out_specs=out_specs,
        out_shape=out_shape,
        compiler_params=_cparams(("arbitrary", "arbitrary")),
        name=name,
    )(*args)
    return outs if w_aux is not None else outs[0]


def _shift_rows(h, s, up):
    n = h.shape[0]
    row = lax.broadcasted_iota(jnp.int32, h.shape, 0)
    if up:
        return jnp.where(row >= n - s, 0.0, pltpu.roll(h, n - s, 0))
    return jnp.where(row < s, 0.0, pltpu.roll(h, s, 0))


def _chunk_scan(zr, zi, pr, pi, reverse):
    n = zr.shape[0]
    ntab = pr.shape[0]
    hr, hi = zr, zi
    s = 1
    while s < n:
        idx = ntab - 1 - s if reverse else s
        ar, ai = pr[idx:idx + 1], pi[idx:idx + 1]
        sr, si = _shift_rows(hr, s, reverse), _shift_rows(hi, s, reverse)
        hr, hi = hr + ar * sr - ai * si, hi + ar * si + ai * sr
        s *= 2
    return hr, hi


def _group_mask(shape, row_shift, col_shift):
    rg = lax.broadcasted_iota(jnp.int32, shape, 0) >> row_shift
    cg = (lax.broadcasted_iota(jnp.int32, shape, 1) >> col_shift) & (S5_GPB - 1)
    return rg == cg


def _s5_state_kernel(xl_ref, xc_ref, ez_ref, p_ref, hl_ref, hc_ref, wz_ref):
    ncc = xc_ref.shape[0]

    @pl.when(pl.program_id(1) == 0)
    def _():
        mask = _group_mask((LANES, S5_SW), 4, 6)
        for t in range(S5_T):
            for q in range(4):
                e = jnp.concatenate([ez_ref[t, q]] * S5_GPB, axis=0)
                wz_ref[t * LANES:(t + 1) * LANES, q * S5_SW:(q + 1) * S5_SW] = jnp.where(
                    mask, e, jnp.zeros_like(e))

    x = jnp.concatenate([xc_ref[...], xl_ref[...]], axis=0)
    z = jnp.dot(x, wz_ref[...], preferred_element_type=F32)
    p = p_ref[...]
    outs_c, outs_l = [], []
    for d in range(2):
        rev = d == 1
        c0 = 2 * d * S5_SW
        zr, zi = z[:, c0:c0 + S5_SW], z[:, c0 + S5_SW:c0 + 2 * S5_SW]
        pr, pi = p[:, c0:c0 + S5_SW], p[:, c0 + S5_SW:c0 + 2 * S5_SW]
        cr, ci = _chunk_scan(zr[:ncc], zi[:ncc], pr, pi, rev)
        lr, li = _chunk_scan(zr[ncc:], zi[ncc:], pr, pi, rev)
        if rev:
            car_r, car_i = cr[0:1], ci[0:1]
        else:
            car_r, car_i = cr[ncc - 1:ncc], ci[ncc - 1:ncc]
        hcr, hci = _shift_rows(cr, 1, rev), _shift_rows(ci, 1, rev)
        hlr = _shift_rows(lr, 1, rev) + pr * car_r - pi * car_i
        hli = _shift_rows(li, 1, rev) + pr * car_i + pi * car_r
        outs_c += [hcr, hci]
        outs_l += [hlr, hli]
    hc_ref[...] = jnp.concatenate(outs_c, axis=1).astype(hc_ref.dtype)
    hl_ref[...] = jnp.concatenate(outs_l, axis=1).astype(hl_ref.dtype)


def s5_states(u3_lat, u3_ctx, ez, ptab, layer, batch):
    ncl = u3_lat.shape[1] // batch
    ncc = u3_ctx.shape[1] // batch
    assert ptab.shape[2] == ncl
    return pl.pallas_call(
        _s5_state_kernel,
        grid=(S5_GB, batch),
        in_specs=[
            pl.BlockSpec((None, ncl, S5_W), lambda g, b: (g, b, 0)),
            pl.BlockSpec((None, ncc, S5_W), lambda g, b: (g, b, 0)),
            pl.BlockSpec((None, None, S5_T, 4, S5_GROUP, S5_SW), lambda g, b: (layer, g, 0, 0, 0, 0)),
            pl.BlockSpec((None, None, ncl, 4 * S5_SW), lambda g, b: (layer, g, 0, 0)),
        ],
        out_specs=[
            pl.BlockSpec((None, ncl, 4 * S5_SW), lambda g, b: (g, b, 0)),
            pl.BlockSpec((None, ncc, 4 * S5_SW), lambda g, b: (g, b, 0)),
        ],
        out_shape=[
            jax.ShapeDtypeStruct((S5_GB, u3_lat.shape[1], 4 * S5_SW), BF16),
            jax.ShapeDtypeStruct((S5_GB, u3_ctx.shape[1], 4 * S5_SW), BF16),
        ],
        scratch_shapes=[pltpu.VMEM((S5_W, 4 * S5_SW), BF16)],
        compiler_params=_cparams(("arbitrary", "arbitrary")),
        name="s5_states",
    )(u3_lat, u3_ctx, ez, ptab)


S5_TAPS = 2 * S5_T
S5_SEL_IN = LANES
S5_SEL_OUT = (LANES // S5_GROUP) * LANES


def _s5_out_kernel(xl_ref, xc_ref, hl_ref, hc_ref, kl_ref, cl_ref, sel_ref, yl_ref, yc_ref, wt_ref, wm_ref):
    nl = xl_ref.shape[0]
    sel = sel_ref[...]
    taps = []
    mask_t = _group_mask((LANES, S5_SEL_OUT), 4, 4)
    for a in range(S5_TAPS * S5_GROUP // S5_SEL_IN):
        t = jnp.dot(kl_ref[:, a * S5_SEL_IN:(a + 1) * S5_SEL_IN], sel, preferred_element_type=F32)
        taps.append(jnp.where(mask_t, t, 0.0).astype(BF16))
    taps = jnp.concatenate(taps, axis=1)
    for s in range(S5_T):
        lo = (S5_T - 1 - s) * LANES
        wt_ref[s * LANES:(s + 1) * LANES, :] = taps[:, lo:lo + S5_W]
    mask_m = _group_mask((S5_SW, S5_SEL_OUT), 6, 4)
    for q in range(4):
        for a in range(S5_T * S5_GROUP // S5_SEL_IN):
            m = jnp.dot(cl_ref[q, :, a * S5_SEL_IN:(a + 1) * S5_SEL_IN], sel, preferred_element_type=F32)
            wm_ref[q * S5_SW:(q + 1) * S5_SW, a * S5_SEL_OUT:(a + 1) * S5_SEL_OUT] = jnp.where(
                mask_m, m, 0.0).astype(BF16)
    x = jnp.concatenate([xl_ref[...], xc_ref[...]], axis=0)
    h = jnp.concatenate([hl_ref[...], hc_ref[...]], axis=0)
    y = jnp.dot(x, wt_ref[...], preferred_element_type=F32)
    y = y + jnp.dot(h, wm_ref[...], preferred_element_type=F32)
    yl_ref[...] = y[:nl].astype(yl_ref.dtype)
    yc_ref[...] = y[nl:].astype(yc_ref.dtype)


def s5_outputs(u3_lat, u3_ctx, h_lat, h_ctx, kl, cl, sel, layer):
    nl, nc = u3_lat.shape[1], u3_ctx.shape[1]
    return pl.pallas_call(
        _s5_out_kernel,
        grid=(S5_GB,),
        in_specs=[
            pl.BlockSpec((None, nl, S5_W), lambda g: (g, 0, 0)),
            pl.BlockSpec((None, nc, S5_W), lambda g: (g, 0, 0)),
            pl.BlockSpec((None, nl, 4 * S5_SW), lambda g: (g, 0, 0)),
            pl.BlockSpec((None, nc, 4 * S5_SW), lambda g: (g, 0, 0)),
            pl.BlockSpec((None, None, LANES, S5_TAPS * S5_GROUP), lambda g: (layer, g, 0, 0)),
            pl.BlockSpec((None, 4, None, S5_SW, S5_T * S5_GROUP), lambda g: (layer, 0, g, 0, 0)),
            pl.BlockSpec((S5_SEL_IN, S5_SEL_OUT), lambda g: (0, 0)),
        ],
        out_specs=[
            pl.BlockSpec((None, nl, S5_W), lambda g: (g, 0, 0)),
            pl.BlockSpec((None, nc, S5_W), lambda g: (g, 0, 0)),
        ],
        out_shape=[
            jax.ShapeDtypeStruct((S5_GB, nl, S5_W), BF16),
            jax.ShapeDtypeStruct((S5_GB, nc, S5_W), BF16),
        ],
        scratch_shapes=[pltpu.VMEM((S5_W, S5_W), BF16), pltpu.VMEM((4 * S5_SW, S5_W), BF16)],
        compiler_params=_cparams(("arbitrary",)),
        name="s5_outputs",
    )(u3_lat, u3_ctx, h_lat, h_ctx, kl, cl, sel)


def _s5_finish_kernel(y_ref, wg_ref, zs_ref, o_ref):
    y = jnp.concatenate([y_ref[q] for q in range(S5_GB)], axis=1)
    g = jax.nn.gelu(y.astype(F32))
    t = jnp.dot(g.astype(BF16), wg_ref[...], preferred_element_type=F32)
    zs = zs_ref[...].astype(F32)
    o_ref[...] = (g * jax.nn.sigmoid(t) * (zs * jax.nn.sigmoid(zs))).astype(o_ref.dtype)


def s5_finish(y3, w_glu, hin, zs_block):
    m = y3.shape[1]
    tm = min(512, m)
    return pl.pallas_call(
        _s5_finish_kernel,
        grid=(m // tm,),
        in_specs=[
            pl.BlockSpec((S5_GB, tm, LANES), lambda i: (0, i, 0)),
            pl.BlockSpec((BRANCH, BRANCH), lambda i: (0, 0)),
            pl.BlockSpec((tm, BRANCH), lambda i: (i, zs_block)),
        ],
        out_specs=pl.BlockSpec((tm, BRANCH), lambda i: (i, 0)),
        out_shape=jax.ShapeDtypeStruct((m, BRANCH), BF16),
        compiler_params=_cparams(("arbitrary",)),
        name="s5_finish",
    )(y3, w_glu, hin)


def s5_prepare(a_re, a_im, log_dt, b_re, b_im, c_re, c_im, d_skip, ncl):
    T, G, P, C = S5_T, S5_GROUPS, S5_STATE, S5_GROUP
    GB, GPB, SW = S5_GB, S5_GPB, S5_SW
    hp = lax.Precision.HIGHEST
    a_re, a_im = a_re.astype(F32), a_im.astype(F32)
    dt = jnp.exp(log_dt.astype(F32))[..., None]
    la_re, la_im = a_re * dt, a_im * dt
    mag = jnp.exp(la_re)
    lb_re, lb_im = mag * jnp.cos(la_im), mag * jnp.sin(la_im)
    nr, ni = lb_re - 1.0, lb_im
    den = a_re * a_re + a_im * a_im
    f_re = (nr * a_re + ni * a_im) / den
    f_im = (ni * a_re - nr * a_im) / den
    bb_re = f_re[..., None] * b_re - f_im[..., None] * b_im
    bb_im = f_re[..., None] * b_im + f_im[..., None] * b_re

    def cpow(k, lr, li):
        m = jnp.exp(k * lr)
        return m * jnp.cos(k * li), m * jnp.sin(k * li)

    lad_re, lad_im = la_re.reshape(2, GB, 1, SW), la_im.reshape(2, GB, 1, SW)
    tt = jnp.arange(T, dtype=F32).reshape(1, 1, T, 1)
    pw_re, pw_im = cpow(tt, lad_re, lad_im)
    to_lanes = lambda w: w.reshape(2, GB, GPB, P, C).transpose(0, 1, 4, 2, 3).reshape(2, GB, C, SW)
    bt_re, bt_im = to_lanes(bb_re), to_lanes(bb_im)
    e_re = pw_re[:, :, :, None] * bt_re[:, :, None] - pw_im[:, :, :, None] * bt_im[:, :, None]
    e_im = pw_re[:, :, :, None] * bt_im[:, :, None] + pw_im[:, :, :, None] * bt_re[:, :, None]
    ez = jnp.stack([e_re[0][:, ::-1], e_im[0][:, ::-1], e_re[1], e_im[1]], axis=2)

    cr = c_re.astype(F32).reshape(2, GB, GPB, C, P)
    ci = c_im.astype(F32).reshape(2, GB, GPB, C, P)
    e6_re, e6_im = e_re.reshape(2, GB, T, C, GPB, P), e_im.reshape(2, GB, T, C, GPB, P)
    kk = (jnp.einsum("dbgop,dbtigp->dbgito", cr, e6_re, precision=hp)
          - jnp.einsum("dbgop,dbtigp->dbgito", ci, e6_im, precision=hp))
    skip = jnp.eye(C, dtype=F32) * d_skip.astype(F32).reshape(GB, GPB, C, 1)
    center = kk[0][..., 0, :] + kk[1][..., 0, :] + skip
    kl = jnp.concatenate([kk[1][..., :0:-1, :], center[..., None, :], kk[0][..., 1:, :],
                          jnp.zeros((GB, GPB, C, 1, C), F32)], axis=3)
    kl = kl.reshape(GB, LANES, S5_TAPS * C)

    lar_re, lar_im = la_re[..., None], la_im[..., None]
    steps = jnp.stack([jnp.arange(1, T + 1, dtype=F32), jnp.arange(T, 0, -1).astype(F32)])
    pr_re, pr_im = cpow(steps.reshape(2, 1, 1, T), lar_re, lar_im)
    lane = jnp.arange(T * C)
    rep_c = (jnp.arange(C)[:, None] == lane[None, :] % C).astype(F32)
    rep_t = (jnp.arange(T)[:, None] == lane[None, :] // C).astype(F32)
    cp = jnp.stack([c_re, c_im]).astype(F32).transpose(0, 1, 2, 4, 3)
    cp = jnp.einsum("rdgpc,cq->rdgpq", cp, rep_c, precision=hp)
    pr = jnp.einsum("rdgpt,tq->rdgpq", jnp.stack([pr_re, pr_im]), rep_t, precision=hp)
    cl_re = cp[0] * pr[0] - cp[1] * pr[1]
    cl_im = cp[0] * pr[1] + cp[1] * pr[0]
    cl = jnp.stack([cl_re[0], -cl_im[0], cl_re[1], -cl_im[1]], axis=0)
    cl = cl.reshape(4, GB, SW, T * C)

    kf = jnp.arange(ncl, dtype=F32).reshape(1, ncl, 1) * float(T)
    af_re, af_im = cpow(kf, lad_re[0], lad_im[0])
    ab_re, ab_im = cpow(kf[:, ::-1], lad_re[1], lad_im[1])
    tab = jnp.concatenate([af_re, af_im, ab_re, ab_im], axis=-1)
    return ez.astype(BF16), kl.astype(BF16), cl.astype(BF16), tab


def s5_select_matrix():
    r = jnp.arange(S5_SEL_IN)
    q = jnp.arange(S5_SEL_OUT)
    return ((r[:, None] // S5_GROUP == q[None, :] // LANES)
            & (r[:, None] % S5_GROUP == q[None, :] % S5_GROUP)).astype(BF16)


MXU_N = 256
LOG2E = 1.4426950408889634


def _seg_mean_sq(x, ones_bd):
    sq = (x * x).astype(BF16)
    parts = [jnp.dot(sq[:, t:t + MXU_N], ones_bd, preferred_element_type=F32)
             for t in range(0, x.shape[1], MXU_N)]
    return jnp.concatenate(parts, axis=1) * (1.0 / DA_HEAD)


def _rope(x, cos, sin_signed):
    n = x.shape[1]
    lane = lax.broadcasted_iota(jnp.int32, x.shape, 1)
    first = (lane % (DA_HEAD // 2)) < (DA_HEAD // 4)
    rot = jnp.where(first, pltpu.roll(x, n - DA_HEAD // 4, 1), pltpu.roll(x, DA_HEAD // 4, 1))
    return x * cos + rot * sin_signed


def _qk_prep_kernel(q_ref, k_ref, *rest, rope):
    if rope:
        cos_ref, sin_ref, qg_ref, kg_ref, ones_ref, o_ref = rest
    else:
        qg_ref, kg_ref, ones_ref, o_ref = rest
    ones_bd = ones_ref[...]
    reps = BRANCH // LANES
    for x_ref, g_ref, scale, c0 in ((q_ref, qg_ref, DA_HEAD ** -0.5 * LOG2E, 0), (k_ref, kg_ref, 1.0, BRANCH)):
        x = x_ref[...].astype(F32)
        x = x * lax.rsqrt(_seg_mean_sq(x, ones_bd) + EPS) * jnp.concatenate([g_ref[...]] * reps, axis=1)
        if rope:
            x = _rope(x, jnp.concatenate([cos_ref[...]] * reps, axis=1),
                      jnp.concatenate([sin_ref[...]] * reps, axis=1))
        o_ref[:, c0:c0 + BRANCH] = (x * scale).astype(o_ref.dtype)


def qk_prep(hin, q_block, k_block, qg, kg, ones_bd, rope_tabs, seq_len):
    m = hin.shape[0]
    tm = min(512, m)
    rope = rope_tabs is not None
    in_specs = [pl.BlockSpec((tm, BRANCH), lambda i: (i, q_block)),
                pl.BlockSpec((tm, BRANCH), lambda i: (i, k_block))]
    args = [hin, hin]
    if rope:
        tps = seq_len // tm
        in_specs += [pl.BlockSpec((tm, LANES), lambda i: (i % tps, 0)),
                     pl.BlockSpec((tm, LANES), lambda i: (i % tps, 0))]
        args += list(rope_tabs)
    in_specs += [pl.BlockSpec((1, LANES), lambda i: (0, 0)), pl.BlockSpec((1, LANES), lambda i: (0, 0)),
                 pl.BlockSpec((MXU_N, MXU_N), lambda i: (0, 0))]
    args += [qg, kg, ones_bd]
    return pl.pallas_call(
        functools.partial(_qk_prep_kernel, rope=rope),
        grid=(m // tm,),
        in_specs=in_specs,
        out_specs=pl.BlockSpec((tm, 2 * BRANCH), lambda i: (i, 0)),
        out_shape=jax.ShapeDtypeStruct((m, 2 * BRANCH), BF16),
        compiler_params=_cparams(("arbitrary",)),
        name="qk_prep",
    )(*args)


def _stack_maps(q):
    lane = lax.broadcasted_iota(jnp.int32, q.shape, 1)
    return jnp.concatenate([jnp.where(lane < DA_HEAD, q, 0.0), jnp.where(lane >= DA_HEAD, q, 0.0)], axis=0)


def _diff_combine(pv, tq, lam, sg, lam_scale, zd):
    o = pv[:, :DA_VDIM] / pv[:, DA_VDIM:]
    o = o[:tq] - lam * o[tq:]
    o = o * lax.rsqrt(jnp.mean(o * o, axis=-1, keepdims=True) + EPS) * sg * lam_scale
    return o * (zd * jax.nn.sigmoid(zd))


def _diff_attn_ctx_kernel(q_ref, kc_ref, vc_ref, zd_ref, sg_ref, lam_ref, o_ref, *, lam_scale):
    v1 = jnp.concatenate([vc_ref[...], jnp.ones_like(vc_ref)], axis=1)
    tq = q_ref.shape[0]
    s = lax.dot_general(_stack_maps(q_ref[...]), kc_ref[...], (((1,), (1,)), ((), ())),
                        preferred_element_type=F32)
    p = jnp.exp2(s - jnp.max(s, axis=-1, keepdims=True))
    pv = jnp.dot(p.astype(BF16), v1, preferred_element_type=F32)
    o_ref[...] = _diff_combine(pv, tq, lam_ref[...], sg_ref[...], lam_scale,
                               zd_ref[...].astype(F32)).astype(o_ref.dtype)


def _diff_attn_lat_kernel(q_ref, kl_ref, vl_ref, kc_ref, vc_ref, zd_ref, sg_ref, lam_ref, o_ref,
                          kn_ref, v1_ref, s0_ref, s1_ref, m0_ref, m1_ref, *, tq, lam_scale):
    nctx = kc_ref.shape[0]
    nq = q_ref.shape[0] // tq

    kn_ref[0:nctx, :] = kc_ref[...]
    kn_ref[nctx:, :] = kl_ref[...]
    v1_ref[0:nctx, :] = jnp.concatenate([vc_ref[...], jnp.ones_like(vc_ref)], axis=1)
    v1_ref[nctx:, :] = jnp.concatenate([vl_ref[...], jnp.ones_like(vl_ref)], axis=1)

    bufs = ((s0_ref, m0_ref), (s1_ref, m1_ref))

    def scores(i, slot):
        s_ref, m_ref = bufs[slot]
        rows = pl.ds(pl.multiple_of(i * tq, tq), tq)
        s = lax.dot_general(_stack_maps(q_ref[rows, :]), kn_ref[...], (((1,), (1,)), ((), ())),
                            preferred_element_type=F32)
        s_ref[...] = s
        m_ref[...] = jnp.broadcast_to(jnp.max(s, axis=-1, keepdims=True), m_ref.shape)

    def finish(i, slot):
        s_ref, m_ref = bufs[slot]
        rows = pl.ds(pl.multiple_of(i * tq, tq), tq)
        p = jnp.exp2(s_ref[...] - m_ref[:, 0:1])
        pv = jnp.dot(p.astype(BF16), v1_ref[...], preferred_element_type=F32)
        o = _diff_combine(pv, tq, lam_ref[...], sg_ref[...], lam_scale, zd_ref[rows, :].astype(F32))
        o_ref[rows, :] = o.astype(o_ref.dtype)

    assert nq % 2 == 0
    scores(0, 0)

    def body(j, carry):
        scores(2 * j + 1, 1)
        finish(2 * j, 0)
        scores(2 * j + 2, 0)
        finish(2 * j + 1, 1)
        return carry

    lax.fori_loop(0, nq // 2 - 1, body, 0)
    scores(nq - 1, 1)
    finish(nq - 2, 0)
    finish(nq - 1, 1)


def diff_attention(hin_lat, hin_ctx, col_q, col_k, col_v, col_zd, batch, qg, kg, sg, lam_row, lam_init,
                   cos_t, sin_t, ones_bd):
    ml, mc = hin_lat.shape[0], hin_ctx.shape[0]
    L, nctx = ml // batch, mc // batch
    tq = min(256, L)
    qk_lat = qk_prep(hin_lat, col_q * LANES // BRANCH, col_k * LANES // BRANCH, qg, kg, ones_bd,
                     (cos_t, sin_t), L)
    qk_ctx = qk_prep(hin_ctx, col_q * LANES // BRANCH, col_k * LANES // BRANCH, qg, kg, ones_bd, None, nctx)
    kcol = BRANCH // LANES
    small = lambda b, h: (0, 0)
    common = [pl.BlockSpec((1, LANES), small), pl.BlockSpec((1, LANES), small)]
    cargs = [sg, lam_row]
    lam_scale = 1.0 - lam_init
    nk = nctx + L
    y_lat = pl.pallas_call(
        functools.partial(_diff_attn_lat_kernel, tq=tq, lam_scale=lam_scale),
        grid=(batch, DA_HEADS),
        in_specs=[
            pl.BlockSpec((L, LANES), lambda b, h: (b, h)),
            pl.BlockSpec((L, LANES), lambda b, h: (b, kcol + h)),
            pl.BlockSpec((L, LANES), lambda b, h: (b, col_v + h)),
            pl.BlockSpec((nctx, LANES), lambda b, h: (b, kcol + h)),
            pl.BlockSpec((nctx, LANES), lambda b, h: (b, col_v + h)),
            pl.BlockSpec((L, LANES), lambda b, h: (b, col_zd + h)),
        ] + common,
        out_specs=pl.BlockSpec((L, LANES), lambda b, h: (b, h)),
        out_shape=jax.ShapeDtypeStruct((ml, BRANCH), BF16),
        scratch_shapes=[
            pltpu.VMEM((nk, LANES), BF16),
            pltpu.VMEM((nk, 2 * DA_VDIM), BF16),
            pltpu.VMEM((2 * tq, nk), F32), pltpu.VMEM((2 * tq, nk), F32),
            pltpu.VMEM((2 * tq, LANES), F32), pltpu.VMEM((2 * tq, LANES), F32),
        ],
        compiler_params=_cparams(("arbitrary", "arbitrary")),
        name="diff_attn_lat",
    )(qk_lat, qk_lat, hin_lat, qk_ctx, hin_ctx, hin_lat, *cargs)
    y_ctx = pl.pallas_call(
        functools.partial(_diff_attn_ctx_kernel, lam_scale=lam_scale),
        grid=(batch, DA_HEADS),
        in_specs=[
            pl.BlockSpec((nctx, LANES), lambda b, h: (b, h)),
            pl.BlockSpec((nctx, LANES), lambda b, h: (b, kcol + h)),
            pl.BlockSpec((nctx, LANES), lambda b, h: (b, col_v + h)),
            pl.BlockSpec((nctx, LANES), lambda b, h: (b, col_zd + h)),
        ] + common,
        out_specs=pl.BlockSpec((nctx, LANES), lambda b, h: (b, h)),
        out_shape=jax.ShapeDtypeStruct((mc, BRANCH), BF16),
        compiler_params=_cparams(("arbitrary", "arbitrary")),
        name="diff_attn_ctx",
    )(qk_ctx, qk_ctx, hin_ctx, hin_ctx, *cargs)
    return y_lat, y_ctx


def seg_ones():
    seg = jnp.arange(MXU_N) // DA_HEAD
    return (seg[:, None] == seg[None, :]).astype(BF16)


def rope_tables(n_tokens):
    rows = n_tokens // GRID_W
    row = jnp.repeat(jnp.arange(rows, dtype=F32), GRID_W)
    col = jnp.tile(jnp.arange(GRID_W, dtype=F32), rows)
    n_freq = DA_HEAD // 4
    inv_freq = ROPE_BASE ** (-jnp.arange(n_freq, dtype=F32) / n_freq)
    ang_r = row[:, None] * inv_freq
    ang_c = col[:, None] * inv_freq
    ang = jnp.concatenate([ang_r, ang_r, ang_c, ang_c], axis=-1)
    sign = jnp.tile(jnp.concatenate([-jnp.ones(n_freq, F32), jnp.ones(n_freq, F32)]), 2)
    cos = jnp.tile(jnp.cos(ang), (1, 2))
    sin_signed = jnp.tile(jnp.sin(ang) * sign, (1, 2))
    return cos, sin_signed


def _mm_out_kernel(*refs, n_parts, with_next):
    a_refs, w_refs = refs[:n_parts], refs[n_parts:2 * n_parts]
    rest = refs[2 * n_parts:]
    if with_next:
        x_ref, g_ref, gs_ref, sh_ref, o_ref, hn_ref = rest
    else:
        x_ref, g_ref, o_ref = rest
    acc = None
    for a_ref, w_ref in zip(a_refs, w_refs):
        t = jnp.dot(a_ref[...], w_ref[...], preferred_element_type=F32)
        acc = t if acc is None else acc + t
    xn = x_ref[...] + g_ref[...] * acc
    o_ref[...] = xn
    if with_next:
        hn_ref[...] = _modulate(xn, gs_ref[...], sh_ref[...]).astype(hn_ref.dtype)


def mm_out(a_parts, w_stack, layer, x, gate, next_mod, rows_per_mod):
    m = x.shape[0]
    n_parts = len(a_parts)
    kp = w_stack.shape[1] // n_parts
    tm = min(512, rows_per_mod)
    tpm = rows_per_mod // tm
    mod_spec = pl.BlockSpec((None, 1, D_MODEL), lambda i: (i // tpm, 0, 0))
    in_specs = [pl.BlockSpec((tm, kp), lambda i: (i, 0)) for _ in a_parts]
    in_specs += [pl.BlockSpec((None, kp, D_MODEL), functools.partial(lambda i, p: (layer, p, 0), p=p))
                 for p in range(n_parts)]
    in_specs += [pl.BlockSpec((tm, D_MODEL), lambda i: (i, 0)), mod_spec]
    args = [*a_parts, *([w_stack] * n_parts), x, gate]
    out_specs = [pl.BlockSpec((tm, D_MODEL), lambda i: (i, 0))]
    out_shape = [jax.ShapeDtypeStruct((m, D_MODEL), F32)]
    if next_mod is not None:
        in_specs += [mod_spec, mod_spec]
        args += list(next_mod)
        out_specs.append(pl.BlockSpec((tm, D_MODEL), lambda i: (i, 0)))
        out_shape.append(jax.ShapeDtypeStruct((m, D_MODEL), BF16))
    outs = pl.pallas_call(
        functools.partial(_mm_out_kernel, n_parts=n_parts, with_next=next_mod is not None),
        grid=(m // tm,),
        in_specs=in_specs,
        out_specs=out_specs,
        out_shape=out_shape,
        compiler_params=_cparams(("arbitrary",)),
        name="mm_out",
    )(*args)
    return (outs[0], outs[1]) if next_mod is not None else (outs[0], None)


GLA_STEP = 4


def _seg_cumsum(x, reverse):
    n = x.shape[0]
    pos = lax.broadcasted_iota(jnp.int32, x.shape, 0) & (GLA_CHUNK - 1)
    s = 1
    while s < GLA_CHUNK:
        if reverse:
            x = x + jnp.where(pos < GLA_CHUNK - s, pltpu.roll(x, n - s, 0), 0.0)
        else:
            x = x + jnp.where(pos >= s, pltpu.roll(x, s, 0), 0.0)
        s *= 2
    return x


def _gla_kernel(ql_ref, kl_ref, vl_ref, rl_ref, zl_ref, qc_ref, kc_ref, vc_ref, rc_ref, zc_ref,
                wa_ref, ba_ref, ng_ref, *rest, ctx_out):
    if ctx_out:
        ol_ref, oc_ref = rest[:2]
        rest = rest[2:]
    else:
        ol_ref, oc_ref = rest[0], None
        rest = rest[1:]
    sf_ref, sb_ref, kv_ref, sin_ref, qd_ref, kd_ref, b_ref, vb_ref, ofl_ref, ofc_ref, obl_ref, obc_ref = rest
    C = GLA_CHUNK
    L, nctx = ql_ref.shape[0], kc_ref.shape[0]
    nl, nc = L // C, nctx // C
    row = lax.broadcasted_iota(jnp.int32, (C, C), 0)
    col = lax.broadcasted_iota(jnp.int32, (C, C), 1)

    for base, n, q_ref, k_ref, v_ref, r_ref in ((0, nctx, qc_ref, kc_ref, vc_ref, rc_ref),
                                                (nctx, L, ql_ref, kl_ref, vl_ref, rl_ref)):
        rows = slice(base, base + n)
        vb_ref[rows, :] = v_ref[...].astype(BF16)
        r = r_ref[...].astype(BF16)
        k = k_ref[...].astype(F32)
        with_q = ctx_out or base > 0
        for d in range(2):
            logits = jnp.dot(r, wa_ref[d], preferred_element_type=F32) + ba_ref[d]
            ls = jnp.minimum(logits, 0.0) - jnp.log(1.0 + jnp.exp(-jnp.abs(logits)))
            b = _seg_cumsum(ls * (1.0 / GLA_TAU), reverse=d == 1)
            b_ref[d, rows, :] = b
            kd_ref[d, rows, :] = (k * jnp.exp(-b)).astype(BF16)
            if with_q:
                qd_ref[d, rows, :] = (q_ref[...].astype(F32) * (GLA_DK ** -0.5) * jnp.exp(b)).astype(BF16)

    sf_ref[...] = jnp.zeros_like(sf_ref)
    sb_ref[...] = jnp.zeros_like(sb_ref)

    def segment(base, n_chunks, of_ref, ob_ref):
        assert n_chunks % GLA_STEP == 0

        def body(i, carry):
            work = []
            for u in range(GLA_STEP):
                cf = i * GLA_STEP + u
                cb = n_chunks - 1 - cf
                work.append((0, u, pl.multiple_of(base + cf * C, C), pl.multiple_of(cf * C, C)))
                work.append((1, GLA_STEP + u, pl.multiple_of(base + cb * C, C), pl.multiple_of(cb * C, C)))
            outs = {0: of_ref, 1: ob_ref}
            with_out = of_ref is not None
            scores, decs = {}, {}
            if with_out:
                for d, slot, row0, _ in work:
                    rows = pl.ds(row0, C)
                    scores[slot] = lax.dot_general(qd_ref[d, rows, :], kd_ref[d, rows, :],
                                                   (((1,), (1,)), ((), ())), preferred_element_type=F32)
            for d, slot, row0, _ in work:
                rows = pl.ds(row0, C)
                dec = jnp.exp(b_ref[d, pl.ds(row0 if d == 1 else row0 + C - 1, 1), :])
                k_end = (kd_ref[d, rows, :].astype(F32) * dec).astype(BF16)
                kv_ref[slot] = lax.dot_general(k_end, vb_ref[rows, :], (((0,), (0,)), ((), ())),
                                               preferred_element_type=F32)
                decs[slot] = jnp.transpose(jnp.broadcast_to(dec, (LANES, GLA_DK)))[:, 0:1]
            if with_out:
                for d, slot, row0, orow0 in work:
                    incl = (col >= row) if d == 1 else (col <= row)
                    sc = jnp.where(incl, scores[slot], 0.0).astype(BF16)
                    outs[d][pl.ds(orow0, C), :] = jnp.dot(sc, vb_ref[pl.ds(row0, C), :],
                                                          preferred_element_type=F32)
            for d, slot, _, _ in work:
                s_ref = sb_ref if d == 1 else sf_ref
                s_old = s_ref[...]
                sin_ref[slot] = s_old.astype(BF16)
                s_ref[...] = decs[slot] * s_old + kv_ref[slot]
            if with_out:
                for d, slot, row0, orow0 in work:
                    outs[d][pl.ds(orow0, C), :] += jnp.dot(qd_ref[d, pl.ds(row0, C), :], sin_ref[slot],
                                                           preferred_element_type=F32)
            return carry

        lax.fori_loop(0, n_chunks // GLA_STEP, body, 0)

    segment(0, nc, ofc_ref if ctx_out else None, obc_ref if ctx_out else None)
    segment(nctx, nl, ofl_ref, obl_ref)
    def finish(of_ref, ob_ref, z_ref, o_ref):
        o = of_ref[...] + ob_ref[...]
        z = z_ref[...].astype(F32)
        y = o * lax.rsqrt(jnp.mean(o * o, axis=-1, keepdims=True) + EPS) * ng_ref[...]
        o_ref[...] = (y * (z * jax.nn.sigmoid(z))).astype(o_ref.dtype)

    finish(ofl_ref, obl_ref, zl_ref, ol_ref)
    if ctx_out:
        finish(ofc_ref, obc_ref, zc_ref, oc_ref)


def gla_mix(hin_lat, hin_ctx, r_lat, r_ctx, wa2p, ba, norm_g, batch, ctx_out):
    ml, mc = hin_lat.shape[0], hin_ctx.shape[0]
    L, nctx = ml // batch, mc // batch
    kb = GLA_KEY // GLA_DK
    vb = 2 * GLA_KEY // GLA_DV
    zb = (2 * GLA_KEY + GLA_VAL) // GLA_DV

    def seg_specs(n):
        return [pl.BlockSpec((n, GLA_DK), lambda b, h: (b, h)),
                pl.BlockSpec((n, GLA_DK), lambda b, h: (b, kb + h)),
                pl.BlockSpec((n, GLA_DV), lambda b, h: (b, vb + h)),
                pl.BlockSpec((n, LANES), lambda b, h: (b, 0)),
                pl.BlockSpec((n, GLA_DV), lambda b, h: (b, zb + h))]

    in_specs = seg_specs(L) + seg_specs(nctx) + [
        pl.BlockSpec((2, LANES, GLA_DK), lambda b, h: (0, 0, h)),
        pl.BlockSpec((2, 1, GLA_DK), lambda b, h: (0, 0, h)),
        pl.BlockSpec((1, GLA_DV), lambda b, h: (0, 0)),
    ]
    out_specs = [pl.BlockSpec((L, GLA_DV), lambda b, h: (b, h)),
                 pl.BlockSpec((nctx, GLA_DV), lambda b, h: (b, h))]
    out_shape = [jax.ShapeDtypeStruct((ml, GLA_VAL), BF16), jax.ShapeDtypeStruct((mc, GLA_VAL), BF16)]
    if not ctx_out:
        out_specs, out_shape = out_specs[:1], out_shape[:1]
    outs = pl.pallas_call(
        functools.partial(_gla_kernel, ctx_out=ctx_out),
        grid=(batch, GLA_HEADS),
        in_specs=in_specs,
        out_specs=out_specs,
        out_shape=out_shape,
        scratch_shapes=[
            pltpu.VMEM((GLA_DK, GLA_DV), F32), pltpu.VMEM((GLA_DK, GLA_DV), F32),
            pltpu.VMEM((2 * GLA_STEP, GLA_DK, GLA_DV), F32),
            pltpu.VMEM((2 * GLA_STEP, GLA_DK, GLA_DV), BF16),
            pltpu.VMEM((2, nctx + L, GLA_DK), BF16),
            pltpu.VMEM((2, nctx + L, GLA_DK), BF16),
            pltpu.VMEM((2, nctx + L, GLA_DK), F32),
            pltpu.VMEM((nctx + L, GLA_DV), BF16),
            pltpu.VMEM((L, GLA_DV), F32), pltpu.VMEM((nctx, GLA_DV), F32),
            pltpu.VMEM((L, GLA_DV), F32), pltpu.VMEM((nctx, GLA_DV), F32),
        ],
        compiler_params=_cparams(("arbitrary", "arbitrary")),
        name="gla_mix",
    )(hin_lat, hin_lat, hin_lat, r_lat, hin_lat, hin_ctx, hin_ctx, hin_ctx, r_ctx, hin_ctx, wa2p, ba, norm_g)
    return (outs[0], outs[1]) if ctx_out else (outs[0], None)


def _mods(mod_l, norm_g, batch):
    shift, scale, gate = mod_l[:, :D_MODEL], mod_l[:, D_MODEL:2 * D_MODEL], mod_l[:, 2 * D_MODEL:]
    gs = (norm_g.astype(F32)[None, :] * (1.0 + scale))[:, None, :]
    sh = shift[:, None, :]
    gt = gate[:, None, :]
    lat = (gs[:batch], sh[:batch], gt[:batch])
    ctx = (gs[batch:batch + 1], sh[batch:batch + 1], gt[batch:batch + 1])
    return lat, ctx


def even_layer(x_lat, x_ctx, hn_l, hn_c, gates, next_mods, w_in, w_out, j, s5p, w_glu, qn_g, kn_g,
               lam_vecs, subln_g, lam_init, batch, rope):
    L = x_lat.shape[0] // batch
    gt_l, gt_c = gates
    rest = 5 * BRANCH
    u3_l = mm_in(hn_l, w_in, j, 0, BRANCH, out_dtype=BF16, gb_out=True, name="mm_in_u")
    u3_c = mm_in(hn_c, w_in, j, 0, BRANCH, out_dtype=BF16, gb_out=True, name="mm_in_u_ctx")
    hin_l = mm_in(hn_l, w_in, j, BRANCH, rest, out_dtype=BF16, name="mm_in_rest")
    hin_c = mm_in(hn_c, w_in, j, BRANCH, rest, out_dtype=BF16, name="mm_in_rest_ctx")

    ez, kl, cl, ptab, sel = s5p
    u3_l = u3_l.reshape(S5_GB, x_lat.shape[0] // S5_T, S5_W)
    u3_c = u3_c.reshape(S5_GB, x_ctx.shape[0] // S5_T, S5_W)
    h_l, h_c = s5_states(u3_l, u3_c, ez, ptab, j, batch)
    y_l, y_c = s5_outputs(u3_l, u3_c, h_l, h_c, kl, cl, sel, j)
    wg = w_glu.astype(BF16)
    a_s5_l = s5_finish(y_l.reshape(S5_GB, x_lat.shape[0], LANES), wg, hin_l, 0)
    a_s5_c = s5_finish(y_c.reshape(S5_GB, x_ctx.shape[0], LANES), wg, hin_c, 0)

    lv = lam_vecs.astype(F32)
    lam = jnp.exp(jnp.sum(lv[0] * lv[1])) - jnp.exp(jnp.sum(lv[2] * lv[3])) + lam_init
    lam_row = jnp.full((1, LANES), lam, F32)
    qg = jnp.tile(qn_g.astype(F32), 2)[None, :]
    kg = jnp.tile(kn_g.astype(F32), 2)[None, :]
    sg = subln_g.astype(F32)[None, :]
    cb = BRANCH // LANES
    cos_t, sin_t, ones_bd = rope
    a_da_l, a_da_c = diff_attention(hin_l, hin_c, cb, 2 * cb, 3 * cb, 4 * cb, batch, qg, kg, sg, lam_row,
                                    lam_init, cos_t, sin_t, ones_bd)

    nm_l, nm_c = next_mods
    x_lat, hn_l = mm_out([a_s5_l, a_da_l], w_out, j, x_lat, gt_l, nm_l, L)
    x_ctx, hn_c = mm_out([a_s5_c, a_da_c], w_out, j, x_ctx, gt_c, nm_c, x_ctx.shape[0])
    return x_lat, x_ctx, hn_l, hn_c


def odd_layer(x_lat, x_ctx, hn_l, hn_c, gates, next_mods, w_in, w_out, j, wa1, wa2, ba, gla_norm_g, batch,
              with_ctx_out):
    L = x_lat.shape[0] // batch
    gt_l, gt_c = gates
    n = 3 * D_MODEL
    w_aux = jnp.zeros((D_MODEL, LANES), F32).at[:, :GLA_RANK].set(wa1[0]).at[:, GLA_RANK:2 * GLA_RANK].set(wa1[1])
    w_aux = w_aux.astype(BF16)
    hin_l, r_l = mm_in(hn_l, w_in, j, 0, n, out_dtype=BF16, w_aux=w_aux, name="mm_in_odd")
    hin_c, r_c = mm_in(hn_c, w_in, j, 0, n, out_dtype=BF16, w_aux=w_aux, name="mm_in_odd_ctx")
    wa2p = jnp.zeros((2, LANES, GLA_KEY), F32)
    wa2p = wa2p.at[0, :GLA_RANK].set(wa2[0]).at[1, GLA_RANK:2 * GLA_RANK].set(wa2[1]).astype(BF16)
    ng = gla_norm_g.astype(F32)[None, :]
    a_l, a_c = gla_mix(hin_l, hin_c, r_l, r_c, wa2p, ba.astype(F32).reshape(2, 1, GLA_KEY), ng, batch,
                       with_ctx_out)
    nm_l, nm_c = next_mods
    x_lat, hn_l = mm_out([a_l], w_out, j, x_lat, gt_l, nm_l, L)
    if with_ctx_out:
        x_ctx, hn_c = mm_out([a_c], w_out, j, x_ctx, gt_c, nm_c, x_ctx.shape[0])
    return x_lat, x_ctx, hn_l, hn_c


def kernel(x, c, ctx, c_ctx, ada_w, ada_b, norm_g, ev_w_in, ev_w_out, s5_a_re, s5_a_im, s5_log_dt, s5_b_re, s5_b_im, s5_c_re, s5_c_im, s5_d, s5_w_glu, da_qn_g, da_kn_g, da_lam, da_subln_g, od_w_in, od_w_out, gla_wa1, gla_wa2, gla_ba, gla_norm_g):
    batch, L, _ = x.shape
    nctx = ctx.shape[1]
    x_lat = x.reshape(batch * L, D_MODEL)
    x_ctx = ctx.reshape(batch * nctx, D_MODEL)
    cond = jnp.zeros((8, D_MODEL), F32).at[:batch].set(c).at[batch].set(c_ctx)
    mod = modulation_all(cond, ada_w, ada_b)

    cos_t, sin_t = rope_tables(L)
    rope = (cos_t, sin_t, seg_ones())
    mods = [_mods(mod[i], norm_g[i], batch) for i in range(DEPTH)]
    ev_w_out_b, od_w_out_b = ev_w_out.astype(BF16), od_w_out.astype(BF16)
    ev_w_in, od_w_in = ev_w_in.astype(BF16), od_w_in.astype(BF16)

    s5p = jax.vmap(functools.partial(s5_prepare, ncl=L // S5_T))(
        s5_a_re, s5_a_im, s5_log_dt, s5_b_re, s5_b_im, s5_c_re, s5_c_im, s5_d) + (s5_select_matrix(),)

    (gs_l, sh_l, _), (gs_c, sh_c, _) = mods[0]
    hn_l = prenorm(x_lat, gs_l, sh_l, L)
    hn_c = prenorm(x_ctx, gs_c, sh_c, x_ctx.shape[0])
    for i in range(DEPTH):
        j = i // 2
        with_ctx_out = i < DEPTH - 1
        gates = (mods[i][0][2], mods[i][1][2])
        next_mods = (mods[i + 1][0][:2], mods[i + 1][1][:2]) if with_ctx_out else (None, None)
        if i % 2 == 0:
            lam_init = 0.8 - 0.6 * math.exp(-0.3 * i)
            x_lat, x_ctx, hn_l, hn_c = even_layer(
                x_lat, x_ctx, hn_l, hn_c, gates, next_mods, ev_w_in, ev_w_out_b, j, s5p, s5_w_glu[j],
                da_qn_g[j], da_kn_g[j], da_lam[j], da_subln_g[j], lam_init, batch, rope)
        else:
            x_lat, x_ctx, hn_l, hn_c = odd_layer(
                x_lat, x_ctx, hn_l, hn_c, gates, next_mods, od_w_in, od_w_out_b, j, gla_wa1[j], gla_wa2[j],
                gla_ba[j], gla_norm_g[j], batch, with_ctx_out)
    return x_lat.reshape(batch, L, D_MODEL)
```

```python
import functools
import math

import jax
import jax.numpy as jnp
from jax import lax
from jax.experimental import pallas as pl
from jax.experimental.pallas import tpu as pltpu

F32 = jnp.float32
BF16 = jnp.bfloat16

D_MODEL = 2048
DEPTH = 4
GRID_W = 64
EPS = 1e-6
BRANCH = D_MODEL // 2
S5_GROUP = 16
S5_GROUPS = BRANCH // S5_GROUP
S5_STATE = 64
DA_HEAD = 64
DA_HEADS = BRANCH // (2 * DA_HEAD)
DA_VDIM = 2 * DA_HEAD
ROPE_BASE = 10000.0
GLA_HEADS = 4
GLA_KEY = D_MODEL // 2
GLA_VAL = D_MODEL
GLA_DK = GLA_KEY // GLA_HEADS
GLA_DV = GLA_VAL // GLA_HEADS
GLA_RANK = 16
GLA_TAU = 16.0
GLA_CHUNK = 64

LANES = 128
VMEM_LIMIT = 56 * 1024 * 1024

S5_T = 16
S5_GB = BRANCH // LANES
S5_GPB = LANES // S5_GROUP
S5_W = S5_T * LANES
S5_SW = S5_GPB * S5_STATE


def _cparams(sem):
    return pltpu.CompilerParams(dimension_semantics=sem, vmem_limit_bytes=VMEM_LIMIT)


def _mod_kernel(c_ref, w_ref, b_ref, o_ref):
    c = c_ref[...]
    s = (c * jax.nn.sigmoid(c)).astype(BF16)
    acc = jnp.dot(s, w_ref[...].astype(BF16), preferred_element_type=F32)
    o_ref[...] = acc + b_ref[...]


def modulation_all(cond, ada_w, ada_b):
    tn = 512
    n = 3 * D_MODEL
    return pl.pallas_call(
        _mod_kernel,
        grid=(DEPTH, n // tn),
        in_specs=[
            pl.BlockSpec((8, D_MODEL), lambda l, j: (0, 0)),
            pl.BlockSpec((None, D_MODEL, tn), lambda l, j: (l, 0, j)),
            pl.BlockSpec((None, 1, tn), lambda l, j: (l, 0, j)),
        ],
        out_specs=pl.BlockSpec((None, 8, tn), lambda l, j: (l, 0, j)),
        out_shape=jax.ShapeDtypeStruct((DEPTH, 8, n), F32),
        compiler_params=_cparams(("arbitrary", "arbitrary")),
        name="modulation",
    )(cond, ada_w, ada_b.reshape(DEPTH, 1, n))


def _modulate(x, gs, sh):
    ms = jnp.mean(x * x, axis=-1, keepdims=True)
    return x * lax.rsqrt(ms + EPS) * gs + sh


def _prenorm_kernel(x_ref, gs_ref, sh_ref, o_ref):
    o_ref[...] = _modulate(x_ref[...], gs_ref[...], sh_ref[...]).astype(o_ref.dtype)


def prenorm(x, gs, sh, rows_per_mod):
    m = x.shape[0]
    tm = min(512, rows_per_mod)
    tpm = rows_per_mod // tm
    return pl.pallas_call(
        _prenorm_kernel,
        grid=(m // tm,),
        in_specs=[
            pl.BlockSpec((tm, D_MODEL), lambda i: (i, 0)),
            pl.BlockSpec((None, 1, D_MODEL), lambda i: (i // tpm, 0, 0)),
            pl.BlockSpec((None, 1, D_MODEL), lambda i: (i // tpm, 0, 0)),
        ],
        out_specs=pl.BlockSpec((tm, D_MODEL), lambda i: (i, 0)),
        out_shape=jax.ShapeDtypeStruct((m, D_MODEL), BF16),
        compiler_params=_cparams(("arbitrary",)),
        name="prenorm",
    )(x, gs, sh)


def _mm_in_kernel(a_ref, w_ref, *rest, gb_out, has_aux):
    if has_aux:
        wa_ref, o_ref, aux_ref = rest

        @pl.when(pl.program_id(1) == 0)
        def _():
            aux_ref[...] = jnp.dot(a_ref[...], wa_ref[...], preferred_element_type=F32)
    elif gb_out:
        o_ref, stage_ref = rest
    else:
        (o_ref,) = rest

    acc = jnp.dot(a_ref[...], w_ref[...].astype(BF16), preferred_element_type=F32)
    if gb_out:
        n_rows = acc.shape[0] // S5_T
        for q in range(acc.shape[1] // LANES):
            stage_ref[q] = acc[:, q * LANES:(q + 1) * LANES]
        for q in range(acc.shape[1] // LANES):
            for t in range(S5_T):
                o_ref[q, :, t * LANES:(t + 1) * LANES] = stage_ref[q, pl.ds(t, n_rows, stride=S5_T), :].astype(
                    o_ref.dtype)
    else:
        o_ref[...] = acc.astype(o_ref.dtype)


def mm_in(a, w_stack, layer, col0, ncols, *, out_dtype, gb_out=False, w_aux=None, name="mm_in"):
    m = a.shape[0]
    tm = min(2048, m)
    tn = 512
    assert m % tm == 0 and ncols % tn == 0 and col0 % tn == 0
    jb = col0 // tn
    in_specs = [
        pl.BlockSpec((tm, D_MODEL), lambda i, j: (i, 0)),
        pl.BlockSpec((None, D_MODEL, tn), lambda i, j: (layer, 0, j + jb)),
    ]
    args = [a, w_stack]
    scratch = []
    if gb_out:
        out_shape = [jax.ShapeDtypeStruct((ncols // LANES, m // S5_T, S5_W), out_dtype)]
        out_specs = [pl.BlockSpec((tn // LANES, tm // S5_T, S5_W), lambda i, j: (j, i, 0))]
        scratch = [pltpu.VMEM((tn // LANES, tm, LANES), F32)]
    else:
        out_shape = [jax.ShapeDtypeStruct((m, ncols), out_dtype)]
        out_specs = [pl.BlockSpec((tm, tn), lambda i, j: (i, j))]
    if w_aux is not None:
        in_specs.append(pl.BlockSpec((D_MODEL, LANES), lambda i, j: (0, 0)))
        args.append(w_aux)
        out_shape.append(jax.ShapeDtypeStruct((m, LANES), F32))
        out_specs.append(pl.BlockSpec((tm, LANES), lambda i, j: (i, 0)))
    outs = pl.pallas_call(
        functools.partial(_mm_in_kernel, gb_out=gb_out, has_aux=w_aux is not None),
        grid=(m // tm, ncols // tn),
        in_specs=in_specs,
        out_specs=out_specs,
        out_shape=out_shape,
        scratch_shapes=scratch,
        compiler_params=_cparams(("arbitrary", "arbitrary")),
        name=name,
    )(*args)
    return outs if w_aux is not None else outs[0]


def _shift_rows(h, s, up):
    n = h.shape[0]
    row = lax.broadcasted_iota(jnp.int32, h.shape, 0)
    if up:
        return jnp.where(row >= n - s, 0.0, pltpu.roll(h, n - s, 0))
    return jnp.where(row < s, 0.0, pltpu.roll(h, s, 0))


def _chunk_scan(zr, zi, pr, pi, reverse):
    n = zr.shape[0]
    ntab = pr.shape[0]
    hr, hi = zr, zi
    s = 1
    while s < n:
        idx = ntab - 1 - s if reverse else s
        ar, ai = pr[idx:idx + 1], pi[idx:idx + 1]
        sr, si = _shift_rows(hr, s, reverse), _shift_rows(hi, s, reverse)
        hr, hi = hr + ar * sr - ai * si, hi + ar * si + ai * sr
        s *= 2
    return hr, hi


def _group_mask(shape, row_shift, col_shift):
    rg = lax.broadcasted_iota(jnp.int32, shape, 0) >> row_shift
    cg = (lax.broadcasted_iota(jnp.int32, shape, 1) >> col_shift) & (S5_GPB - 1)
    return rg == cg


def _s5_state_kernel(xl_ref, xc_ref, ez_ref, p_ref, hl_ref, hc_ref, wz_ref):
    ncc = xc_ref.shape[0]

    @pl.when(pl.program_id(1) == 0)
    def _():
        mask = _group_mask((LANES, S5_SW), 4, 6)
        for t in range(S5_T):
            for q in range(4):
                e = jnp.concatenate([ez_ref[t, q]] * S5_GPB, axis=0)
                wz_ref[t * LANES:(t + 1) * LANES, q * S5_SW:(q + 1) * S5_SW] = jnp.where(
                    mask, e, jnp.zeros_like(e))

    x = jnp.concatenate([xc_ref[...], xl_ref[...]], axis=0)
    z = jnp.dot(x, wz_ref[...], preferred_element_type=F32)
    p = p_ref[...]
    outs_c, outs_l = [], []
    for d in range(2):
        rev = d == 1
        c0 = 2 * d * S5_SW
        zr, zi = z[:, c0:c0 + S5_SW], z[:, c0 + S5_SW:c0 + 2 * S5_SW]
        pr, pi = p[:, c0:c0 + S5_SW], p[:, c0 + S5_SW:c0 + 2 * S5_SW]
        cr, ci = _chunk_scan(zr[:ncc], zi[:ncc], pr, pi, rev)
        lr, li = _chunk_scan(zr[ncc:], zi[ncc:], pr, pi, rev)
        if rev:
            car_r, car_i = cr[0:1], ci[0:1]
        else:
            car_r, car_i = cr[ncc - 1:ncc], ci[ncc - 1:ncc]
        hcr, hci = _shift_rows(cr, 1, rev), _shift_rows(ci, 1, rev)
        hlr = _shift_rows(lr, 1, rev) + pr * car_r - pi * car_i
        hli = _shift_rows(li, 1, rev) + pr * car_i + pi * car_r
        outs_c += [hcr, hci]
        outs_l += [hlr, hli]
    hc_ref[...] = jnp.concatenate(outs_c, axis=1).astype(hc_ref.dtype)
    hl_ref[...] = jnp.concatenate(outs_l, axis=1).astype(hl_ref.dtype)


def s5_states(u3_lat, u3_ctx, ez, ptab, layer, batch):
    ncl = u3_lat.shape[1] // batch
    ncc = u3_ctx.shape[1] // batch
    assert ptab.shape[2] == ncl
    return pl.pallas_call(
        _s5_state_kernel,
        grid=(S5_GB, batch),
        in_specs=[
            pl.BlockSpec((None, ncl, S5_W), lambda g, b: (g, b, 0)),
            pl.BlockSpec((None, ncc, S5_W), lambda g, b: (g, b, 0)),
            pl.BlockSpec((None, None, S5_T, 4, S5_GROUP, S5_SW), lambda g, b: (layer, g, 0, 0, 0, 0)),
            pl.BlockSpec((None, None, ncl, 4 * S5_SW), lambda g, b: (layer, g, 0, 0)),
        ],
        out_specs=[
            pl.BlockSpec((None, ncl, 4 * S5_SW), lambda g, b: (g, b, 0)),
            pl.BlockSpec((None, ncc, 4 * S5_SW), lambda g, b: (g, b, 0)),
        ],
        out_shape=[
            jax.ShapeDtypeStruct((S5_GB, u3_lat.shape[1], 4 * S5_SW), BF16),
            jax.ShapeDtypeStruct((S5_GB, u3_ctx.shape[1], 4 * S5_SW), BF16),
        ],
        scratch_shapes=[pltpu.VMEM((S5_W, 4 * S5_SW), BF16)],
        compiler_params=_cparams(("arbitrary", "arbitrary")),
        name="s5_states",
    )(u3_lat, u3_ctx, ez, ptab)


S5_TAPS = 2 * S5_T
S5_SEL_IN = LANES
S5_SEL_OUT = (LANES // S5_GROUP) * LANES


def _s5_out_kernel(xl_ref, xc_ref, hl_ref, hc_ref, kl_ref, cl_ref, sel_ref, yl_ref, yc_ref, wt_ref, wm_ref):
    nl = xl_ref.shape[0]
    sel = sel_ref[...]
    taps = []
    mask_t = _group_mask((LANES, S5_SEL_OUT), 4, 4)
    for a in range(S5_TAPS * S5_GROUP // S5_SEL_IN):
        t = jnp.dot(kl_ref[:, a * S5_SEL_IN:(a + 1) * S5_SEL_IN], sel, preferred_element_type=F32)
        taps.append(jnp.where(mask_t, t, 0.0).astype(BF16))
    taps = jnp.concatenate(taps, axis=1)
    for s in range(S5_T):
        lo = (S5_T - 1 - s) * LANES
        wt_ref[s * LANES:(s + 1) * LANES, :] = taps[:, lo:lo + S5_W]
    mask_m = _group_mask((S5_SW, S5_SEL_OUT), 6, 4)
    for q in range(4):
        for a in range(S5_T * S5_GROUP // S5_SEL_IN):
            m = jnp.dot(cl_ref[q, :, a * S5_SEL_IN:(a + 1) * S5_SEL_IN], sel, preferred_element_type=F32)
            wm_ref[q * S5_SW:(q + 1) * S5_SW, a * S5_SEL_OUT:(a + 1) * S5_SEL_OUT] = jnp.where(
                mask_m, m, 0.0).astype(BF16)
    x = jnp.concatenate([xl_ref[...], xc_ref[...]], axis=0)
    h = jnp.concatenate([hl_ref[...], hc_ref[...]], axis=0)
    y = jnp.dot(x, wt_ref[...], preferred_element_type=F32)
    y = y + jnp.dot(h, wm_ref[...], preferred_element_type=F32)
    yl_ref[...] = y[:nl].astype(yl_ref.dtype)
    yc_ref[...] = y[nl:].astype(yc_ref.dtype)


def s5_outputs(u3_lat, u3_ctx, h_lat, h_ctx, kl, cl, sel, layer):
    nl, nc = u3_lat.shape[1], u3_ctx.shape[1]
    return pl.pallas_call(
        _s5_out_kernel,
        grid=(S5_GB,),
        in_specs=[
            pl.BlockSpec((None, nl, S5_W), lambda g: (g, 0, 0)),
            pl.BlockSpec((None, nc, S5_W), lambda g: (g, 0, 0)),
            pl.BlockSpec((None, nl, 4 * S5_SW), lambda g: (g, 0, 0)),
            pl.BlockSpec((None, nc, 4 * S5_SW), lambda g: (g, 0, 0)),
            pl.BlockSpec((None, None, LANES, S5_TAPS * S5_GROUP), lambda g: (layer, g, 0, 0)),
            pl.BlockSpec((None, 4, None, S5_SW, S5_T * S5_GROUP), lambda g: (layer, 0, g, 0, 0)),
            pl.BlockSpec((S5_SEL_IN, S5_SEL_OUT), lambda g: (0, 0)),
        ],
        out_specs=[
            pl.BlockSpec((None, nl, S5_W), lambda g: (g, 0, 0)),
            pl.BlockSpec((None, nc, S5_W), lambda g: (g, 0, 0)),
        ],
        out_shape=[
            jax.ShapeDtypeStruct((S5_GB, nl, S5_W), BF16),
            jax.ShapeDtypeStruct((S5_GB, nc, S5_W), BF16),
        ],
        scratch_shapes=[pltpu.VMEM((S5_W, S5_W), BF16), pltpu.VMEM((4 * S5_SW, S5_W), BF16)],
        compiler_params=_cparams(("arbitrary",)),
        name="s5_outputs",
    )(u3_lat, u3_ctx, h_lat, h_ctx, kl, cl, sel)


def _s5_finish_kernel(y_ref, wg_ref, zs_ref, o_ref, stage_ref):
    n_rows = y_ref.shape[1]
    for q in range(S5_GB):
        for t in range(S5_T):
            stage_ref[q, pl.ds(t, n_rows, stride=S5_T), :] = y_ref[q, :, t * LANES:(t + 1) * LANES].astype(F32)
    y = jnp.concatenate([stage_ref[q] for q in range(S5_GB)], axis=1)
    g = jax.nn.gelu(y)
    t = jnp.dot(g.astype(BF16), wg_ref[...], preferred_element_type=F32)
    zs = zs_ref[...].astype(F32)
    o_ref[...] = (g * jax.nn.sigmoid(t) * (zs * jax.nn.sigmoid(zs))).astype(o_ref.dtype)


def s5_finish(y3, w_glu, hin, zs_block):
    m = y3.shape[1] * S5_T
    tm = min(512, m)
    return pl.pallas_call(
        _s5_finish_kernel,
        grid=(m // tm,),
        in_specs=[
            pl.BlockSpec((S5_GB, tm // S5_T, S5_W), lambda i: (0, i, 0)),
            pl.BlockSpec((BRANCH, BRANCH), lambda i: (0, 0)),
            pl.BlockSpec((tm, BRANCH), lambda i: (i, zs_block)),
        ],
        out_specs=pl.BlockSpec((tm, BRANCH), lambda i: (i, 0)),
        out_shape=jax.ShapeDtypeStruct((m, BRANCH), BF16),
        scratch_shapes=[pltpu.VMEM((S5_GB, tm, LANES), F32)],
        compiler_params=_cparams(("arbitrary",)),
        name="s5_finish",
    )(y3, w_glu, hin)


def s5_prepare(a_re, a_im, log_dt, b_re, b_im, c_re, c_im, d_skip, ncl):
    T, G, P, C = S5_T, S5_GROUPS, S5_STATE, S5_GROUP
    GB, GPB, SW = S5_GB, S5_GPB, S5_SW
    hp = lax.Precision.HIGHEST
    a_re, a_im = a_re.astype(F32), a_im.astype(F32)
    dt = jnp.exp(log_dt.astype(F32))[..., None]
    la_re, la_im = a_re * dt, a_im * dt
    mag = jnp.exp(la_re)
    lb_re, lb_im = mag * jnp.cos(la_im), mag * jnp.sin(la_im)
    nr, ni = lb_re - 1.0, lb_im
    den = a_re * a_re + a_im * a_im
    f_re = (nr * a_re + ni * a_im) / den
    f_im = (ni * a_re - nr * a_im) / den
    bb_re = f_re[..., None] * b_re - f_im[..., None] * b_im
    bb_im = f_re[..., None] * b_im + f_im[..., None] * b_re

    def cpow(k, lr, li):
        m = jnp.exp(k * lr)
        return m * jnp.cos(k * li), m * jnp.sin(k * li)

    lad_re, lad_im = la_re.reshape(2, GB, 1, SW), la_im.reshape(2, GB, 1, SW)
    tt = jnp.arange(T, dtype=F32).reshape(1, 1, T, 1)
    pw_re, pw_im = cpow(tt, lad_re, lad_im)
    to_lanes = lambda w: w.reshape(2, GB, GPB, P, C).transpose(0, 1, 4, 2, 3).reshape(2, GB, C, SW)
    bt_re, bt_im = to_lanes(bb_re), to_lanes(bb_im)
    e_re = pw_re[:, :, :, None] * bt_re[:, :, None] - pw_im[:, :, :, None] * bt_im[:, :, None]
    e_im = pw_re[:, :, :, None] * bt_im[:, :, None] + pw_im[:, :, :, None] * bt_re[:, :, None]
    ez = jnp.stack([e_re[0][:, ::-1], e_im[0][:, ::-1], e_re[1], e_im[1]], axis=2)

    cr = c_re.astype(F32).reshape(2, GB, GPB, C, P)
    ci = c_im.astype(F32).reshape(2, GB, GPB, C, P)
    e6_re, e6_im = e_re.reshape(2, GB, T, C, GPB, P), e_im.reshape(2, GB, T, C, GPB, P)
    kk = (jnp.einsum("dbgop,dbtigp->dbgito", cr, e6_re, precision=hp)
          - jnp.einsum("dbgop,dbtigp->dbgito", ci, e6_im, precision=hp))
    skip = jnp.eye(C, dtype=F32) * d_skip.astype(F32).reshape(GB, GPB, C, 1)
    center = kk[0][..., 0, :] + kk[1][..., 0, :] + skip
    kl = jnp.concatenate([kk[1][..., :0:-1, :], center[..., None, :], kk[0][..., 1:, :],
                          jnp.zeros((GB, GPB, C, 1, C), F32)], axis=3)
    kl = kl.reshape(GB, LANES, S5_TAPS * C)

    lar_re, lar_im = la_re[..., None], la_im[..., None]
    steps = jnp.stack([jnp.arange(1, T + 1, dtype=F32), jnp.arange(T, 0, -1).astype(F32)])
    pr_re, pr_im = cpow(steps.reshape(2, 1, 1, T), lar_re, lar_im)
    lane = jnp.arange(T * C)
    rep_c = (jnp.arange(C)[:, None] == lane[None, :] % C).astype(F32)
    rep_t = (jnp.arange(T)[:, None] == lane[None, :] // C).astype(F32)
    cp = jnp.stack([c_re, c_im]).astype(F32).transpose(0, 1, 2, 4, 3)
    cp = jnp.einsum("rdgpc,cq->rdgpq", cp, rep_c, precision=hp)
    pr = jnp.einsum("rdgpt,tq->rdgpq", jnp.stack([pr_re, pr_im]), rep_t, precision=hp)
    cl_re = cp[0] * pr[0] - cp[1] * pr[1]
    cl_im = cp[0] * pr[1] + cp[1] * pr[0]
    cl = jnp.stack([cl_re[0], -cl_im[0], cl_re[1], -cl_im[1]], axis=0)
    cl = cl.reshape(4, GB, SW, T * C)

    kf = jnp.arange(ncl, dtype=F32).reshape(1, ncl, 1) * float(T)
    af_re, af_im = cpow(kf, lad_re[0], lad_im[0])
    ab_re, ab_im = cpow(kf[:, ::-1], lad_re[1], lad_im[1])
    tab = jnp.concatenate([af_re, af_im, ab_re, ab_im], axis=-1)
    return ez.astype(BF16), kl.astype(BF16), cl.astype(BF16), tab


def s5_select_matrix():
    r = jnp.arange(S5_SEL_IN)
    q = jnp.arange(S5_SEL_OUT)
    return ((r[:, None] // S5_GROUP == q[None, :] // LANES)
            & (r[:, None] % S5_GROUP == q[None, :] % S5_GROUP)).astype(BF16)


MXU_N = 256
LOG2E = 1.4426950408889634


def _seg_mean_sq(x, ones_bd):
    sq = (x * x).astype(BF16)
    parts = [jnp.dot(sq[:, t:t + MXU_N], ones_bd, preferred_element_type=F32)
             for t in range(0, x.shape[1], MXU_N)]
    return jnp.concatenate(parts, axis=1) * (1.0 / DA_HEAD)


def _rope(x, cos, sin_signed):
    n = x.shape[1]
    lane = lax.broadcasted_iota(jnp.int32, x.shape, 1)
    first = (lane % (DA_HEAD // 2)) < (DA_HEAD // 4)
    rot = jnp.where(first, pltpu.roll(x, n - DA_HEAD // 4, 1), pltpu.roll(x, DA_HEAD // 4, 1))
    return x * cos + rot * sin_signed


def _qk_prep_kernel(q_ref, k_ref, *rest, rope):
    if rope:
        cos_ref, sin_ref, qg_ref, kg_ref, ones_ref, o_ref = rest
    else:
        qg_ref, kg_ref, ones_ref, o_ref = rest
    ones_bd = ones_ref[...]
    reps = BRANCH // LANES
    for x_ref, g_ref, scale, c0 in ((q_ref, qg_ref, DA_HEAD ** -0.5 * LOG2E, 0), (k_ref, kg_ref, 1.0, BRANCH)):
        x = x_ref[...].astype(F32)
        x = x * lax.rsqrt(_seg_mean_sq(x, ones_bd) + EPS) * jnp.concatenate([g_ref[...]] * reps, axis=1)
        if rope:
            x = _rope(x, jnp.concatenate([cos_ref[...]] * reps, axis=1),
                      jnp.concatenate([sin_ref[...]] * reps, axis=1))
        o_ref[:, c0:c0 + BRANCH] = (x * scale).astype(o_ref.dtype)


def qk_prep(hin, q_block, k_block, qg, kg, ones_bd, rope_tabs, seq_len):
    m = hin.shape[0]
    tm = min(512, m)
    rope = rope_tabs is not None
    in_specs = [pl.BlockSpec((tm, BRANCH), lambda i: (i, q_block)),
                pl.BlockSpec((tm, BRANCH), lambda i: (i, k_block))]
    args = [hin, hin]
    if rope:
        tps = seq_len // tm
        in_specs += [pl.BlockSpec((tm, LANES), lambda i: (i % tps, 0)),
                     pl.BlockSpec((tm, LANES), lambda i: (i % tps, 0))]
        args += list(rope_tabs)
    in_specs += [pl.BlockSpec((1, LANES), lambda i: (0, 0)), pl.BlockSpec((1, LANES), lambda i: (0, 0)),
                 pl.BlockSpec((MXU_N, MXU_N), lambda i: (0, 0))]
    args += [qg, kg, ones_bd]
    return pl.pallas_call(
        functools.partial(_qk_prep_kernel, rope=rope),
        grid=(m // tm,),
        in_specs=in_specs,
        out_specs=pl.BlockSpec((tm, 2 * BRANCH), lambda i: (i, 0)),
        out_shape=jax.ShapeDtypeStruct((m, 2 * BRANCH), BF16),
        compiler_params=_cparams(("arbitrary",)),
        name="qk_prep",
    )(*args)


def _stack_maps(q):
    lane = lax.broadcasted_iota(jnp.int32, q.shape, 1)
    return jnp.concatenate([jnp.where(lane < DA_HEAD, q, 0.0), jnp.where(lane >= DA_HEAD, q, 0.0)], axis=0)


def _diff_combine(pv, tq, lam, sg, lam_scale, zd):
    o = pv[:, :DA_VDIM] / pv[:, DA_VDIM:]
    o = o[:tq] - lam * o[tq:]
    o = o * lax.rsqrt(jnp.mean(o * o, axis=-1, keepdims=True) + EPS) * sg * lam_scale
    return o * (zd * jax.nn.sigmoid(zd))


def _diff_attn_ctx_kernel(q_ref, kc_ref, vc_ref, zd_ref, sg_ref, lam_ref, o_ref, *, lam_scale):
    v1 = jnp.concatenate([vc_ref[...], jnp.ones_like(vc_ref)], axis=1)
    tq = q_ref.shape[0]
    s = lax.dot_general(_stack_maps(q_ref[...]), kc_ref[...], (((1,), (1,)), ((), ())),
                        preferred_element_type=F32)
    p = jnp.exp2(s - jnp.max(s, axis=-1, keepdims=True))
    pv = jnp.dot(p.astype(BF16), v1, preferred_element_type=F32)
    o_ref[...] = _diff_combine(pv, tq, lam_ref[...], sg_ref[...], lam_scale,
                               zd_ref[...].astype(F32)).astype(o_ref.dtype)


def _diff_attn_lat_kernel(q_ref, kl_ref, vl_ref, kc_ref, vc_ref, zd_ref, sg_ref, lam_ref, o_ref,
                          kn_ref, v1_ref, s0_ref, s1_ref, m0_ref, m1_ref, *, tq, lam_scale):
    nctx = kc_ref.shape[0]
    nq = q_ref.shape[0] // tq

    kn_ref[:, 0:nctx] = jnp.transpose(kc_ref[...].astype(F32)).astype(BF16)
    kn_ref[:, nctx:] = jnp.transpose(kl_ref[...].astype(F32)).astype(BF16)
    v1_ref[0:nctx, :] = jnp.concatenate([vc_ref[...], jnp.ones_like(vc_ref)], axis=1)
    v1_ref[nctx:, :] = jnp.concatenate([vl_ref[...], jnp.ones_like(vl_ref)], axis=1)

    bufs = ((s0_ref, m0_ref), (s1_ref, m1_ref))

    def scores(i, slot):
        s_ref, m_ref = bufs[slot]
        rows = pl.ds(pl.multiple_of(i * tq, tq), tq)
        s = jnp.dot(_stack_maps(q_ref[rows, :]), kn_ref[...], preferred_element_type=F32)
        s_ref[...] = s
        m_ref[...] = jnp.broadcast_to(jnp.max(s, axis=-1, keepdims=True), m_ref.shape)

    def finish(i, slot):
        s_ref, m_ref = bufs[slot]
        rows = pl.ds(pl.multiple_of(i * tq, tq), tq)
        p = jnp.exp2(s_ref[...] - m_ref[:, 0:1])
        pv = jnp.dot(p.astype(BF16), v1_ref[...], preferred_element_type=F32)
        o = _diff_combine(pv, tq, lam_ref[...], sg_ref[...], lam_scale, zd_ref[rows, :].astype(F32))
        o_ref[rows, :] = o.astype(o_ref.dtype)

    assert nq % 2 == 0
    scores(0, 0)

    def body(j, carry):
        scores(2 * j + 1, 1)
        finish(2 * j, 0)
        scores(2 * j + 2, 0)
        finish(2 * j + 1, 1)
        return carry

    lax.fori_loop(0, nq // 2 - 1, body, 0)
    scores(nq - 1, 1)
    finish(nq - 2, 0)
    finish(nq - 1, 1)


def diff_attention(hin_lat, hin_ctx, col_q, col_k, col_v, col_zd, batch, qg, kg, sg, lam_row, lam_init,
                   cos_t, sin_t, ones_bd):
    ml, mc = hin_lat.shape[0], hin_ctx.shape[0]
    L, nctx = ml // batch, mc // batch
    tq = min(256, L)
    qk_lat = qk_prep(hin_lat, col_q * LANES // BRANCH, col_k * LANES // BRANCH, qg, kg, ones_bd,
                     (cos_t, sin_t), L)
    qk_ctx = qk_prep(hin_ctx, col_q * LANES // BRANCH, col_k * LANES // BRANCH, qg, kg, ones_bd, None, nctx)
    kcol = BRANCH // LANES
    small = lambda b, h: (0, 0)
    common = [pl.BlockSpec((1, LANES), small), pl.BlockSpec((1, LANES), small)]
    cargs = [sg, lam_row]
    lam_scale = 1.0 - lam_init
    nk = nctx + L
    y_lat = pl.pallas_call(
        functools.partial(_diff_attn_lat_kernel, tq=tq, lam_scale=lam_scale),
        grid=(batch, DA_HEADS),
        in_specs=[
            pl.BlockSpec((L, LANES), lambda b, h: (b, h)),
            pl.BlockSpec((L, LANES), lambda b, h: (b, kcol + h)),
            pl.BlockSpec((L, LANES), lambda b, h: (b, col_v + h)),
            pl.BlockSpec((nctx, LANES), lambda b, h: (b, kcol + h)),
            pl.BlockSpec((nctx, LANES), lambda b, h: (b, col_v + h)),
            pl.BlockSpec((L, LANES), lambda b, h: (b, col_zd + h)),
        ] + common,
        out_specs=pl.BlockSpec((L, LANES), lambda b, h: (b, h)),
        out_shape=jax.ShapeDtypeStruct((ml, BRANCH), BF16),
        scratch_shapes=[
            pltpu.VMEM((LANES, nk), BF16),
            pltpu.VMEM((nk, 2 * DA_VDIM), BF16),
            pltpu.VMEM((2 * tq, nk), F32), pltpu.VMEM((2 * tq, nk), F32),
            pltpu.VMEM((2 * tq, LANES), F32), pltpu.VMEM((2 * tq, LANES), F32),
        ],
        compiler_params=_cparams(("arbitrary", "arbitrary")),
        name="diff_attn_lat",
    )(qk_lat, qk_lat, hin_lat, qk_ctx, hin_ctx, hin_lat, *cargs)
    y_ctx = pl.pallas_call(
        functools.partial(_diff_attn_ctx_kernel, lam_scale=lam_scale),
        grid=(batch, DA_HEADS),
        in_specs=[
            pl.BlockSpec((nctx, LANES), lambda b, h: (b, h)),
            pl.BlockSpec((nctx, LANES), lambda b, h: (b, kcol + h)),
            pl.BlockSpec((nctx, LANES), lambda b, h: (b, col_v + h)),
            pl.BlockSpec((nctx, LANES), lambda b, h: (b, col_zd + h)),
        ] + common,
        out_specs=pl.BlockSpec((nctx, LANES), lambda b, h: (b, h)),
        out_shape=jax.ShapeDtypeStruct((mc, BRANCH), BF16),
        compiler_params=_cparams(("arbitrary", "arbitrary")),
        name="diff_attn_ctx",
    )(qk_ctx, qk_ctx, hin_ctx, hin_ctx, *cargs)
    return y_lat, y_ctx


def seg_ones():
    seg = jnp.arange(MXU_N) // DA_HEAD
    return (seg[:, None] == seg[None, :]).astype(BF16)


def rope_tables(n_tokens):
    rows = n_tokens // GRID_W
    row = jnp.repeat(jnp.arange(rows, dtype=F32), GRID_W)
    col = jnp.tile(jnp.arange(GRID_W, dtype=F32), rows)
    n_freq = DA_HEAD // 4
    inv_freq = ROPE_BASE ** (-jnp.arange(n_freq, dtype=F32) / n_freq)
    ang_r = row[:, None] * inv_freq
    ang_c = col[:, None] * inv_freq
    ang = jnp.concatenate([ang_r, ang_r, ang_c, ang_c], axis=-1)
    sign = jnp.tile(jnp.concatenate([-jnp.ones(n_freq, F32), jnp.ones(n_freq, F32)]), 2)
    cos = jnp.tile(jnp.cos(ang), (1, 2))
    sin_signed = jnp.tile(jnp.sin(ang) * sign, (1, 2))
    return cos, sin_signed


def _mm_out_kernel(*refs, n_parts, with_next):
    a_refs, w_refs = refs[:n_parts], refs[n_parts:2 * n_parts]
    rest = refs[2 * n_parts:]
    if with_next:
        x_ref, g_ref, gs_ref, sh_ref, o_ref, hn_ref = rest
    else:
        x_ref, g_ref, o_ref = rest
    acc = None
    for a_ref, w_ref in zip(a_refs, w_refs):
        t = jnp.dot(a_ref[...], w_ref[...], preferred_element_type=F32)
        acc = t if acc is None else acc + t
    xn = x_ref[...] + g_ref[...] * acc
    o_ref[...] = xn
    if with_next:
        hn_ref[...] = _modulate(xn, gs_ref[...], sh_ref[...]).astype(hn_ref.dtype)


def mm_out(a_parts, w_stack, layer, x, gate, next_mod, rows_per_mod):
    m = x.shape[0]
    n_parts = len(a_parts)
    kp = w_stack.shape[1] // n_parts
    tm = min(512, rows_per_mod)
    tpm = rows_per_mod // tm
    mod_spec = pl.BlockSpec((None, 1, D_MODEL), lambda i: (i // tpm, 0, 0))
    in_specs = [pl.BlockSpec((tm, kp), lambda i: (i, 0)) for _ in a_parts]
    in_specs += [pl.BlockSpec((None, kp, D_MODEL), functools.partial(lambda i, p: (layer, p, 0), p=p))
                 for p in range(n_parts)]
    in_specs += [pl.BlockSpec((tm, D_MODEL), lambda i: (i, 0)), mod_spec]
    args = [*a_parts, *([w_stack] * n_parts), x, gate]
    out_specs = [pl.BlockSpec((tm, D_MODEL), lambda i: (i, 0))]
    out_shape = [jax.ShapeDtypeStruct((m, D_MODEL), F32)]
    if next_mod is not None:
        in_specs += [mod_spec, mod_spec]
        args += list(next_mod)
        out_specs.append(pl.BlockSpec((tm, D_MODEL), lambda i: (i, 0)))
        out_shape.append(jax.ShapeDtypeStruct((m, D_MODEL), BF16))
    outs = pl.pallas_call(
        functools.partial(_mm_out_kernel, n_parts=n_parts, with_next=next_mod is not None),
        grid=(m // tm,),
        in_specs=in_specs,
        out_specs=out_specs,
        out_shape=out_shape,
        compiler_params=_cparams(("arbitrary",)),
        name="mm_out",
    )(*args)
    return (outs[0], outs[1]) if next_mod is not None else (outs[0], None)


GLA_STEP = 4


def _seg_cumsum(x, reverse):
    n = x.shape[0]
    pos = lax.broadcasted_iota(jnp.int32, x.shape, 0) & (GLA_CHUNK - 1)
    s = 1
    while s < GLA_CHUNK:
        if reverse:
            x = x + jnp.where(pos < GLA_CHUNK - s, pltpu.roll(x, n - s, 0), 0.0)
        else:
            x = x + jnp.where(pos >= s, pltpu.roll(x, s, 0), 0.0)
        s *= 2
    return x


def _gla_kernel(ql_ref, kl_ref, vl_ref, rl_ref, zl_ref, qc_ref, kc_ref, vc_ref, rc_ref, zc_ref,
                wa_ref, ba_ref, ng_ref, *rest, ctx_out):
    if ctx_out:
        ol_ref, oc_ref = rest[:2]
        rest = rest[2:]
    else:
        ol_ref, oc_ref = rest[0], None
        rest = rest[1:]
    sf_ref, sb_ref, kv_ref, sin_ref, qd_ref, kd_ref, b_ref, vb_ref, ofl_ref, ofc_ref, obl_ref, obc_ref = rest
    C = GLA_CHUNK
    L, nctx = ql_ref.shape[0], kc_ref.shape[0]
    nl, nc = L // C, nctx // C
    row = lax.broadcasted_iota(jnp.int32, (C, C), 0)
    col = lax.broadcasted_iota(jnp.int32, (C, C), 1)

    for base, n, q_ref, k_ref, v_ref, r_ref in ((0, nctx, qc_ref, kc_ref, vc_ref, rc_ref),
                                                (nctx, L, ql_ref, kl_ref, vl_ref, rl_ref)):
        rows = slice(base, base + n)
        vb_ref[rows, :] = v_ref[...].astype(BF16)
        r = r_ref[...].astype(BF16)
        k = k_ref[...].astype(F32)
        with_q = ctx_out or base > 0
        for d in range(2):
            logits = jnp.dot(r, wa_ref[d], preferred_element_type=F32) + ba_ref[d]
            ls = jnp.minimum(logits, 0.0) - jnp.log(1.0 + jnp.exp(-jnp.abs(logits)))
            b = _seg_cumsum(ls * (1.0 / GLA_TAU), reverse=d == 1)
            b_ref[d, rows, :] = b
            kd_ref[d, rows, :] = (k * jnp.exp(-b)).astype(BF16)
            if with_q:
                qd_ref[d, rows, :] = (q_ref[...].astype(F32) * (GLA_DK ** -0.5) * jnp.exp(b)).astype(BF16)

    sf_ref[...] = jnp.zeros_like(sf_ref)
    sb_ref[...] = jnp.zeros_like(sb_ref)

    def segment(base, n_chunks, of_ref, ob_ref):
        assert n_chunks % GLA_STEP == 0

        def body(i, carry):
            work = []
            for u in range(GLA_STEP):
                cf = i * GLA_STEP + u
                cb = n_chunks - 1 - cf
                work.append((0, u, pl.multiple_of(base + cf * C, C), pl.multiple_of(cf * C, C)))
                work.append((1, GLA_STEP + u, pl.multiple_of(base + cb * C, C), pl.multiple_of(cb * C, C)))
            outs = {0: of_ref, 1: ob_ref}
            with_out = of_ref is not None
            scores, decs = {}, {}
            if with_out:
                for d, slot, row0, _ in work:
                    rows = pl.ds(row0, C)
                    scores[slot] = lax.dot_general(qd_ref[d, rows, :], kd_ref[d, rows, :],
                                                   (((1,), (1,)), ((), ())), preferred_element_type=F32)
            for d, slot, row0, _ in work:
                rows = pl.ds(row0, C)
                dec = jnp.exp(b_ref[d, pl.ds(row0 if d == 1 else row0 + C - 1, 1), :])
                k_end = (kd_ref[d, rows, :].astype(F32) * dec).astype(BF16)
                kv_ref[slot] = lax.dot_general(k_end, vb_ref[rows, :], (((0,), (0,)), ((), ())),
                                               preferred_element_type=F32)
                decs[slot] = jnp.transpose(jnp.broadcast_to(dec, (LANES, GLA_DK)))[:, 0:1]
            if with_out:
                for d, slot, row0, orow0 in work:
                    incl = (col >= row) if d == 1 else (col <= row)
                    sc = jnp.where(incl, scores[slot], 0.0).astype(BF16)
                    outs[d][pl.ds(orow0, C), :] = jnp.dot(sc, vb_ref[pl.ds(row0, C), :],
                                                          preferred_element_type=F32)
            for d, slot, _, _ in work:
                s_ref = sb_ref if d == 1 else sf_ref
                s_old = s_ref[...]
                sin_ref[slot] = s_old.astype(BF16)
                s_ref[...] = decs[slot] * s_old + kv_ref[slot]
            if with_out:
                for d, slot, row0, orow0 in work:
                    outs[d][pl.ds(orow0, C), :] += jnp.dot(qd_ref[d, pl.ds(row0, C), :], sin_ref[slot],
                                                           preferred_element_type=F32)
            return carry

        lax.fori_loop(0, n_chunks // GLA_STEP, body, 0)

    segment(0, nc, ofc_ref if ctx_out else None, obc_ref if ctx_out else None)
    segment(nctx, nl, ofl_ref, obl_ref)
    def finish(of_ref, ob_ref, z_ref, o_ref):
        o = of_ref[...] + ob_ref[...]
        z = z_ref[...].astype(F32)
        y = o * lax.rsqrt(jnp.mean(o * o, axis=-1, keepdims=True) + EPS) * ng_ref[...]
        o_ref[...] = (y * (z * jax.nn.sigmoid(z))).astype(o_ref.dtype)

    finish(ofl_ref, obl_ref, zl_ref, ol_ref)
    if ctx_out:
        finish(ofc_ref, obc_ref, zc_ref, oc_ref)


def gla_mix(hin_lat, hin_ctx, r_lat, r_ctx, wa2p, ba, norm_g, batch, ctx_out):
    ml, mc = hin_lat.shape[0], hin_ctx.shape[0]
    L, nctx = ml // batch, mc // batch
    kb = GLA_KEY // GLA_DK
    vb = 2 * GLA_KEY // GLA_DV
    zb = (2 * GLA_KEY + GLA_VAL) // GLA_DV

    def seg_specs(n):
        return [pl.BlockSpec((n, GLA_DK), lambda b, h: (b, h)),
                pl.BlockSpec((n, GLA_DK), lambda b, h: (b, kb + h)),
                pl.BlockSpec((n, GLA_DV), lambda b, h: (b, vb + h)),
                pl.BlockSpec((n, LANES), lambda b, h: (b, 0)),
                pl.BlockSpec((n, GLA_DV), lambda b, h: (b, zb + h))]

    in_specs = seg_specs(L) + seg_specs(nctx) + [
        pl.BlockSpec((2, LANES, GLA_DK), lambda b, h: (0, 0, h)),
        pl.BlockSpec((2, 1, GLA_DK), lambda b, h: (0, 0, h)),
        pl.BlockSpec((1, GLA_DV), lambda b, h: (0, 0)),
    ]
    out_specs = [pl.BlockSpec((L, GLA_DV), lambda b, h: (b, h)),
                 pl.BlockSpec((nctx, GLA_DV), lambda b, h: (b, h))]
    out_shape = [jax.ShapeDtypeStruct((ml, GLA_VAL), BF16), jax.ShapeDtypeStruct((mc, GLA_VAL), BF16)]
    if not ctx_out:
        out_specs, out_shape = out_specs[:1], out_shape[:1]
    outs = pl.pallas_call(
        functools.partial(_gla_kernel, ctx_out=ctx_out),
        grid=(batch, GLA_HEADS),
        in_specs=in_specs,
        out_specs=out_specs,
        out_shape=out_shape,
        scratch_shapes=[
            pltpu.VMEM((GLA_DK, GLA_DV), F32), pltpu.VMEM((GLA_DK, GLA_DV), F32),
            pltpu.VMEM((2 * GLA_STEP, GLA_DK, GLA_DV), F32),
            pltpu.VMEM((2 * GLA_STEP, GLA_DK, GLA_DV), BF16),
            pltpu.VMEM((2, nctx + L, GLA_DK), BF16),
            pltpu.VMEM((2, nctx + L, GLA_DK), BF16),
            pltpu.VMEM((2, nctx + L, GLA_DK), F32),
            pltpu.VMEM((nctx + L, GLA_DV), BF16),
            pltpu.VMEM((L, GLA_DV), F32), pltpu.VMEM((nctx, GLA_DV), F32),
            pltpu.VMEM((L, GLA_DV), F32), pltpu.VMEM((nctx, GLA_DV), F32),
        ],
        compiler_params=_cparams(("arbitrary", "arbitrary")),
        name="gla_mix",
    )(hin_lat, hin_lat, hin_lat, r_lat, hin_lat, hin_ctx, hin_ctx, hin_ctx, r_ctx, hin_ctx, wa2p, ba, norm_g)
    return (outs[0], outs[1]) if ctx_out else (outs[0], None)


def _mods(mod_l, norm_g, batch):
    shift, scale, gate = mod_l[:, :D_MODEL], mod_l[:, D_MODEL:2 * D_MODEL], mod_l[:, 2 * D_MODEL:]
    gs = (norm_g.astype(F32)[None, :] * (1.0 + scale))[:, None, :]
    sh = shift[:, None, :]
    gt = gate[:, None, :]
    lat = (gs[:batch], sh[:batch], gt[:batch])
    ctx = (gs[batch:batch + 1], sh[batch:batch + 1], gt[batch:batch + 1])
    return lat, ctx


def even_layer(x_lat, x_ctx, hn_l, hn_c, gates, next_mods, w_in, w_out, j, s5p, w_glu, qn_g, kn_g,
               lam_vecs, subln_g, lam_init, batch, rope):
    L = x_lat.shape[0] // batch
    gt_l, gt_c = gates
    rest = 5 * BRANCH
    u3_l = mm_in(hn_l, w_in, j, 0, BRANCH, out_dtype=BF16, gb_out=True, name="mm_in_u")
    u3_c = mm_in(hn_c, w_in, j, 0, BRANCH, out_dtype=BF16, gb_out=True, name="mm_in_u_ctx")
    hin_l = mm_in(hn_l, w_in, j, BRANCH, rest, out_dtype=BF16, name="mm_in_rest")
    hin_c = mm_in(hn_c, w_in, j, BRANCH, rest, out_dtype=BF16, name="mm_in_rest_ctx")

    ez, kl, cl, ptab, sel = s5p
    h_l, h_c = s5_states(u3_l, u3_c, ez, ptab, j, batch)
    y_l, y_c = s5_outputs(u3_l, u3_c, h_l, h_c, kl, cl, sel, j)
    wg = w_glu.astype(BF16)
    a_s5_l = s5_finish(y_l, wg, hin_l, 0)
    a_s5_c = s5_finish(y_c, wg, hin_c, 0)

    lv = lam_vecs.astype(F32)
    lam = jnp.exp(jnp.sum(lv[0] * lv[1])) - jnp.exp(jnp.sum(lv[2] * lv[3])) + lam_init
    lam_row = jnp.full((1, LANES), lam, F32)
    qg = jnp.tile(qn_g.astype(F32), 2)[None, :]
    kg = jnp.tile(kn_g.astype(F32), 2)[None, :]
    sg = subln_g.astype(F32)[None, :]
    cb = BRANCH // LANES
    cos_t, sin_t, ones_bd = rope
    a_da_l, a_da_c = diff_attention(hin_l, hin_c, cb, 2 * cb, 3 * cb, 4 * cb, batch, qg, kg, sg, lam_row,
                                    lam_init, cos_t, sin_t, ones_bd)

    nm_l, nm_c = next_mods
    x_lat, hn_l = mm_out([a_s5_l, a_da_l], w_out, j, x_lat, gt_l, nm_l, L)
    x_ctx, hn_c = mm_out([a_s5_c, a_da_c], w_out, j, x_ctx, gt_c, nm_c, x_ctx.shape[0])
    return x_lat, x_ctx, hn_l, hn_c


def odd_layer(x_lat, x_ctx, hn_l, hn_c, gates, next_mods, w_in, w_out, j, wa1, wa2, ba, gla_norm_g, batch,
              with_ctx_out):
    L = x_lat.shape[0] // batch
    gt_l, gt_c = gates
    n = 3 * D_MODEL
    w_aux = jnp.zeros((D_MODEL, LANES), F32).at[:, :GLA_RANK].set(wa1[0]).at[:, GLA_RANK:2 * GLA_RANK].set(wa1[1])
    w_aux = w_aux.astype(BF16)
    hin_l, r_l = mm_in(hn_l, w_in, j, 0, n, out_dtype=BF16, w_aux=w_aux, name="mm_in_odd")
    hin_c, r_c = mm_in(hn_c, w_in, j, 0, n, out_dtype=BF16, w_aux=w_aux, name="mm_in_odd_ctx")
    wa2p = jnp.zeros((2, LANES, GLA_KEY), F32)
    wa2p = wa2p.at[0, :GLA_RANK].set(wa2[0]).at[1, GLA_RANK:2 * GLA_RANK].set(wa2[1]).astype(BF16)
    ng = gla_norm_g.astype(F32)[None, :]
    a_l, a_c = gla_mix(hin_l, hin_c, r_l, r_c, wa2p, ba.astype(F32).reshape(2, 1, GLA_KEY), ng, batch,
                       with_ctx_out)
    nm_l, nm_c = next_mods
    x_lat, hn_l = mm_out([a_l], w_out, j, x_lat, gt_l, nm_l, L)
    if with_ctx_out:
        x_ctx, hn_c = mm_out([a_c], w_out, j, x_ctx, gt_c, nm_c, x_ctx.shape[0])
    return x_lat, x_ctx, hn_l, hn_c


def kernel(x, c, ctx, c_ctx, ada_w, ada_b, norm_g, ev_w_in, ev_w_out, s5_a_re, s5_a_im, s5_log_dt, s5_b_re, s5_b_im, s5_c_re, s5_c_im, s5_d, s5_w_glu, da_qn_g, da_kn_g, da_lam, da_subln_g, od_w_in, od_w_out, gla_wa1, gla_wa2, gla_ba, gla_norm_g):
    batch, L, _ = x.shape
    nctx = ctx.shape[1]
    x_lat = x.reshape(batch * L, D_MODEL)
    x_ctx = ctx.reshape(batch * nctx, D_MODEL)
    cond = jnp.zeros((8, D_MODEL), F32).at[:batch].set(c).at[batch].set(c_ctx)
    mod = modulation_all(cond, ada_w, ada_b)

    cos_t, sin_t = rope_tables(L)
    rope = (cos_t, sin_t, seg_ones())
    mods = [_mods(mod[i], norm_g[i], batch) for i in range(DEPTH)]
    ev_w_out_b, od_w_out_b = ev_w_out.astype(BF16), od_w_out.astype(BF16)

    s5p = jax.vmap(functools.partial(s5_prepare, ncl=L // S5_T))(
        s5_a_re, s5_a_im, s5_log_dt, s5_b_re, s5_b_im, s5_c_re, s5_c_im, s5_d) + (s5_select_matrix(),)

    (gs_l, sh_l, _), (gs_c, sh_c, _) = mods[0]
    hn_l = prenorm(x_lat, gs_l, sh_l, L)
    hn_c = prenorm(x_ctx, gs_c, sh_c, x_ctx.shape[0])
    for i in range(DEPTH):
        j = i // 2
        with_ctx_out = i < DEPTH - 1
        gates = (mods[i][0][2], mods[i][1][2])
        next_mods = (mods[i + 1][0][:2], mods[i + 1][1][:2]) if with_ctx_out else (None, None)
        if i % 2 == 0:
            lam_init = 0.8 - 0.6 * math.exp(-0.3 * i)
            x_lat, x_ctx, hn_l, hn_c = even_layer(
                x_lat, x_ctx, hn_l, hn_c, gates, next_mods, ev_w_in, ev_w_out_b, j, s5p, s5_w_glu[j],
                da_qn_g[j], da_kn_g[j], da_lam[j], da_subln_g[j], lam_init, batch, rope)
        else:
            x_lat, x_ctx, hn_l, hn_c = odd_layer(
                x_lat, x_ctx, hn_l, hn_c, gates, next_mods, od_w_in, od_w_out_b, j, gla_wa1[j], gla_wa2[j],
                gla_ba[j], gla_norm_g[j], batch, with_ctx_out)
    return x_lat.reshape(batch, L, D_MODEL)
```

```python
import functools
import math

import jax
import jax.numpy as jnp
from jax import lax
from jax.experimental import pallas as pl
from jax.experimental.pallas import tpu as pltpu

F32 = jnp.float32
BF16 = jnp.bfloat16

D_MODEL = 2048
DEPTH = 4
GRID_W = 64
EPS = 1e-6
BRANCH = D_MODEL // 2
S5_GROUP = 16
S5_GROUPS = BRANCH // S5_GROUP
S5_STATE = 64
DA_HEAD = 64
DA_HEADS = BRANCH // (2 * DA_HEAD)
DA_VDIM = 2 * DA_HEAD
ROPE_BASE = 10000.0
GLA_HEADS = 4
GLA_KEY = D_MODEL // 2
GLA_VAL = D_MODEL
GLA_DK = GLA_KEY // GLA_HEADS
GLA_DV = GLA_VAL // GLA_HEADS
GLA_RANK = 16
GLA_TAU = 16.0
GLA_CHUNK = 64

LANES = 128
VMEM_LIMIT = 56 * 1024 * 1024

S5_T = 16
S5_GB = BRANCH // LANES
S5_GPB = LANES // S5_GROUP
S5_W = S5_T * LANES
S5_SW = S5_GPB * S5_STATE


def _cparams(sem):
    return pltpu.CompilerParams(dimension_semantics=sem, vmem_limit_bytes=VMEM_LIMIT)


def _mod_kernel(c_ref, w_ref, b_ref, o_ref):
    c = c_ref[...]
    s = (c * jax.nn.sigmoid(c)).astype(BF16)
    acc = jnp.dot(s, w_ref[...].astype(BF16), preferred_element_type=F32)
    o_ref[...] = acc + b_ref[...]


def modulation_all(cond, ada_w, ada_b):
    tn = 512
    n = 3 * D_MODEL
    return pl.pallas_call(
        _mod_kernel,
        grid=(DEPTH, n // tn),
        in_specs=[
            pl.BlockSpec((8, D_MODEL), lambda l, j: (0, 0)),
            pl.BlockSpec((None, D_MODEL, tn), lambda l, j: (l, 0, j)),
            pl.BlockSpec((None, 1, tn), lambda l, j: (l, 0, j)),
        ],
        out_specs=pl.BlockSpec((None, 8, tn), lambda l, j: (l, 0, j)),
        out_shape=jax.ShapeDtypeStruct((DEPTH, 8, n), F32),
        compiler_params=_cparams(("arbitrary", "arbitrary")),
        name="modulation",
    )(cond, ada_w, ada_b.reshape(DEPTH, 1, n))


def _modulate(x, gs, sh):
    ms = jnp.mean(x * x, axis=-1, keepdims=True)
    return x * lax.rsqrt(ms + EPS) * gs + sh


def _prenorm_kernel(x_ref, gs_ref, sh_ref, o_ref):
    o_ref[...] = _modulate(x_ref[...], gs_ref[...], sh_ref[...]).astype(o_ref.dtype)


def prenorm(x, gs, sh, rows_per_mod):
    m = x.shape[0]
    tm = min(512, rows_per_mod)
    tpm = rows_per_mod // tm
    return pl.pallas_call(
        _prenorm_kernel,
        grid=(m // tm,),
        in_specs=[
            pl.BlockSpec((tm, D_MODEL), lambda i: (i, 0)),
            pl.BlockSpec((None, 1, D_MODEL), lambda i: (i // tpm, 0, 0)),
            pl.BlockSpec((None, 1, D_MODEL), lambda i: (i // tpm, 0, 0)),
        ],
        out_specs=pl.BlockSpec((tm, D_MODEL), lambda i: (i, 0)),
        out_shape=jax.ShapeDtypeStruct((m, D_MODEL), BF16),
        compiler_params=_cparams(("arbitrary",)),
        name="prenorm",
    )(x, gs, sh)


def _mm_in_kernel(a_ref, w_ref, *rest, gb_out, has_aux):
    if has_aux:
        wa_ref, o_ref, aux_ref = rest

        @pl.when(pl.program_id(1) == 0)
        def _():
            aux_ref[...] = jnp.dot(a_ref[...], wa_ref[...], preferred_element_type=F32)
    elif gb_out:
        o_ref, stage_ref = rest
    else:
        (o_ref,) = rest

    acc = jnp.dot(a_ref[...], w_ref[...].astype(BF16), preferred_element_type=F32)
    if gb_out:
        n_rows = acc.shape[0] // S5_T
        for q in range(acc.shape[1] // LANES):
            stage_ref[q] = acc[:, q * LANES:(q + 1) * LANES]
        for q in range(acc.shape[1] // LANES):
            for t in range(S5_T):
                o_ref[q, :, t * LANES:(t + 1) * LANES] = stage_ref[q, pl.ds(t, n_rows, stride=S5_T), :].astype(
                    o_ref.dtype)
    else:
        o_ref[...] = acc.astype(o_ref.dtype)


def mm_in(a, w_stack, layer, col0, ncols, *, out_dtype, gb_out=False, w_aux=None, name="mm_in"):
    m = a.shape[0]
    tm = min(2048, m)
    tn = 512
    assert m % tm == 0 and ncols % tn == 0 and col0 % tn == 0
    jb = col0 // tn
    in_specs = [
        pl.BlockSpec((tm, D_MODEL), lambda i, j: (i, 0)),
        pl.BlockSpec((None, D_MODEL, tn), lambda i, j: (layer, 0, j + jb)),
    ]
    args = [a, w_stack]
    scratch = []
    if gb_out:
        out_shape = [jax.ShapeDtypeStruct((ncols // LANES, m // S5_T, S5_W), out_dtype)]
        out_specs = [pl.BlockSpec((tn // LANES, tm // S5_T, S5_W), lambda i, j: (j, i, 0))]
        scratch = [pltpu.VMEM((tn // LANES, tm, LANES), F32)]
    else:
        out_shape = [jax.ShapeDtypeStruct((m, ncols), out_dtype)]
        out_specs = [pl.BlockSpec((tm, tn), lambda i, j: (i, j))]
    if w_aux is not None:
        in_specs.append(pl.BlockSpec((D_MODEL, LANES), lambda i, j: (0, 0)))
        args.append(w_aux)
        out_shape.append(jax.ShapeDtypeStruct((m, LANES), F32))
        out_specs.append(pl.BlockSpec((tm, LANES), lambda i, j: (i, 0)))
    outs = pl.pallas_call(
        functools.partial(_mm_in_kernel, gb_out=gb_out, has_aux=w_aux is not None),
        grid=(m // tm, ncols // tn),
        in_specs=in_specs,
        out_specs=out_specs,
        out_shape=out_shape,
        scratch_shapes=scratch,
        compiler_params=_cparams(("arbitrary", "arbitrary")),
        name=name,
    )(*args)
    return outs if w_aux is not None else outs[0]


def _shift_rows(h, s, up):
    n = h.shape[0]
    row = lax.broadcasted_iota(jnp.int32, h.shape, 0)
    if up:
        return jnp.where(row >= n - s, 0.0, pltpu.roll(h, n - s, 0))
    return jnp.where(row < s, 0.0, pltpu.roll(h, s, 0))


def _chunk_scan(zr, zi, pr, pi, reverse):
    n = zr.shape[0]
    ntab = pr.shape[0]
    hr, hi = zr, zi
    s = 1
    while s < n:
        idx = ntab - 1 - s if reverse else s
        ar, ai = pr[idx:idx + 1], pi[idx:idx + 1]
        sr, si = _shift_rows(hr, s, reverse), _shift_rows(hi, s, reverse)
        hr, hi = hr + ar * sr - ai * si, hi + ar * si + ai * sr
        s *= 2
    return hr, hi


def _group_mask(shape, row_shift, col_shift):
    rg = lax.broadcasted_iota(jnp.int32, shape, 0) >> row_shift
    cg = (lax.broadcasted_iota(jnp.int32, shape, 1) >> col_shift) & (S5_GPB - 1)
    return rg == cg


def _s5_state_kernel(xl_ref, xc_ref, ez_ref, p_ref, hl_ref, hc_ref, wz_ref):
    ncc = xc_ref.shape[0]

    @pl.when(pl.program_id(1) == 0)
    def _():
        mask = _group_mask((LANES, S5_SW), 4, 6)
        for t in range(S5_T):
            for q in range(4):
                e = jnp.concatenate([ez_ref[t, q]] * S5_GPB, axis=0)
                wz_ref[t * LANES:(t + 1) * LANES, q * S5_SW:(q + 1) * S5_SW] = jnp.where(
                    mask, e, jnp.zeros_like(e))

    x = jnp.concatenate([xc_ref[...], xl_ref[...]], axis=0)
    z = jnp.dot(x, wz_ref[...], preferred_element_type=F32)
    p = p_ref[...]
    outs_c, outs_l = [], []
    for d in range(2):
        rev = d == 1
        c0 = 2 * d * S5_SW
        zr, zi = z[:, c0:c0 + S5_SW], z[:, c0 + S5_SW:c0 + 2 * S5_SW]
        pr, pi = p[:, c0:c0 + S5_SW], p[:, c0 + S5_SW:c0 + 2 * S5_SW]
        cr, ci = _chunk_scan(zr[:ncc], zi[:ncc], pr, pi, rev)
        lr, li = _chunk_scan(zr[ncc:], zi[ncc:], pr, pi, rev)
        if rev:
            car_r, car_i = cr[0:1], ci[0:1]
        else:
            car_r, car_i = cr[ncc - 1:ncc], ci[ncc - 1:ncc]
        hcr, hci = _shift_rows(cr, 1, rev), _shift_rows(ci, 1, rev)
        hlr = _shift_rows(lr, 1, rev) + pr * car_r - pi * car_i
        hli = _shift_rows(li, 1, rev) + pr * car_i + pi * car_r
        outs_c += [hcr, hci]
        outs_l += [hlr, hli]
    hc_ref[...] = jnp.concatenate(outs_c, axis=1).astype(hc_ref.dtype)
    hl_ref[...] = jnp.concatenate(outs_l, axis=1).astype(hl_ref.dtype)


def s5_states(u3_lat, u3_ctx, ez, ptab, layer, batch):
    ncl = u3_lat.shape[1] // batch
    ncc = u3_ctx.shape[1] // batch
    assert ptab.shape[2] == ncl
    return pl.pallas_call(
        _s5_state_kernel,
        grid=(S5_GB, batch),
        in_specs=[
            pl.BlockSpec((None, ncl, S5_W), lambda g, b: (g, b, 0)),
            pl.BlockSpec((None, ncc, S5_W), lambda g, b: (g, b, 0)),
            pl.BlockSpec((None, None, S5_T, 4, S5_GROUP, S5_SW), lambda g, b: (layer, g, 0, 0, 0, 0)),
            pl.BlockSpec((None, None, ncl, 4 * S5_SW), lambda g, b: (layer, g, 0, 0)),
        ],
        out_specs=[
            pl.BlockSpec((None, ncl, 4 * S5_SW), lambda g, b: (g, b, 0)),
            pl.BlockSpec((None, ncc, 4 * S5_SW), lambda g, b: (g, b, 0)),
        ],
        out_shape=[
            jax.ShapeDtypeStruct((S5_GB, u3_lat.shape[1], 4 * S5_SW), BF16),
            jax.ShapeDtypeStruct((S5_GB, u3_ctx.shape[1], 4 * S5_SW), BF16),
        ],
        scratch_shapes=[pltpu.VMEM((S5_W, 4 * S5_SW), BF16)],
        compiler_params=_cparams(("arbitrary", "arbitrary")),
        name="s5_states",
    )(u3_lat, u3_ctx, ez, ptab)


S5_TAPS = 2 * S5_T
S5_SEL_IN = LANES
S5_SEL_OUT = (LANES // S5_GROUP) * LANES


def _s5_out_kernel(xl_ref, xc_ref, hl_ref, hc_ref, kl_ref, cl_ref, sel_ref, yl_ref, yc_ref, wt_ref, wm_ref):
    nl = xl_ref.shape[0]
    sel = sel_ref[...]
    taps = []
    mask_t = _group_mask((LANES, S5_SEL_OUT), 4, 4)
    for a in range(S5_TAPS * S5_GROUP // S5_SEL_IN):
        t = jnp.dot(kl_ref[:, a * S5_SEL_IN:(a + 1) * S5_SEL_IN], sel, preferred_element_type=F32)
        taps.append(jnp.where(mask_t, t, 0.0).astype(BF16))
    taps = jnp.concatenate(taps, axis=1)
    for s in range(S5_T):
        lo = (S5_T - 1 - s) * LANES
        wt_ref[s * LANES:(s + 1) * LANES, :] = taps[:, lo:lo + S5_W]
    mask_m = _group_mask((S5_SW, S5_SEL_OUT), 6, 4)
    for q in range(4):
        for a in range(S5_T * S5_GROUP // S5_SEL_IN):
            m = jnp.dot(cl_ref[q, :, a * S5_SEL_IN:(a + 1) * S5_SEL_IN], sel, preferred_element_type=F32)
            wm_ref[q * S5_SW:(q + 1) * S5_SW, a * S5_SEL_OUT:(a + 1) * S5_SEL_OUT] = jnp.where(
                mask_m, m, 0.0).astype(BF16)
    x = jnp.concatenate([xl_ref[...], xc_ref[...]], axis=0)
    h = jnp.concatenate([hl_ref[...], hc_ref[...]], axis=0)
    y = jnp.dot(x, wt_ref[...], preferred_element_type=F32)
    y = y + jnp.dot(h, wm_ref[...], preferred_element_type=F32)
    yl_ref[...] = y[:nl].astype(yl_ref.dtype)
    yc_ref[...] = y[nl:].astype(yc_ref.dtype)


def s5_outputs(u3_lat, u3_ctx, h_lat, h_ctx, kl, cl, sel, layer):
    nl, nc = u3_lat.shape[1], u3_ctx.shape[1]
    return pl.pallas_call(
        _s5_out_kernel,
        grid=(S5_GB,),
        in_specs=[
            pl.BlockSpec((None, nl, S5_W), lambda g: (g, 0, 0)),
            pl.BlockSpec((None, nc, S5_W), lambda g: (g, 0, 0)),
            pl.BlockSpec((None, nl, 4 * S5_SW), lambda g: (g, 0, 0)),
            pl.BlockSpec((None, nc, 4 * S5_SW), lambda g: (g, 0, 0)),
            pl.BlockSpec((None, None, LANES, S5_TAPS * S5_GROUP), lambda g: (layer, g, 0, 0)),
            pl.BlockSpec((None, 4, None, S5_SW, S5_T * S5_GROUP), lambda g: (layer, 0, g, 0, 0)),
            pl.BlockSpec((S5_SEL_IN, S5_SEL_OUT), lambda g: (0, 0)),
        ],
        out_specs=[
            pl.BlockSpec((None, nl, S5_W), lambda g: (g, 0, 0)),
            pl.BlockSpec((None, nc, S5_W), lambda g: (g, 0, 0)),
        ],
        out_shape=[
            jax.ShapeDtypeStruct((S5_GB, nl, S5_W), BF16),
            jax.ShapeDtypeStruct((S5_GB, nc, S5_W), BF16),
        ],
        scratch_shapes=[pltpu.VMEM((S5_W, S5_W), BF16), pltpu.VMEM((4 * S5_SW, S5_W), BF16)],
        compiler_params=_cparams(("arbitrary",)),
        name="s5_outputs",
    )(u3_lat, u3_ctx, h_lat, h_ctx, kl, cl, sel)


def _s5_finish_kernel(y_ref, wg_ref, zs_ref, o_ref, stage_ref):
    n_rows = y_ref.shape[1]
    for q in range(S5_GB):
        for t in range(S5_T):
            stage_ref[q, pl.ds(t, n_rows, stride=S5_T), :] = y_ref[q, :, t * LANES:(t + 1) * LANES].astype(F32)
    y = jnp.concatenate([stage_ref[q] for q in range(S5_GB)], axis=1)
    g = jax.nn.gelu(y)
    t = jnp.dot(g.astype(BF16), wg_ref[...], preferred_element_type=F32)
    zs = zs_ref[...].astype(F32)
    o_ref[...] = (g * jax.nn.sigmoid(t) * (zs * jax.nn.sigmoid(zs))).astype(o_ref.dtype)


def s5_finish(y3, w_glu, hin, zs_block):
    m = y3.shape[1] * S5_T
    tm = min(512, m)
    return pl.pallas_call(
        _s5_finish_kernel,
        grid=(m // tm,),
        in_specs=[
            pl.BlockSpec((S5_GB, tm // S5_T, S5_W), lambda i: (0, i, 0)),
            pl.BlockSpec((BRANCH, BRANCH), lambda i: (0, 0)),
            pl.BlockSpec((tm, BRANCH), lambda i: (i, zs_block)),
        ],
        out_specs=pl.BlockSpec((tm, BRANCH), lambda i: (i, 0)),
        out_shape=jax.ShapeDtypeStruct((m, BRANCH), BF16),
        scratch_shapes=[pltpu.VMEM((S5_GB, tm, LANES), F32)],
        compiler_params=_cparams(("arbitrary",)),
        name="s5_finish",
    )(y3, w_glu, hin)


def s5_prepare(a_re, a_im, log_dt, b_re, b_im, c_re, c_im, d_skip, ncl):
    T, G, P, C = S5_T, S5_GROUPS, S5_STATE, S5_GROUP
    GB, GPB, SW = S5_GB, S5_GPB, S5_SW
    hp = lax.Precision.HIGHEST
    a_re, a_im = a_re.astype(F32), a_im.astype(F32)
    dt = jnp.exp(log_dt.astype(F32))[..., None]
    la_re, la_im = a_re * dt, a_im * dt
    mag = jnp.exp(la_re)
    lb_re, lb_im = mag * jnp.cos(la_im), mag * jnp.sin(la_im)
    nr, ni = lb_re - 1.0, lb_im
    den = a_re * a_re + a_im * a_im
    f_re = (nr * a_re + ni * a_im) / den
    f_im = (ni * a_re - nr * a_im) / den
    bb_re = f_re[..., None] * b_re - f_im[..., None] * b_im
    bb_im = f_re[..., None] * b_im + f_im[..., None] * b_re

    def cpow(k, lr, li):
        m = jnp.exp(k * lr)
        return m * jnp.cos(k * li), m * jnp.sin(k * li)

    lad_re, lad_im = la_re.reshape(2, GB, 1, SW), la_im.reshape(2, GB, 1, SW)
    tt = jnp.arange(T, dtype=F32).reshape(1, 1, T, 1)
    pw_re, pw_im = cpow(tt, lad_re, lad_im)
    to_lanes = lambda w: w.reshape(2, GB, GPB, P, C).transpose(0, 1, 4, 2, 3).reshape(2, GB, C, SW)
    bt_re, bt_im = to_lanes(bb_re), to_lanes(bb_im)
    e_re = pw_re[:, :, :, None] * bt_re[:, :, None] - pw_im[:, :, :, None] * bt_im[:, :, None]
    e_im = pw_re[:, :, :, None] * bt_im[:, :, None] + pw_im[:, :, :, None] * bt_re[:, :, None]
    ez = jnp.stack([e_re[0][:, ::-1], e_im[0][:, ::-1], e_re[1], e_im[1]], axis=2)

    cr = c_re.astype(F32).reshape(2, GB, GPB, C, P)
    ci = c_im.astype(F32).reshape(2, GB, GPB, C, P)
    e6_re, e6_im = e_re.reshape(2, GB, T, C, GPB, P), e_im.reshape(2, GB, T, C, GPB, P)
    kk = (jnp.einsum("dbgop,dbtigp->dbgito", cr, e6_re, precision=hp)
          - jnp.einsum("dbgop,dbtigp->dbgito", ci, e6_im, precision=hp))
    skip = jnp.eye(C, dtype=F32) * d_skip.astype(F32).reshape(GB, GPB, C, 1)
    center = kk[0][..., 0, :] + kk[1][..., 0, :] + skip
    kl = jnp.concatenate([kk[1][..., :0:-1, :], center[..., None, :], kk[0][..., 1:, :],
                          jnp.zeros((GB, GPB, C, 1, C), F32)], axis=3)
    kl = kl.reshape(GB, LANES, S5_TAPS * C)

    lar_re, lar_im = la_re[..., None], la_im[..., None]
    steps = jnp.stack([jnp.arange(1, T + 1, dtype=F32), jnp.arange(T, 0, -1).astype(F32)])
    pr_re, pr_im = cpow(steps.reshape(2, 1, 1, T), lar_re, lar_im)
    lane = jnp.arange(T * C)
    rep_c = (jnp.arange(C)[:, None] == lane[None, :] % C).astype(F32)
    rep_t = (jnp.arange(T)[:, None] == lane[None, :] // C).astype(F32)
    cp = jnp.stack([c_re, c_im]).astype(F32).transpose(0, 1, 2, 4, 3)
    cp = jnp.einsum("rdgpc,cq->rdgpq", cp, rep_c, precision=hp)
    pr = jnp.einsum("rdgpt,tq->rdgpq", jnp.stack([pr_re, pr_im]), rep_t, precision=hp)
    cl_re = cp[0] * pr[0] - cp[1] * pr[1]
    cl_im = cp[0] * pr[1] + cp[1] * pr[0]
    cl = jnp.stack([cl_re[0], -cl_im[0], cl_re[1], -cl_im[1]], axis=0)
    cl = cl.reshape(4, GB, SW, T * C)

    kf = jnp.arange(ncl, dtype=F32).reshape(1, ncl, 1) * float(T)
    af_re, af_im = cpow(kf, lad_re[0], lad_im[0])
    ab_re, ab_im = cpow(kf[:, ::-1], lad_re[1], lad_im[1])
    tab = jnp.concatenate([af_re, af_im, ab_re, ab_im], axis=-1)
    return ez.astype(BF16), kl.astype(BF16), cl.astype(BF16), tab


def s5_select_matrix():
    r = jnp.arange(S5_SEL_IN)
    q = jnp.arange(S5_SEL_OUT)
    return ((r[:, None] // S5_GROUP == q[None, :] // LANES)
            & (r[:, None] % S5_GROUP == q[None, :] % S5_GROUP)).astype(BF16)


MXU_N = 256
LOG2E = 1.4426950408889634


def _seg_mean_sq(x, ones_bd):
    sq = (x * x).astype(BF16)
    parts = [jnp.dot(sq[:, t:t + MXU_N], ones_bd, preferred_element_type=F32)
             for t in range(0, x.shape[1], MXU_N)]
    return jnp.concatenate(parts, axis=1) * (1.0 / DA_HEAD)


def _rope(x, cos, sin_signed):
    n = x.shape[1]
    lane = lax.broadcasted_iota(jnp.int32, x.shape, 1)
    first = (lane % (DA_HEAD // 2)) < (DA_HEAD // 4)
    rot = jnp.where(first, pltpu.roll(x, n - DA_HEAD // 4, 1), pltpu.roll(x, DA_HEAD // 4, 1))
    return x * cos + rot * sin_signed


def _qk_prep_kernel(q_ref, k_ref, *rest, rope):
    if rope:
        cos_ref, sin_ref, qg_ref, kg_ref, ones_ref, o_ref = rest
    else:
        qg_ref, kg_ref, ones_ref, o_ref = rest
    ones_bd = ones_ref[...]
    reps = BRANCH // LANES
    for x_ref, g_ref, scale, c0 in ((q_ref, qg_ref, DA_HEAD ** -0.5 * LOG2E, 0), (k_ref, kg_ref, 1.0, BRANCH)):
        x = x_ref[...].astype(F32)
        x = x * lax.rsqrt(_seg_mean_sq(x, ones_bd) + EPS) * jnp.concatenate([g_ref[...]] * reps, axis=1)
        if rope:
            x = _rope(x, jnp.concatenate([cos_ref[...]] * reps, axis=1),
                      jnp.concatenate([sin_ref[...]] * reps, axis=1))
        o_ref[:, c0:c0 + BRANCH] = (x * scale).astype(o_ref.dtype)


def qk_prep(hin, q_block, k_block, qg, kg, ones_bd, rope_tabs, seq_len):
    m = hin.shape[0]
    tm = min(512, m)
    rope = rope_tabs is not None
    in_specs = [pl.BlockSpec((tm, BRANCH), lambda i: (i, q_block)),
                pl.BlockSpec((tm, BRANCH), lambda i: (i, k_block))]
    args = [hin, hin]
    if rope:
        tps = seq_len // tm
        in_specs += [pl.BlockSpec((tm, LANES), lambda i: (i % tps, 0)),
                     pl.BlockSpec((tm, LANES), lambda i: (i % tps, 0))]
        args += list(rope_tabs)
    in_specs += [pl.BlockSpec((1, LANES), lambda i: (0, 0)), pl.BlockSpec((1, LANES), lambda i: (0, 0)),
                 pl.BlockSpec((MXU_N, MXU_N), lambda i: (0, 0))]
    args += [qg, kg, ones_bd]
    return pl.pallas_call(
        functools.partial(_qk_prep_kernel, rope=rope),
        grid=(m // tm,),
        in_specs=in_specs,
        out_specs=pl.BlockSpec((tm, 2 * BRANCH), lambda i: (i, 0)),
        out_shape=jax.ShapeDtypeStruct((m, 2 * BRANCH), BF16),
        compiler_params=_cparams(("arbitrary",)),
        name="qk_prep",
    )(*args)


def _stack_maps(q):
    lane = lax.broadcasted_iota(jnp.int32, q.shape, 1)
    return jnp.concatenate([jnp.where(lane < DA_HEAD, q, 0.0), jnp.where(lane >= DA_HEAD, q, 0.0)], axis=0)


def _diff_combine(pv, tq, lam, sg, lam_scale, zd):
    o = pv[:, :DA_VDIM] / pv[:, DA_VDIM:]
    o = o[:tq] - lam * o[tq:]
    o = o * lax.rsqrt(jnp.mean(o * o, axis=-1, keepdims=True) + EPS) * sg * lam_scale
    return o * (zd * jax.nn.sigmoid(zd))


def _diff_attn_ctx_kernel(q_ref, kc_ref, vc_ref, zd_ref, sg_ref, lam_ref, o_ref, *, lam_scale):
    v1 = jnp.concatenate([vc_ref[...], jnp.ones_like(vc_ref)], axis=1)
    tq = q_ref.shape[0]
    s = lax.dot_general(_stack_maps(q_ref[...]), kc_ref[...], (((1,), (1,)), ((), ())),
                        preferred_element_type=F32)
    p = jnp.exp2(s - jnp.max(s, axis=-1, keepdims=True))
    pv = jnp.dot(p.astype(BF16), v1, preferred_element_type=F32)
    o_ref[...] = _diff_combine(pv, tq, lam_ref[...], sg_ref[...], lam_scale,
                               zd_ref[...].astype(F32)).astype(o_ref.dtype)


def _diff_attn_lat_kernel(q_ref, kl_ref, vl_ref, kc_ref, vc_ref, zd_ref, sg_ref, lam_ref, o_ref,
                          kn_ref, v1_ref, s0_ref, s1_ref, m0_ref, m1_ref, *, tq, lam_scale):
    nctx = kc_ref.shape[0]
    nq = q_ref.shape[0] // tq

    kn_ref[:, 0:nctx] = jnp.transpose(kc_ref[...].astype(F32)).astype(BF16)
    kn_ref[:, nctx:] = jnp.transpose(kl_ref[...].astype(F32)).astype(BF16)
    v1_ref[0:nctx, :] = jnp.concatenate([vc_ref[...], jnp.ones_like(vc_ref)], axis=1)
    v1_ref[nctx:, :] = jnp.concatenate([vl_ref[...], jnp.ones_like(vl_ref)], axis=1)

    bufs = ((s0_ref, m0_ref), (s1_ref, m1_ref))

    def scores(i, slot):
        s_ref, m_ref = bufs[slot]
        rows = pl.ds(pl.multiple_of(i * tq, tq), tq)
        s = jnp.dot(_stack_maps(q_ref[rows, :]), kn_ref[...], preferred_element_type=F32)
        s_ref[...] = s
        m_ref[...] = jnp.broadcast_to(jnp.max(s, axis=-1, keepdims=True), m_ref.shape)

    def finish(i, slot):
        s_ref, m_ref = bufs[slot]
        rows = pl.ds(pl.multiple_of(i * tq, tq), tq)
        p = jnp.exp2(s_ref[...] - m_ref[:, 0:1])
        pv = jnp.dot(p.astype(BF16), v1_ref[...], preferred_element_type=F32)
        o = _diff_combine(pv, tq, lam_ref[...], sg_ref[...], lam_scale, zd_ref[rows, :].astype(F32))
        o_ref[rows, :] = o.astype(o_ref.dtype)

    assert nq % 2 == 0
    scores(0, 0)

    def body(j, carry):
        scores(2 * j + 1, 1)
        finish(2 * j, 0)
        scores(2 * j + 2, 0)
        finish(2 * j + 1, 1)
        return carry

    lax.fori_loop(0, nq // 2 - 1, body, 0)
    scores(nq - 1, 1)
    finish(nq - 2, 0)
    finish(nq - 1, 1)


def diff_attention(hin_lat, hin_ctx, col_q, col_k, col_v, col_zd, batch, qg, kg, sg, lam_row, lam_init,
                   cos_t, sin_t, ones_bd):
    ml, mc = hin_lat.shape[0], hin_ctx.shape[0]
    L, nctx = ml // batch, mc // batch
    tq = min(256, L)
    qk_lat = qk_prep(hin_lat, col_q * LANES // BRANCH, col_k * LANES // BRANCH, qg, kg, ones_bd,
                     (cos_t, sin_t), L)
    qk_ctx = qk_prep(hin_ctx, col_q * LANES // BRANCH, col_k * LANES // BRANCH, qg, kg, ones_bd, None, nctx)
    kcol = BRANCH // LANES
    small = lambda b, h: (0, 0)
    common = [pl.BlockSpec((1, LANES), small), pl.BlockSpec((1, LANES), small)]
    cargs = [sg, lam_row]
    lam_scale = 1.0 - lam_init
    nk = nctx + L
    y_lat = pl.pallas_call(
        functools.partial(_diff_attn_lat_kernel, tq=tq, lam_scale=lam_scale),
        grid=(batch, DA_HEADS),
        in_specs=[
            pl.BlockSpec((L, LANES), lambda b, h: (b, h)),
            pl.BlockSpec((L, LANES), lambda b, h: (b, kcol + h)),
            pl.BlockSpec((L, LANES), lambda b, h: (b, col_v + h)),
            pl.BlockSpec((nctx, LANES), lambda b, h: (b, kcol + h)),
            pl.BlockSpec((nctx, LANES), lambda b, h: (b, col_v + h)),
            pl.BlockSpec((L, LANES), lambda b, h: (b, col_zd + h)),
        ] + common,
        out_specs=pl.BlockSpec((L, LANES), lambda b, h: (b, h)),
        out_shape=jax.ShapeDtypeStruct((ml, BRANCH), BF16),
        scratch_shapes=[
            pltpu.VMEM((LANES, nk), BF16),
            pltpu.VMEM((nk, 2 * DA_VDIM), BF16),
            pltpu.VMEM((2 * tq, nk), F32), pltpu.VMEM((2 * tq, nk), F32),
            pltpu.VMEM((2 * tq, LANES), F32), pltpu.VMEM((2 * tq, LANES), F32),
        ],
        compiler_params=_cparams(("arbitrary", "arbitrary")),
        name="diff_attn_lat",
    )(qk_lat, qk_lat, hin_lat, qk_ctx, hin_ctx, hin_lat, *cargs)
    y_ctx = pl.pallas_call(
        functools.partial(_diff_attn_ctx_kernel, lam_scale=lam_scale),
        grid=(batch, DA_HEADS),
        in_specs=[
            pl.BlockSpec((nctx, LANES), lambda b, h: (b, h)),
            pl.BlockSpec((nctx, LANES), lambda b, h: (b, kcol + h)),
            pl.BlockSpec((nctx, LANES), lambda b, h: (b, col_v + h)),
            pl.BlockSpec((nctx, LANES), lambda b, h: (b, col_zd + h)),
        ] + common,
        out_specs=pl.BlockSpec((nctx, LANES), lambda b, h: (b, h)),
        out_shape=jax.ShapeDtypeStruct((mc, BRANCH), BF16),
        compiler_params=_cparams(("arbitrary", "arbitrary")),
        name="diff_attn_ctx",
    )(qk_ctx, qk_ctx, hin_ctx, hin_ctx, *cargs)
    return y_lat, y_ctx


def seg_ones():
    seg = jnp.arange(MXU_N) // DA_HEAD
    return (seg[:, None] == seg[None, :]).astype(BF16)


def rope_tables(n_tokens):
    rows = n_tokens // GRID_W
    row = jnp.repeat(jnp.arange(rows, dtype=F32), GRID_W)
    col = jnp.tile(jnp.arange(GRID_W, dtype=F32), rows)
    n_freq = DA_HEAD // 4
    inv_freq = ROPE_BASE ** (-jnp.arange(n_freq, dtype=F32) / n_freq)
    ang_r = row[:, None] * inv_freq
    ang_c = col[:, None] * inv_freq
    ang = jnp.concatenate([ang_r, ang_r, ang_c, ang_c], axis=-1)
    sign = jnp.tile(jnp.concatenate([-jnp.ones(n_freq, F32), jnp.ones(n_freq, F32)]), 2)
    cos = jnp.tile(jnp.cos(ang), (1, 2))
    sin_signed = jnp.tile(jnp.sin(ang) * sign, (1, 2))
    return cos, sin_signed


def _mm_out_kernel(*refs, n_parts, with_next):
    a_refs, w_refs = refs[:n_parts], refs[n_parts:2 * n_parts]
    rest = refs[2 * n_parts:]
    if with_next:
        x_ref, g_ref, gs_ref, sh_ref, o_ref, hn_ref = rest
    else:
        x_ref, g_ref, o_ref = rest
    acc = None
    for a_ref, w_ref in zip(a_refs, w_refs):
        t = jnp.dot(a_ref[...], w_ref[...], preferred_element_type=F32)
        acc = t if acc is None else acc + t
    xn = x_ref[...] + g_ref[...] * acc
    o_ref[...] = xn
    if with_next:
        hn_ref[...] = _modulate(xn, gs_ref[...], sh_ref[...]).astype(hn_ref.dtype)


def mm_out(a_parts, w_stack, layer, x, gate, next_mod, rows_per_mod):
    m = x.shape[0]
    n_parts = len(a_parts)
    kp = w_stack.shape[1] // n_parts
    tm = min(512, rows_per_mod)
    tpm = rows_per_mod // tm
    mod_spec = pl.BlockSpec((None, 1, D_MODEL), lambda i: (i // tpm, 0, 0))
    in_specs = [pl.BlockSpec((tm, kp), lambda i: (i, 0)) for _ in a_parts]
    in_specs += [pl.BlockSpec((None, kp, D_MODEL), functools.partial(lambda i, p: (layer, p, 0), p=p))
                 for p in range(n_parts)]
    in_specs += [pl.BlockSpec((tm, D_MODEL), lambda i: (i, 0)), mod_spec]
    args = [*a_parts, *([w_stack] * n_parts), x, gate]
    out_specs = [pl.BlockSpec((tm, D_MODEL), lambda i: (i, 0))]
    out_shape = [jax.ShapeDtypeStruct((m, D_MODEL), F32)]
    if next_mod is not None:
        in_specs += [mod_spec, mod_spec]
        args += list(next_mod)
        out_specs.append(pl.BlockSpec((tm, D_MODEL), lambda i: (i, 0)))
        out_shape.append(jax.ShapeDtypeStruct((m, D_MODEL), BF16))
    outs = pl.pallas_call(
        functools.partial(_mm_out_kernel, n_parts=n_parts, with_next=next_mod is not None),
        grid=(m // tm,),
        in_specs=in_specs,
        out_specs=out_specs,
        out_shape=out_shape,
        compiler_params=_cparams(("arbitrary",)),
        name="mm_out",
    )(*args)
    return (outs[0], outs[1]) if next_mod is not None else (outs[0], None)


GLA_STEP = 4


def _seg_cumsum(x, reverse):
    n = x.shape[0]
    r = lax.broadcasted_iota(jnp.int32, (MXU_N, MXU_N), 0)
    c = lax.broadcasted_iota(jnp.int32, (MXU_N, MXU_N), 1)
    same = (r // GLA_CHUNK) == (c // GLA_CHUNK)
    tri = (same & ((c >= r) if reverse else (c <= r))).astype(BF16)
    hi = x.astype(BF16)
    lo = (x - hi.astype(F32)).astype(BF16)
    parts = []
    for t in range(0, n, MXU_N):
        parts.append(jnp.dot(tri, hi[t:t + MXU_N], preferred_element_type=F32)
                     + jnp.dot(tri, lo[t:t + MXU_N], preferred_element_type=F32))
    return jnp.concatenate(parts, axis=0)


def _gla_kernel(ql_ref, kl_ref, vl_ref, rl_ref, zl_ref, qc_ref, kc_ref, vc_ref, rc_ref, zc_ref,
                wa_ref, ba_ref, ng_ref, *rest, ctx_out):
    if ctx_out:
        ol_ref, oc_ref = rest[:2]
        rest = rest[2:]
    else:
        ol_ref, oc_ref = rest[0], None
        rest = rest[1:]
    sf_ref, sb_ref, kv_ref, sin_ref, qd_ref, kd_ref, b_ref, vb_ref, ofl_ref, ofc_ref, obl_ref, obc_ref = rest
    C = GLA_CHUNK
    L, nctx = ql_ref.shape[0], kc_ref.shape[0]
    nl, nc = L // C, nctx // C
    row = lax.broadcasted_iota(jnp.int32, (C, C), 0)
    col = lax.broadcasted_iota(jnp.int32, (C, C), 1)

    for base, n, q_ref, k_ref, v_ref, r_ref in ((0, nctx, qc_ref, kc_ref, vc_ref, rc_ref),
                                                (nctx, L, ql_ref, kl_ref, vl_ref, rl_ref)):
        rows = slice(base, base + n)
        vb_ref[rows, :] = v_ref[...].astype(BF16)
        r = r_ref[...].astype(BF16)
        k = k_ref[...].astype(F32)
        with_q = ctx_out or base > 0
        for d in range(2):
            logits = jnp.dot(r, wa_ref[d], preferred_element_type=F32) + ba_ref[d]
            ls = jnp.minimum(logits, 0.0) - jnp.log(1.0 + jnp.exp(-jnp.abs(logits)))
            b = _seg_cumsum(ls * (1.0 / GLA_TAU), reverse=d == 1)
            b_ref[d, rows, :] = b
            kd_ref[d, rows, :] = (k * jnp.exp(-b)).astype(BF16)
            if with_q:
                qd_ref[d, rows, :] = (q_ref[...].astype(F32) * (GLA_DK ** -0.5) * jnp.exp(b)).astype(BF16)

    sf_ref[...] = jnp.zeros_like(sf_ref)
    sb_ref[...] = jnp.zeros_like(sb_ref)

    def segment(base, n_chunks, of_ref, ob_ref):
        step = min(GLA_STEP, n_chunks)
        assert n_chunks % step == 0

        def body(i, carry):
            work = []
            for u in range(step):
                cf = i * step + u
                cb = n_chunks - 1 - cf
                work.append((0, u, pl.multiple_of(base + cf * C, C), pl.multiple_of(cf * C, C)))
                work.append((1, step + u, pl.multiple_of(base + cb * C, C), pl.multiple_of(cb * C, C)))
            outs = {0: of_ref, 1: ob_ref}
            with_out = of_ref is not None
            scores, decs = {}, {}
            if with_out:
                for d, slot, row0, _ in work:
                    rows = pl.ds(row0, C)
                    scores[slot] = lax.dot_general(qd_ref[d, rows, :], kd_ref[d, rows, :],
                                                   (((1,), (1,)), ((), ())), preferred_element_type=F32)
            for d, slot, row0, _ in work:
                rows = pl.ds(row0, C)
                dec = jnp.exp(b_ref[d, pl.ds(row0 if d == 1 else row0 + C - 1, 1), :])
                k_end = (kd_ref[d, rows, :].astype(F32) * dec).astype(BF16)
                kv_ref[slot] = lax.dot_general(k_end, vb_ref[rows, :], (((0,), (0,)), ((), ())),
                                               preferred_element_type=F32)
                decs[slot] = jnp.transpose(jnp.broadcast_to(dec, (LANES, GLA_DK)))[:, 0:1]
            if with_out:
                for d, slot, row0, orow0 in work:
                    incl = (col >= row) if d == 1 else (col <= row)
                    sc = jnp.where(incl, scores[slot], 0.0).astype(BF16)
                    outs[d][pl.ds(orow0, C), :] = jnp.dot(sc, vb_ref[pl.ds(row0, C), :],
                                                          preferred_element_type=F32)
            for d, slot, _, _ in work:
                s_ref = sb_ref if d == 1 else sf_ref
                s_old = s_ref[...]
                sin_ref[slot] = s_old.astype(BF16)
                s_ref[...] = decs[slot] * s_old + kv_ref[slot]
            if with_out:
                for d, slot, row0, orow0 in work:
                    outs[d][pl.ds(orow0, C), :] += jnp.dot(qd_ref[d, pl.ds(row0, C), :], sin_ref[slot],
                                                           preferred_element_type=F32)
            return carry

        lax.fori_loop(0, n_chunks // step, body, 0)

    segment(0, nc, ofc_ref if ctx_out else None, obc_ref if ctx_out else None)
    segment(nctx, nl, ofl_ref, obl_ref)
    def finish(of_ref, ob_ref, z_ref, o_ref):
        o = of_ref[...] + ob_ref[...]
        z = z_ref[...].astype(F32)
        y = o * lax.rsqrt(jnp.mean(o * o, axis=-1, keepdims=True) + EPS) * ng_ref[...]
        o_ref[...] = (y * (z * jax.nn.sigmoid(z))).astype(o_ref.dtype)

    finish(ofl_ref, obl_ref, zl_ref, ol_ref)
    if ctx_out:
        finish(ofc_ref, obc_ref, zc_ref, oc_ref)


def gla_mix(hin_lat, hin_ctx, r_lat, r_ctx, wa2p, ba, norm_g, batch, ctx_out):
    ml, mc = hin_lat.shape[0], hin_ctx.shape[0]
    L, nctx = ml // batch, mc // batch
    kb = GLA_KEY // GLA_DK
    vb = 2 * GLA_KEY // GLA_DV
    zb = (2 * GLA_KEY + GLA_VAL) // GLA_DV

    def seg_specs(n):
        return [pl.BlockSpec((n, GLA_DK), lambda b, h: (b, h)),
                pl.BlockSpec((n, GLA_DK), lambda b, h: (b, kb + h)),
                pl.BlockSpec((n, GLA_DV), lambda b, h: (b, vb + h)),
                pl.BlockSpec((n, LANES), lambda b, h: (b, 0)),
                pl.BlockSpec((n, GLA_DV), lambda b, h: (b, zb + h))]

    in_specs = seg_specs(L) + seg_specs(nctx) + [
        pl.BlockSpec((2, LANES, GLA_DK), lambda b, h: (0, 0, h)),
        pl.BlockSpec((2, 1, GLA_DK), lambda b, h: (0, 0, h)),
        pl.BlockSpec((1, GLA_DV), lambda b, h: (0, 0)),
    ]
    out_specs = [pl.BlockSpec((L, GLA_DV), lambda b, h: (b, h)),
                 pl.BlockSpec((nctx, GLA_DV), lambda b, h: (b, h))]
    out_shape = [jax.ShapeDtypeStruct((ml, GLA_VAL), BF16), jax.ShapeDtypeStruct((mc, GLA_VAL), BF16)]
    if not ctx_out:
        out_specs, out_shape = out_specs[:1], out_shape[:1]
    outs = pl.pallas_call(
        functools.partial(_gla_kernel, ctx_out=ctx_out),
        grid=(batch, GLA_HEADS),
        in_specs=in_specs,
        out_specs=out_specs,
        out_shape=out_shape,
        scratch_shapes=[
            pltpu.VMEM((GLA_DK, GLA_DV), F32), pltpu.VMEM((GLA_DK, GLA_DV), F32),
            pltpu.VMEM((2 * GLA_STEP, GLA_DK, GLA_DV), F32),
            pltpu.VMEM((2 * GLA_STEP, GLA_DK, GLA_DV), BF16),
            pltpu.VMEM((2, nctx + L, GLA_DK), BF16),
            pltpu.VMEM((2, nctx + L, GLA_DK), BF16),
            pltpu.VMEM((2, nctx + L, GLA_DK), F32),
            pltpu.VMEM((nctx + L, GLA_DV), BF16),
            pltpu.VMEM((L, GLA_DV), F32), pltpu.VMEM((nctx, GLA_DV), F32),
            pltpu.VMEM((L, GLA_DV), F32), pltpu.VMEM((nctx, GLA_DV), F32),
        ],
        compiler_params=_cparams(("arbitrary", "arbitrary")),
        name="gla_mix",
    )(hin_lat, hin_lat, hin_lat, r_lat, hin_lat, hin_ctx, hin_ctx, hin_ctx, r_ctx, hin_ctx, wa2p, ba, norm_g)
    return (outs[0], outs[1]) if ctx_out else (outs[0], None)


def _mods(mod_l, norm_g, batch):
    shift, scale, gate = mod_l[:, :D_MODEL], mod_l[:, D_MODEL:2 * D_MODEL], mod_l[:, 2 * D_MODEL:]
    gs = (norm_g.astype(F32)[None, :] * (1.0 + scale))[:, None, :]
    sh = shift[:, None, :]
    gt = gate[:, None, :]
    lat = (gs[:batch], sh[:batch], gt[:batch])
    ctx = (gs[batch:batch + 1], sh[batch:batch + 1], gt[batch:batch + 1])
    return lat, ctx


def even_layer(x_lat, x_ctx, hn_l, hn_c, gates, next_mods, w_in, w_out, j, s5p, w_glu, qn_g, kn_g,
               lam_vecs, subln_g, lam_init, batch, rope):
    L = x_lat.shape[0] // batch
    gt_l, gt_c = gates
    rest = 5 * BRANCH
    u3_l = mm_in(hn_l, w_in, j, 0, BRANCH, out_dtype=BF16, gb_out=True, name="mm_in_u")
    u3_c = mm_in(hn_c, w_in, j, 0, BRANCH, out_dtype=BF16, gb_out=True, name="mm_in_u_ctx")
    hin_l = mm_in(hn_l, w_in, j, BRANCH, rest, out_dtype=BF16, name="mm_in_rest")
    hin_c = mm_in(hn_c, w_in, j, BRANCH, rest, out_dtype=BF16, name="mm_in_rest_ctx")

    ez, kl, cl, ptab, sel = s5p
    h_l, h_c = s5_states(u3_l, u3_c, ez, ptab, j, batch)
    y_l, y_c = s5_outputs(u3_l, u3_c, h_l, h_c, kl, cl, sel, j)
    wg = w_glu.astype(BF16)
    a_s5_l = s5_finish(y_l, wg, hin_l, 0)
    a_s5_c = s5_finish(y_c, wg, hin_c, 0)

    lv = lam_vecs.astype(F32)
    lam = jnp.exp(jnp.sum(lv[0] * lv[1])) - jnp.exp(jnp.sum(lv[2] * lv[3])) + lam_init
    lam_row = jnp.full((1, LANES), lam, F32)
    qg = jnp.tile(qn_g.astype(F32), 2)[None, :]
    kg = jnp.tile(kn_g.astype(F32), 2)[None, :]
    sg = subln_g.astype(F32)[None, :]
    cb = BRANCH // LANES
    cos_t, sin_t, ones_bd = rope
    a_da_l, a_da_c = diff_attention(hin_l, hin_c, cb, 2 * cb, 3 * cb, 4 * cb, batch, qg, kg, sg, lam_row,
                                    lam_init, cos_t, sin_t, ones_bd)

    nm_l, nm_c = next_mods
    x_lat, hn_l = mm_out([a_s5_l, a_da_l], w_out, j, x_lat, gt_l, nm_l, L)
    x_ctx, hn_c = mm_out([a_s5_c, a_da_c], w_out, j, x_ctx, gt_c, nm_c, x_ctx.shape[0])
    return x_lat, x_ctx, hn_l, hn_c


def odd_layer(x_lat, x_ctx, hn_l, hn_c, gates, next_mods, w_in, w_out, j, wa1, wa2, ba, gla_norm_g, batch,
              with_ctx_out):
    L = x_lat.shape[0] // batch
    gt_l, gt_c = gates
    n = 3 * D_MODEL
    w_aux = jnp.zeros((D_MODEL, LANES), F32).at[:, :GLA_RANK].set(wa1[0]).at[:, GLA_RANK:2 * GLA_RANK].set(wa1[1])
    w_aux = w_aux.astype(BF16)
    hin_l, r_l = mm_in(hn_l, w_in, j, 0, n, out_dtype=BF16, w_aux=w_aux, name="mm_in_odd")
    hin_c, r_c = mm_in(hn_c, w_in, j, 0, n, out_dtype=BF16, w_aux=w_aux, name="mm_in_odd_ctx")
    wa2p = jnp.zeros((2, LANES, GLA_KEY), F32)
    wa2p = wa2p.at[0, :GLA_RANK].set(wa2[0]).at[1, GLA_RANK:2 * GLA_RANK].set(wa2[1]).astype(BF16)
    ng = gla_norm_g.astype(F32)[None, :]
    a_l, a_c = gla_mix(hin_l, hin_c, r_l, r_c, wa2p, ba.astype(F32).reshape(2, 1, GLA_KEY), ng, batch,
                       with_ctx_out)
    nm_l, nm_c = next_mods
    x_lat, hn_l = mm_out([a_l], w_out, j, x_lat, gt_l, nm_l, L)
    if with_ctx_out:
        x_ctx, hn_c = mm_out([a_c], w_out, j, x_ctx, gt_c, nm_c, x_ctx.shape[0])
    return x_lat, x_ctx, hn_l, hn_c


def kernel(x, c, ctx, c_ctx, ada_w, ada_b, norm_g, ev_w_in, ev_w_out, s5_a_re, s5_a_im, s5_log_dt, s5_b_re, s5_b_im, s5_c_re, s5_c_im, s5_d, s5_w_glu, da_qn_g, da_kn_g, da_lam, da_subln_g, od_w_in, od_w_out, gla_wa1, gla_wa2, gla_ba, gla_norm_g):
    batch, L, _ = x.shape
    nctx = ctx.shape[1]
    x_lat = x.reshape(batch * L, D_MODEL)
    x_ctx = ctx.reshape(batch * nctx, D_MODEL)
    cond = jnp.zeros((8, D_MODEL), F32).at[:batch].set(c).at[batch].set(c_ctx)
    mod = modulation_all(cond, ada_w, ada_b)

    cos_t, sin_t = rope_tables(L)
    rope = (cos_t, sin_t, seg_ones())
    mods = [_mods(mod[i], norm_g[i], batch) for i in range(DEPTH)]
    ev_w_out_b, od_w_out_b = ev_w_out.astype(BF16), od_w_out.astype(BF16)

    s5p = jax.vmap(functools.partial(s5_prepare, ncl=L // S5_T))(
        s5_a_re, s5_a_im, s5_log_dt, s5_b_re, s5_b_im, s5_c_re, s5_c_im, s5_d) + (s5_select_matrix(),)

    (gs_l, sh_l, _), (gs_c, sh_c, _) = mods[0]
    hn_l = prenorm(x_lat, gs_l, sh_l, L)
    hn_c = prenorm(x_ctx, gs_c, sh_c, x_ctx.shape[0])
    for i in range(DEPTH):
        j = i // 2
        with_ctx_out = i < DEPTH - 1
        gates = (mods[i][0][2], mods[i][1][2])
        next_mods = (mods[i + 1][0][:2], mods[i + 1][1][:2]) if with_ctx_out else (None, None)
        if i % 2 == 0:
            lam_init = 0.8 - 0.6 * math.exp(-0.3 * i)
            x_lat, x_ctx, hn_l, hn_c = even_layer(
                x_lat, x_ctx, hn_l, hn_c, gates, next_mods, ev_w_in, ev_w_out_b, j, s5p, s5_w_glu[j],
                da_qn_g[j], da_kn_g[j], da_lam[j], da_subln_g[j], lam_init, batch, rope)
        else:
            x_lat, x_ctx, hn_l, hn_c = odd_layer(
                x_lat, x_ctx, hn_l, hn_c, gates, next_mods, od_w_in, od_w_out_b, j, gla_wa1[j], gla_wa2[j],
                gla_ba[j], gla_norm_g[j], batch, with_ctx_out)
    return x_lat.reshape(batch, L, D_MODEL)
```

```python
import functools
import math

import jax
import jax.numpy as jnp
from jax import lax
from jax.experimental import pallas as pl
from jax.experimental.pallas import tpu as pltpu

F32 = jnp.float32
BF16 = jnp.bfloat16

D_MODEL = 2048
DEPTH = 4
GRID_W = 64
EPS = 1e-6
BRANCH = D_MODEL // 2
S5_GROUP = 16
S5_GROUPS = BRANCH // S5_GROUP
S5_STATE = 64
DA_HEAD = 64
DA_HEADS = BRANCH // (2 * DA_HEAD)
DA_VDIM = 2 * DA_HEAD
ROPE_BASE = 10000.0
GLA_HEADS = 4
GLA_KEY = D_MODEL // 2
GLA_VAL = D_MODEL
GLA_DK = GLA_KEY // GLA_HEADS
GLA_DV = GLA_VAL // GLA_HEADS
GLA_RANK = 16
GLA_TAU = 16.0
GLA_CHUNK = 64

LANES = 128
VMEM_LIMIT = 56 * 1024 * 1024

S5_T = 16
S5_GB = BRANCH // LANES
S5_GPB = LANES // S5_GROUP
S5_W = S5_T * LANES
S5_SW = S5_GPB * S5_STATE


def _cparams(sem):
    return pltpu.CompilerParams(dimension_semantics=sem, vmem_limit_bytes=VMEM_LIMIT)


def _mod_kernel(c_ref, w_ref, b_ref, o_ref):
    c = c_ref[...]
    s = (c * jax.nn.sigmoid(c)).astype(BF16)
    acc = jnp.dot(s, w_ref[...].astype(BF16), preferred_element_type=F32)
    o_ref[...] = acc + b_ref[...]


def modulation_all(cond, ada_w, ada_b):
    tn = 512
    n = 3 * D_MODEL
    return pl.pallas_call(
        _mod_kernel,
        grid=(DEPTH, n // tn),
        in_specs=[
            pl.BlockSpec((8, D_MODEL), lambda l, j: (0, 0)),
            pl.BlockSpec((None, D_MODEL, tn), lambda l, j: (l, 0, j)),
            pl.BlockSpec((None, 1, tn), lambda l, j: (l, 0, j)),
        ],
        out_specs=pl.BlockSpec((None, 8, tn), lambda l, j: (l, 0, j)),
        out_shape=jax.ShapeDtypeStruct((DEPTH, 8, n), F32),
        compiler_params=_cparams(("arbitrary", "arbitrary")),
        name="modulation",
    )(cond, ada_w, ada_b.reshape(DEPTH, 1, n))


def _modulate(x, gs, sh):
    ms = jnp.mean(x * x, axis=-1, keepdims=True)
    return x * lax.rsqrt(ms + EPS) * gs + sh


def _prenorm_kernel(x_ref, gs_ref, sh_ref, o_ref):
    o_ref[...] = _modulate(x_ref[...], gs_ref[...], sh_ref[...]).astype(o_ref.dtype)


def prenorm(x, gs, sh, rows_per_mod):
    m = x.shape[0]
    tm = min(512, rows_per_mod)
    tpm = rows_per_mod // tm
    return pl.pallas_call(
        _prenorm_kernel,
        grid=(m // tm,),
        in_specs=[
            pl.BlockSpec((tm, D_MODEL), lambda i: (i, 0)),
            pl.BlockSpec((None, 1, D_MODEL), lambda i: (i // tpm, 0, 0)),
            pl.BlockSpec((None, 1, D_MODEL), lambda i: (i // tpm, 0, 0)),
        ],
        out_specs=pl.BlockSpec((tm, D_MODEL), lambda i: (i, 0)),
        out_shape=jax.ShapeDtypeStruct((m, D_MODEL), BF16),
        compiler_params=_cparams(("arbitrary",)),
        name="prenorm",
    )(x, gs, sh)


def _mm_in_kernel(a_ref, w_ref, *rest, gb_out, has_aux):
    if has_aux:
        wa_ref, o_ref, aux_ref = rest

        @pl.when(pl.program_id(1) == 0)
        def _():
            aux_ref[...] = jnp.dot(a_ref[...], wa_ref[...], preferred_element_type=F32)
    elif gb_out:
        o_ref, stage_ref = rest
    else:
        (o_ref,) = rest

    acc = jnp.dot(a_ref[...], w_ref[...].astype(BF16), preferred_element_type=F32)
    if gb_out:
        n_rows = acc.shape[0] // S5_T
        for q in range(acc.shape[1] // LANES):
            stage_ref[q] = acc[:, q * LANES:(q + 1) * LANES]
        for q in range(acc.shape[1] // LANES):
            for t in range(S5_T):
                o_ref[q, :, t * LANES:(t + 1) * LANES] = stage_ref[q, pl.ds(t, n_rows, stride=S5_T), :].astype(
                    o_ref.dtype)
    else:
        o_ref[...] = acc.astype(o_ref.dtype)


def mm_in(a, w_stack, layer, col0, ncols, *, out_dtype, gb_out=False, w_aux=None, name="mm_in"):
    m = a.shape[0]
    tm = min(2048, m)
    tn = 512
    assert m % tm == 0 and ncols % tn == 0 and col0 % tn == 0
    jb = col0 // tn
    in_specs = [
        pl.BlockSpec((tm, D_MODEL), lambda i, j: (i, 0)),
        pl.BlockSpec((None, D_MODEL, tn), lambda i, j: (layer, 0, j + jb)),
    ]
    args = [a, w_stack]
    scratch = []
    if gb_out:
        out_shape = [jax.ShapeDtypeStruct((ncols // LANES, m // S5_T, S5_W), out_dtype)]
        out_specs = [pl.BlockSpec((tn // LANES, tm // S5_T, S5_W), lambda i, j: (j, i, 0))]
        scratch = [pltpu.VMEM((tn // LANES, tm, LANES), F32)]
    else:
        out_shape = [jax.ShapeDtypeStruct((m, ncols), out_dtype)]
        out_specs = [pl.BlockSpec((tm, tn), lambda i, j: (i, j))]
    if w_aux is not None:
        in_specs.append(pl.BlockSpec((D_MODEL, LANES), lambda i, j: (0, 0)))
        args.append(w_aux)
        out_shape.append(jax.ShapeDtypeStruct((m, LANES), F32))
        out_specs.append(pl.BlockSpec((tm, LANES), lambda i, j: (i, 0)))
    outs = pl.pallas_call(
        functools.partial(_mm_in_kernel, gb_out=gb_out, has_aux=w_aux is not None),
        grid=(m // tm, ncols // tn),
        in_specs=in_specs,
        out_specs=out_specs,
        out_shape=out_shape,
        scratch_shapes=scratch,
        compiler_params=_cparams(("arbitrary", "arbitrary")),
        name=name,
    )(*args)
    return outs if w_aux is not None else outs[0]


def _shift_rows(h, s, up):
    n = h.shape[0]
    row = lax.broadcasted_iota(jnp.int32, h.shape, 0)
    if up:
        return jnp.where(row >= n - s, 0.0, pltpu.roll(h, n - s, 0))
    return jnp.where(row < s, 0.0, pltpu.roll(h, s, 0))


def _chunk_scan(zr, zi, pr, pi, reverse):
    n = zr.shape[0]
    ntab = pr.shape[0]
    hr, hi = zr, zi
    s = 1
    while s < n:
        idx = ntab - 1 - s if reverse else s
        ar, ai = pr[idx:idx + 1], pi[idx:idx + 1]
        sr, si = _shift_rows(hr, s, reverse), _shift_rows(hi, s, reverse)
        hr, hi = hr + ar * sr - ai * si, hi + ar * si + ai * sr
        s *= 2
    return hr, hi


def _group_mask(shape, row_shift, col_shift):
    rg = lax.broadcasted_iota(jnp.int32, shape, 0) >> row_shift
    cg = (lax.broadcasted_iota(jnp.int32, shape, 1) >> col_shift) & (S5_GPB - 1)
    return rg == cg


def _s5_state_kernel(xl_ref, xc_ref, ez_ref, p_ref, hl_ref, hc_ref, wz_ref):
    ncc = xc_ref.shape[0]

    @pl.when(pl.program_id(1) == 0)
    def _():
        mask = _group_mask((LANES, S5_SW), 4, 6)
        for t in range(S5_T):
            for q in range(4):
                e = jnp.concatenate([ez_ref[t, q]] * S5_GPB, axis=0)
                wz_ref[t * LANES:(t + 1) * LANES, q * S5_SW:(q + 1) * S5_SW] = jnp.where(
                    mask, e, jnp.zeros_like(e))

    x = jnp.concatenate([xc_ref[...], xl_ref[...]], axis=0)
    z = jnp.dot(x, wz_ref[...], preferred_element_type=F32)
    p = p_ref[...]
    outs_c, outs_l = [], []
    for d in range(2):
        rev = d == 1
        c0 = 2 * d * S5_SW
        zr, zi = z[:, c0:c0 + S5_SW], z[:, c0 + S5_SW:c0 + 2 * S5_SW]
        pr, pi = p[:, c0:c0 + S5_SW], p[:, c0 + S5_SW:c0 + 2 * S5_SW]
        cr, ci = _chunk_scan(zr[:ncc], zi[:ncc], pr, pi, rev)
        lr, li = _chunk_scan(zr[ncc:], zi[ncc:], pr, pi, rev)
        if rev:
            car_r, car_i = cr[0:1], ci[0:1]
        else:
            car_r, car_i = cr[ncc - 1:ncc], ci[ncc - 1:ncc]
        hcr, hci = _shift_rows(cr, 1, rev), _shift_rows(ci, 1, rev)
        hlr = _shift_rows(lr, 1, rev) + pr * car_r - pi * car_i
        hli = _shift_rows(li, 1, rev) + pr * car_i + pi * car_r
        outs_c += [hcr, hci]
        outs_l += [hlr, hli]
    hc_ref[...] = jnp.concatenate(outs_c, axis=1).astype(hc_ref.dtype)
    hl_ref[...] = jnp.concatenate(outs_l, axis=1).astype(hl_ref.dtype)


def s5_states(u3_lat, u3_ctx, ez, ptab, layer, batch):
    ncl = u3_lat.shape[1] // batch
    ncc = u3_ctx.shape[1] // batch
    assert ptab.shape[2] == ncl
    return pl.pallas_call(
        _s5_state_kernel,
        grid=(S5_GB, batch),
        in_specs=[
            pl.BlockSpec((None, ncl, S5_W), lambda g, b: (g, b, 0)),
            pl.BlockSpec((None, ncc, S5_W), lambda g, b: (g, b, 0)),
            pl.BlockSpec((None, None, S5_T, 4, S5_GROUP, S5_SW), lambda g, b: (layer, g, 0, 0, 0, 0)),
            pl.BlockSpec((None, None, ncl, 4 * S5_SW), lambda g, b: (layer, g, 0, 0)),
        ],
        out_specs=[
            pl.BlockSpec((None, ncl, 4 * S5_SW), lambda g, b: (g, b, 0)),
            pl.BlockSpec((None, ncc, 4 * S5_SW), lambda g, b: (g, b, 0)),
        ],
        out_shape=[
            jax.ShapeDtypeStruct((S5_GB, u3_lat.shape[1], 4 * S5_SW), BF16),
            jax.ShapeDtypeStruct((S5_GB, u3_ctx.shape[1], 4 * S5_SW), BF16),
        ],
        scratch_shapes=[pltpu.VMEM((S5_W, 4 * S5_SW), BF16)],
        compiler_params=_cparams(("arbitrary", "arbitrary")),
        name="s5_states",
    )(u3_lat, u3_ctx, ez, ptab)


S5_TAPS = 2 * S5_T
S5_SEL_IN = LANES
S5_SEL_OUT = (LANES // S5_GROUP) * LANES


def _s5_out_kernel(xl_ref, xc_ref, hl_ref, hc_ref, kl_ref, cl_ref, sel_ref, yl_ref, yc_ref, wt_ref, wm_ref):
    nl = xl_ref.shape[0]
    sel = sel_ref[...]
    taps = []
    mask_t = _group_mask((LANES, S5_SEL_OUT), 4, 4)
    for a in range(S5_TAPS * S5_GROUP // S5_SEL_IN):
        t = jnp.dot(kl_ref[:, a * S5_SEL_IN:(a + 1) * S5_SEL_IN], sel, preferred_element_type=F32)
        taps.append(jnp.where(mask_t, t, 0.0).astype(BF16))
    taps = jnp.concatenate(taps, axis=1)
    for s in range(S5_T):
        lo = (S5_T - 1 - s) * LANES
        wt_ref[s * LANES:(s + 1) * LANES, :] = taps[:, lo:lo + S5_W]
    mask_m = _group_mask((S5_SW, S5_SEL_OUT), 6, 4)
    for q in range(4):
        for a in range(S5_T * S5_GROUP // S5_SEL_IN):
            m = jnp.dot(cl_ref[q, :, a * S5_SEL_IN:(a + 1) * S5_SEL_IN], sel, preferred_element_type=F32)
            wm_ref[q * S5_SW:(q + 1) * S5_SW, a * S5_SEL_OUT:(a + 1) * S5_SEL_OUT] = jnp.where(
                mask_m, m, 0.0).astype(BF16)
    x = jnp.concatenate([xl_ref[...], xc_ref[...]], axis=0)
    h = jnp.concatenate([hl_ref[...], hc_ref[...]], axis=0)
    y = jnp.dot(x, wt_ref[...], preferred_element_type=F32)
    y = y + jnp.dot(h, wm_ref[...], preferred_element_type=F32)
    yl_ref[...] = y[:nl].astype(yl_ref.dtype)
    yc_ref[...] = y[nl:].astype(yc_ref.dtype)


def s5_outputs(u3_lat, u3_ctx, h_lat, h_ctx, kl, cl, sel, layer):
    nl, nc = u3_lat.shape[1], u3_ctx.shape[1]
    return pl.pallas_call(
        _s5_out_kernel,
        grid=(S5_GB,),
        in_specs=[
            pl.BlockSpec((None, nl, S5_W), lambda g: (g, 0, 0)),
            pl.BlockSpec((None, nc, S5_W), lambda g: (g, 0, 0)),
            pl.BlockSpec((None, nl, 4 * S5_SW), lambda g: (g, 0, 0)),
            pl.BlockSpec((None, nc, 4 * S5_SW), lambda g: (g, 0, 0)),
            pl.BlockSpec((None, None, LANES, S5_TAPS * S5_GROUP), lambda g: (layer, g, 0, 0)),
            pl.BlockSpec((None, 4, None, S5_SW, S5_T * S5_GROUP), lambda g: (layer, 0, g, 0, 0)),
            pl.BlockSpec((S5_SEL_IN, S5_SEL_OUT), lambda g: (0, 0)),
        ],
        out_specs=[
            pl.BlockSpec((None, nl, S5_W), lambda g: (g, 0, 0)),
            pl.BlockSpec((None, nc, S5_W), lambda g: (g, 0, 0)),
        ],
        out_shape=[
            jax.ShapeDtypeStruct((S5_GB, nl, S5_W), BF16),
            jax.ShapeDtypeStruct((S5_GB, nc, S5_W), BF16),
        ],
        scratch_shapes=[pltpu.VMEM((S5_W, S5_W), BF16), pltpu.VMEM((4 * S5_SW, S5_W), BF16)],
        compiler_params=_cparams(("arbitrary",)),
        name="s5_outputs",
    )(u3_lat, u3_ctx, h_lat, h_ctx, kl, cl, sel)


def _s5_finish_kernel(y_ref, wg_ref, zs_ref, o_ref, stage_ref):
    n_rows = y_ref.shape[1]
    for q in range(S5_GB):
        for t in range(S5_T):
            stage_ref[q, pl.ds(t, n_rows, stride=S5_T), :] = y_ref[q, :, t * LANES:(t + 1) * LANES].astype(F32)
    y = jnp.concatenate([stage_ref[q] for q in range(S5_GB)], axis=1)
    g = jax.nn.gelu(y)
    t = jnp.dot(g.astype(BF16), wg_ref[...], preferred_element_type=F32)
    zs = zs_ref[...].astype(F32)
    o_ref[...] = (g * jax.nn.sigmoid(t) * (zs * jax.nn.sigmoid(zs))).astype(o_ref.dtype)


def s5_finish(y3, w_glu, hin, zs_block):
    m = y3.shape[1] * S5_T
    tm = min(512, m)
    return pl.pallas_call(
        _s5_finish_kernel,
        grid=(m // tm,),
        in_specs=[
            pl.BlockSpec((S5_GB, tm // S5_T, S5_W), lambda i: (0, i, 0)),
            pl.BlockSpec((BRANCH, BRANCH), lambda i: (0, 0)),
            pl.BlockSpec((tm, BRANCH), lambda i: (i, zs_block)),
        ],
        out_specs=pl.BlockSpec((tm, BRANCH), lambda i: (i, 0)),
        out_shape=jax.ShapeDtypeStruct((m, BRANCH), BF16),
        scratch_shapes=[pltpu.VMEM((S5_GB, tm, LANES), F32)],
        compiler_params=_cparams(("arbitrary",)),
        name="s5_finish",
    )(y3, w_glu, hin)


def s5_prepare(a_re, a_im, log_dt, b_re, b_im, c_re, c_im, d_skip, ncl):
    T, G, P, C = S5_T, S5_GROUPS, S5_STATE, S5_GROUP
    GB, GPB, SW = S5_GB, S5_GPB, S5_SW
    hp = lax.Precision.HIGHEST
    a_re, a_im = a_re.astype(F32), a_im.astype(F32)
    dt = jnp.exp(log_dt.astype(F32))[..., None]
    la_re, la_im = a_re * dt, a_im * dt
    mag = jnp.exp(la_re)
    lb_re, lb_im = mag * jnp.cos(la_im), mag * jnp.sin(la_im)
    nr, ni = lb_re - 1.0, lb_im
    den = a_re * a_re + a_im * a_im
    f_re = (nr * a_re + ni * a_im) / den
    f_im = (ni * a_re - nr * a_im) / den
    bb_re = f_re[..., None] * b_re - f_im[..., None] * b_im
    bb_im = f_re[..., None] * b_im + f_im[..., None] * b_re

    def cpow(k, lr, li):
        m = jnp.exp(k * lr)
        return m * jnp.cos(k * li), m * jnp.sin(k * li)

    lad_re, lad_im = la_re.reshape(2, GB, 1, SW), la_im.reshape(2, GB, 1, SW)
    tt = jnp.arange(T, dtype=F32).reshape(1, 1, T, 1)
    pw_re, pw_im = cpow(tt, lad_re, lad_im)
    to_lanes = lambda w: w.reshape(2, GB, GPB, P, C).transpose(0, 1, 4, 2, 3).reshape(2, GB, C, SW)
    bt_re, bt_im = to_lanes(bb_re), to_lanes(bb_im)
    e_re = pw_re[:, :, :, None] * bt_re[:, :, None] - pw_im[:, :, :, None] * bt_im[:, :, None]
    e_im = pw_re[:, :, :, None] * bt_im[:, :, None] + pw_im[:, :, :, None] * bt_re[:, :, None]
    ez = jnp.stack([e_re[0][:, ::-1], e_im[0][:, ::-1], e_re[1], e_im[1]], axis=2)

    cr = c_re.astype(F32).reshape(2, GB, GPB, C, P)
    ci = c_im.astype(F32).reshape(2, GB, GPB, C, P)
    e6_re, e6_im = e_re.reshape(2, GB, T, C, GPB, P), e_im.reshape(2, GB, T, C, GPB, P)
    kk = (jnp.einsum("dbgop,dbtigp->dbgito", cr, e6_re, precision=hp)
          - jnp.einsum("dbgop,dbtigp->dbgito", ci, e6_im, precision=hp))
    skip = jnp.eye(C, dtype=F32) * d_skip.astype(F32).reshape(GB, GPB, C, 1)
    center = kk[0][..., 0, :] + kk[1][..., 0, :] + skip
    kl = jnp.concatenate([kk[1][..., :0:-1, :], center[..., None, :], kk[0][..., 1:, :],
                          jnp.zeros((GB, GPB, C, 1, C), F32)], axis=3)
    kl = kl.reshape(GB, LANES, S5_TAPS * C)

    lar_re, lar_im = la_re[..., None], la_im[..., None]
    steps = jnp.stack([jnp.arange(1, T + 1, dtype=F32), jnp.arange(T, 0, -1).astype(F32)])
    pr_re, pr_im = cpow(steps.reshape(2, 1, 1, T), lar_re, lar_im)
    lane = jnp.arange(T * C)
    rep_c = (jnp.arange(C)[:, None] == lane[None, :] % C).astype(F32)
    rep_t = (jnp.arange(T)[:, None] == lane[None, :] // C).astype(F32)
    cp = jnp.stack([c_re, c_im]).astype(F32).transpose(0, 1, 2, 4, 3)
    cp = jnp.einsum("rdgpc,cq->rdgpq", cp, rep_c, precision=hp)
    pr = jnp.einsum("rdgpt,tq->rdgpq", jnp.stack([pr_re, pr_im]), rep_t, precision=hp)
    cl_re = cp[0] * pr[0] - cp[1] * pr[1]
    cl_im = cp[0] * pr[1] + cp[1] * pr[0]
    cl = jnp.stack([cl_re[0], -cl_im[0], cl_re[1], -cl_im[1]], axis=0)
    cl = cl.reshape(4, GB, SW, T * C)

    kf = jnp.arange(ncl, dtype=F32).reshape(1, ncl, 1) * float(T)
    af_re, af_im = cpow(kf, lad_re[0], lad_im[0])
    ab_re, ab_im = cpow(kf[:, ::-1], lad_re[1], lad_im[1])
    tab = jnp.concatenate([af_re, af_im, ab_re, ab_im], axis=-1)
    return ez.astype(BF16), kl.astype(BF16), cl.astype(BF16), tab


def s5_select_matrix():
    r = jnp.arange(S5_SEL_IN)
    q = jnp.arange(S5_SEL_OUT)
    return ((r[:, None] // S5_GROUP == q[None, :] // LANES)
            & (r[:, None] % S5_GROUP == q[None, :] % S5_GROUP)).astype(BF16)


MXU_N = 256
LOG2E = 1.4426950408889634


def _seg_mean_sq(x, ones_bd):
    sq = (x * x).astype(BF16)
    parts = [jnp.dot(sq[:, t:t + MXU_N], ones_bd, preferred_element_type=F32)
             for t in range(0, x.shape[1], MXU_N)]
    return jnp.concatenate(parts, axis=1) * (1.0 / DA_HEAD)


def _rope(x, cos, sin_signed):
    n = x.shape[1]
    lane = lax.broadcasted_iota(jnp.int32, x.shape, 1)
    first = (lane % (DA_HEAD // 2)) < (DA_HEAD // 4)
    rot = jnp.where(first, pltpu.roll(x, n - DA_HEAD // 4, 1), pltpu.roll(x, DA_HEAD // 4, 1))
    return x * cos + rot * sin_signed


def _qk_prep_kernel(q_ref, k_ref, *rest, rope):
    if rope:
        cos_ref, sin_ref, qg_ref, kg_ref, ones_ref, o_ref = rest
    else:
        qg_ref, kg_ref, ones_ref, o_ref = rest
    ones_bd = ones_ref[...]
    reps = BRANCH // LANES
    for x_ref, g_ref, scale, c0 in ((q_ref, qg_ref, DA_HEAD ** -0.5 * LOG2E, 0), (k_ref, kg_ref, 1.0, BRANCH)):
        x = x_ref[...].astype(F32)
        x = x * lax.rsqrt(_seg_mean_sq(x, ones_bd) + EPS) * jnp.concatenate([g_ref[...]] * reps, axis=1)
        if rope:
            x = _rope(x, jnp.concatenate([cos_ref[...]] * reps, axis=1),
                      jnp.concatenate([sin_ref[...]] * reps, axis=1))
        o_ref[:, c0:c0 + BRANCH] = (x * scale).astype(o_ref.dtype)


def qk_prep(hin, q_block, k_block, qg, kg, ones_bd, rope_tabs, seq_len):
    m = hin.shape[0]
    tm = min(512, m)
    rope = rope_tabs is not None
    in_specs = [pl.BlockSpec((tm, BRANCH), lambda i: (i, q_block)),
                pl.BlockSpec((tm, BRANCH), lambda i: (i, k_block))]
    args = [hin, hin]
    if rope:
        tps = seq_len // tm
        in_specs += [pl.BlockSpec((tm, LANES), lambda i: (i % tps, 0)),
                     pl.BlockSpec((tm, LANES), lambda i: (i % tps, 0))]
        args += list(rope_tabs)
    in_specs += [pl.BlockSpec((1, LANES), lambda i: (0, 0)), pl.BlockSpec((1, LANES), lambda i: (0, 0)),
                 pl.BlockSpec((MXU_N, MXU_N), lambda i: (0, 0))]
    args += [qg, kg, ones_bd]
    return pl.pallas_call(
        functools.partial(_qk_prep_kernel, rope=rope),
        grid=(m // tm,),
        in_specs=in_specs,
        out_specs=pl.BlockSpec((tm, 2 * BRANCH), lambda i: (i, 0)),
        out_shape=jax.ShapeDtypeStruct((m, 2 * BRANCH), BF16),
        compiler_params=_cparams(("arbitrary",)),
        name="qk_prep",
    )(*args)


def _stack_maps(q):
    lane = lax.broadcasted_iota(jnp.int32, q.shape, 1)
    return jnp.concatenate([jnp.where(lane < DA_HEAD, q, 0.0), jnp.where(lane >= DA_HEAD, q, 0.0)], axis=0)


def _diff_combine(pv, tq, lam, sg, lam_scale, zd):
    o = pv[:, :DA_VDIM] / pv[:, DA_VDIM:]
    o = o[:tq] - lam * o[tq:]
    o = o * lax.rsqrt(jnp.mean(o * o, axis=-1, keepdims=True) + EPS) * sg * lam_scale
    return o * (zd * jax.nn.sigmoid(zd))


def _diff_attn_ctx_kernel(q_ref, kc_ref, vc_ref, zd_ref, sg_ref, lam_ref, o_ref, *, lam_scale):
    v1 = jnp.concatenate([vc_ref[...], jnp.ones_like(vc_ref)], axis=1)
    tq = q_ref.shape[0]
    s = lax.dot_general(_stack_maps(q_ref[...]), kc_ref[...], (((1,), (1,)), ((), ())),
                        preferred_element_type=F32)
    p = jnp.exp2(s - jnp.max(s, axis=-1, keepdims=True))
    pv = jnp.dot(p.astype(BF16), v1, preferred_element_type=F32)
    o_ref[...] = _diff_combine(pv, tq, lam_ref[...], sg_ref[...], lam_scale,
                               zd_ref[...].astype(F32)).astype(o_ref.dtype)


def _diff_attn_lat_kernel(q_ref, kl_ref, vl_ref, kc_ref, vc_ref, zd_ref, sg_ref, lam_ref, o_ref,
                          kn_ref, v1_ref, s0_ref, s1_ref, m0_ref, m1_ref, *, tq, lam_scale):
    nctx = kc_ref.shape[0]
    nq = q_ref.shape[0] // tq

    kn_ref[:, 0:nctx] = jnp.transpose(kc_ref[...].astype(F32)).astype(BF16)
    kn_ref[:, nctx:] = jnp.transpose(kl_ref[...].astype(F32)).astype(BF16)
    v1_ref[0:nctx, :] = jnp.concatenate([vc_ref[...], jnp.ones_like(vc_ref)], axis=1)
    v1_ref[nctx:, :] = jnp.concatenate([vl_ref[...], jnp.ones_like(vl_ref)], axis=1)

    bufs = ((s0_ref, m0_ref), (s1_ref, m1_ref))

    def scores(i, slot):
        s_ref, m_ref = bufs[slot]
        rows = pl.ds(pl.multiple_of(i * tq, tq), tq)
        s = jnp.dot(_stack_maps(q_ref[rows, :]), kn_ref[...], preferred_element_type=F32)
        s_ref[...] = s
        m_ref[...] = jnp.broadcast_to(jnp.max(s, axis=-1, keepdims=True), m_ref.shape)

    def finish(i, slot):
        s_ref, m_ref = bufs[slot]
        rows = pl.ds(pl.multiple_of(i * tq, tq), tq)
        p = jnp.exp2(s_ref[...] - m_ref[:, 0:1])
        pv = jnp.dot(p.astype(BF16), v1_ref[...], preferred_element_type=F32)
        o = _diff_combine(pv, tq, lam_ref[...], sg_ref[...], lam_scale, zd_ref[rows, :].astype(F32))
        o_ref[rows, :] = o.astype(o_ref.dtype)

    assert nq % 2 == 0
    scores(0, 0)

    def body(j, carry):
        scores(2 * j + 1, 1)
        finish(2 * j, 0)
        scores(2 * j + 2, 0)
        finish(2 * j + 1, 1)
        return carry

    lax.fori_loop(0, nq // 2 - 1, body, 0)
    scores(nq - 1, 1)
    finish(nq - 2, 0)
    finish(nq - 1, 1)


def diff_attention(hin_lat, hin_ctx, col_q, col_k, col_v, col_zd, batch, qg, kg, sg, lam_row, lam_init,
                   cos_t, sin_t, ones_bd):
    ml, mc = hin_lat.shape[0], hin_ctx.shape[0]
    L, nctx = ml // batch, mc // batch
    tq = min(256, L)
    qk_lat = qk_prep(hin_lat, col_q * LANES // BRANCH, col_k * LANES // BRANCH, qg, kg, ones_bd,
                     (cos_t, sin_t), L)
    qk_ctx = qk_prep(hin_ctx, col_q * LANES // BRANCH, col_k * LANES // BRANCH, qg, kg, ones_bd, None, nctx)
    kcol = BRANCH // LANES
    small = lambda b, h: (0, 0)
    common = [pl.BlockSpec((1, LANES), small), pl.BlockSpec((1, LANES), small)]
    cargs = [sg, lam_row]
    lam_scale = 1.0 - lam_init
    nk = nctx + L
    y_lat = pl.pallas_call(
        functools.partial(_diff_attn_lat_kernel, tq=tq, lam_scale=lam_scale),
        grid=(batch, DA_HEADS),
        in_specs=[
            pl.BlockSpec((L, LANES), lambda b, h: (b, h)),
            pl.BlockSpec((L, LANES), lambda b, h: (b, kcol + h)),
            pl.BlockSpec((L, LANES), lambda b, h: (b, col_v + h)),
            pl.BlockSpec((nctx, LANES), lambda b, h: (b, kcol + h)),
            pl.BlockSpec((nctx, LANES), lambda b, h: (b, col_v + h)),
            pl.BlockSpec((L, LANES), lambda b, h: (b, col_zd + h)),
        ] + common,
        out_specs=pl.BlockSpec((L, LANES), lambda b, h: (b, h)),
        out_shape=jax.ShapeDtypeStruct((ml, BRANCH), BF16),
        scratch_shapes=[
            pltpu.VMEM((LANES, nk), BF16),
            pltpu.VMEM((nk, 2 * DA_VDIM), BF16),
            pltpu.VMEM((2 * tq, nk), F32), pltpu.VMEM((2 * tq, nk), F32),
            pltpu.VMEM((2 * tq, LANES), F32), pltpu.VMEM((2 * tq, LANES), F32),
        ],
        compiler_params=_cparams(("arbitrary", "arbitrary")),
        name="diff_attn_lat",
    )(qk_lat, qk_lat, hin_lat, qk_ctx, hin_ctx, hin_lat, *cargs)
    y_ctx = pl.pallas_call(
        functools.partial(_diff_attn_ctx_kernel, lam_scale=lam_scale),
        grid=(batch, DA_HEADS),
        in_specs=[
            pl.BlockSpec((nctx, LANES), lambda b, h: (b, h)),
            pl.BlockSpec((nctx, LANES), lambda b, h: (b, kcol + h)),
            pl.BlockSpec((nctx, LANES), lambda b, h: (b, col_v + h)),
            pl.BlockSpec((nctx, LANES), lambda b, h: (b, col_zd + h)),
        ] + common,
        out_specs=pl.BlockSpec((nctx, LANES), lambda b, h: (b, h)),
        out_shape=jax.ShapeDtypeStruct((mc, BRANCH), BF16),
        compiler_params=_cparams(("arbitrary", "arbitrary")),
        name="diff_attn_ctx",
    )(qk_ctx, qk_ctx, hin_ctx, hin_ctx, *cargs)
    return y_lat, y_ctx


def seg_ones():
    seg = jnp.arange(MXU_N) // DA_HEAD
    return (seg[:, None] == seg[None, :]).astype(BF16)


def rope_tables(n_tokens):
    rows = n_tokens // GRID_W
    row = jnp.repeat(jnp.arange(rows, dtype=F32), GRID_W)
    col = jnp.tile(jnp.arange(GRID_W, dtype=F32), rows)
    n_freq = DA_HEAD // 4
    inv_freq = ROPE_BASE ** (-jnp.arange(n_freq, dtype=F32) / n_freq)
    ang_r = row[:, None] * inv_freq
    ang_c = col[:, None] * inv_freq
    ang = jnp.concatenate([ang_r, ang_r, ang_c, ang_c], axis=-1)
    sign = jnp.tile(jnp.concatenate([-jnp.ones(n_freq, F32), jnp.ones(n_freq, F32)]), 2)
    cos = jnp.tile(jnp.cos(ang), (1, 2))
    sin_signed = jnp.tile(jnp.sin(ang) * sign, (1, 2))
    return cos, sin_signed


def _mm_out_kernel(*refs, n_parts, with_next):
    a_refs, w_refs = refs[:n_parts], refs[n_parts:2 * n_parts]
    rest = refs[2 * n_parts:]
    if with_next:
        x_ref, g_ref, gs_ref, sh_ref, o_ref, hn_ref = rest
    else:
        x_ref, g_ref, o_ref = rest
    acc = None
    for a_ref, w_ref in zip(a_refs, w_refs):
        t = jnp.dot(a_ref[...], w_ref[...], preferred_element_type=F32)
        acc = t if acc is None else acc + t
    xn = x_ref[...] + g_ref[...] * acc
    o_ref[...] = xn
    if with_next:
        hn_ref[...] = _modulate(xn, gs_ref[...], sh_ref[...]).astype(hn_ref.dtype)


def mm_out(a_parts, w_stack, layer, x, gate, next_mod, rows_per_mod):
    m = x.shape[0]
    n_parts = len(a_parts)
    kp = w_stack.shape[1] // n_parts
    tm = min(512, rows_per_mod)
    tpm = rows_per_mod // tm
    mod_spec = pl.BlockSpec((None, 1, D_MODEL), lambda i: (i // tpm, 0, 0))
    in_specs = [pl.BlockSpec((tm, kp), lambda i: (i, 0)) for _ in a_parts]
    in_specs += [pl.BlockSpec((None, kp, D_MODEL), functools.partial(lambda i, p: (layer, p, 0), p=p))
                 for p in range(n_parts)]
    in_specs += [pl.BlockSpec((tm, D_MODEL), lambda i: (i, 0)), mod_spec]
    args = [*a_parts, *([w_stack] * n_parts), x, gate]
    out_specs = [pl.BlockSpec((tm, D_MODEL), lambda i: (i, 0))]
    out_shape = [jax.ShapeDtypeStruct((m, D_MODEL), F32)]
    if next_mod is not None:
        in_specs += [mod_spec, mod_spec]
        args += list(next_mod)
        out_specs.append(pl.BlockSpec((tm, D_MODEL), lambda i: (i, 0)))
        out_shape.append(jax.ShapeDtypeStruct((m, D_MODEL), BF16))
    outs = pl.pallas_call(
        functools.partial(_mm_out_kernel, n_parts=n_parts, with_next=next_mod is not None),
        grid=(m // tm,),
        in_specs=in_specs,
        out_specs=out_specs,
        out_shape=out_shape,
        compiler_params=_cparams(("arbitrary",)),
        name="mm_out",
    )(*args)
    return (outs[0], outs[1]) if next_mod is not None else (outs[0], None)


GLA_STEP = 4


def _seg_cumsum(x, reverse):
    n = x.shape[0]
    r = lax.broadcasted_iota(jnp.int32, (MXU_N, MXU_N), 0)
    c = lax.broadcasted_iota(jnp.int32, (MXU_N, MXU_N), 1)
    same = (r // GLA_CHUNK) == (c // GLA_CHUNK)
    tri = (same & ((c >= r) if reverse else (c <= r))).astype(BF16)
    hi = x.astype(BF16)
    lo = (x - hi.astype(F32)).astype(BF16)
    parts = []
    for t in range(0, n, MXU_N):
        parts.append(jnp.dot(tri, hi[t:t + MXU_N], preferred_element_type=F32)
                     + jnp.dot(tri, lo[t:t + MXU_N], preferred_element_type=F32))
    return jnp.concatenate(parts, axis=0)


def _gla_kernel(ql_ref, kl_ref, vl_ref, rl_ref, zl_ref, qc_ref, kc_ref, vc_ref, rc_ref, zc_ref,
                wa_ref, ba_ref, ng_ref, *rest, ctx_out):
    if ctx_out:
        ol_ref, oc_ref = rest[:2]
        rest = rest[2:]
    else:
        ol_ref, oc_ref = rest[0], None
        rest = rest[1:]
    sf_ref, sb_ref, kv_ref, sin_ref, qd_ref, kd_ref, b_ref, vb_ref, ofl_ref, ofc_ref, obl_ref, obc_ref = rest
    C = GLA_CHUNK
    L, nctx = ql_ref.shape[0], kc_ref.shape[0]
    nl, nc = L // C, nctx // C
    row = lax.broadcasted_iota(jnp.int32, (C, C), 0)
    col = lax.broadcasted_iota(jnp.int32, (C, C), 1)

    for base, n, q_ref, k_ref, v_ref, r_ref in ((0, nctx, qc_ref, kc_ref, vc_ref, rc_ref),
                                                (nctx, L, ql_ref, kl_ref, vl_ref, rl_ref)):
        rows = slice(base, base + n)
        vb_ref[rows, :] = v_ref[...].astype(BF16)
        r = r_ref[...].astype(BF16)
        k = k_ref[...].astype(F32)
        with_q = ctx_out or base > 0
        for d in range(2):
            logits = jnp.dot(r, wa_ref[d], preferred_element_type=F32) + ba_ref[d]
            ls = jnp.minimum(logits, 0.0) - jnp.log(1.0 + jnp.exp(-jnp.abs(logits)))
            b = _seg_cumsum(ls * (1.0 / GLA_TAU), reverse=d == 1)
            b_ref[d, rows, :] = b
            kd_ref[d, rows, :] = (k * jnp.exp(-b)).astype(BF16)
            if with_q:
                qd_ref[d, rows, :] = (q_ref[...].astype(F32) * (GLA_DK ** -0.5) * jnp.exp(b)).astype(BF16)

    sf_ref[...] = jnp.zeros_like(sf_ref)
    sb_ref[...] = jnp.zeros_like(sb_ref)

    def segment(base, n_chunks, of_ref, ob_ref):
        step = min(GLA_STEP, n_chunks)
        assert n_chunks % step == 0

        def body(i, carry):
            work = []
            for u in range(step):
                cf = i * step + u
                cb = n_chunks - 1 - cf
                work.append((0, u, pl.multiple_of(base + cf * C, C), pl.multiple_of(cf * C, C)))
                work.append((1, step + u, pl.multiple_of(base + cb * C, C), pl.multiple_of(cb * C, C)))
            outs = {0: of_ref, 1: ob_ref}
            with_out = of_ref is not None
            scores, decs = {}, {}
            if with_out:
                for d, slot, row0, _ in work:
                    rows = pl.ds(row0, C)
                    scores[slot] = lax.dot_general(qd_ref[d, rows, :], kd_ref[d, rows, :],
                                                   (((1,), (1,)), ((), ())), preferred_element_type=F32)
            for d, slot, row0, _ in work:
                rows = pl.ds(row0, C)
                dec = jnp.exp(b_ref[d, pl.ds(row0 if d == 1 else row0 + C - 1, 1), :])
                k_end = (kd_ref[d, rows, :].astype(F32) * dec).astype(BF16)
                kv_ref[slot] = lax.dot_general(k_end, vb_ref[rows, :], (((0,), (0,)), ((), ())),
                                               preferred_element_type=F32)
                decs[slot] = jnp.transpose(jnp.broadcast_to(dec, (LANES, GLA_DK)))[:, 0:1]
            if with_out:
                for d, slot, row0, orow0 in work:
                    incl = (col >= row) if d == 1 else (col <= row)
                    sc = jnp.where(incl, scores[slot], 0.0).astype(BF16)
                    outs[d][pl.ds(orow0, C), :] = jnp.dot(sc, vb_ref[pl.ds(row0, C), :],
                                                          preferred_element_type=F32)
            for d, slot, _, _ in work:
                s_ref = sb_ref if d == 1 else sf_ref
                s_old = s_ref[...]
                sin_ref[slot] = s_old.astype(BF16)
                s_ref[...] = decs[slot] * s_old + kv_ref[slot]
            if with_out:
                for d, slot, row0, orow0 in work:
                    outs[d][pl.ds(orow0, C), :] += jnp.dot(qd_ref[d, pl.ds(row0, C), :], sin_ref[slot],
                                                           preferred_element_type=F32)
            return carry

        lax.fori_loop(0, n_chunks // step, body, 0)

    segment(0, nc, ofc_ref if ctx_out else None, obc_ref if ctx_out else None)
    segment(nctx, nl, ofl_ref, obl_ref)
    def finish(of_ref, ob_ref, z_ref, o_ref):
        o = of_ref[...] + ob_ref[...]
        z = z_ref[...].astype(F32)
        y = o * lax.rsqrt(jnp.mean(o * o, axis=-1, keepdims=True) + EPS) * ng_ref[...]
        o_ref[...] = (y * (z * jax.nn.sigmoid(z))).astype(o_ref.dtype)

    finish(ofl_ref, obl_ref, zl_ref, ol_ref)
    if ctx_out:
        finish(ofc_ref, obc_ref, zc_ref, oc_ref)


def gla_mix(hin_lat, hin_ctx, r_lat, r_ctx, wa2p, ba, norm_g, batch, ctx_out):
    ml, mc = hin_lat.shape[0], hin_ctx.shape[0]
    L, nctx = ml // batch, mc // batch
    kb = GLA_KEY // GLA_DK
    vb = 2 * GLA_KEY // GLA_DV
    zb = (2 * GLA_KEY + GLA_VAL) // GLA_DV

    def seg_specs(n, kv_only):
        qo, ko, vo, zo = (0, 0, GLA_KEY // GLA_DV, GLA_KEY // GLA_DV) if kv_only else (0, kb, vb, zb)
        return [pl.BlockSpec((n, GLA_DK), lambda b, h: (b, qo + h)),
                pl.BlockSpec((n, GLA_DK), lambda b, h: (b, ko + h)),
                pl.BlockSpec((n, GLA_DV), lambda b, h: (b, vo + h)),
                pl.BlockSpec((n, LANES), lambda b, h: (b, 0)),
                pl.BlockSpec((n, GLA_DV), lambda b, h: (b, zo + h))]

    in_specs = seg_specs(L, False) + seg_specs(nctx, not ctx_out) + [
        pl.BlockSpec((2, LANES, GLA_DK), lambda b, h: (0, 0, h)),
        pl.BlockSpec((2, 1, GLA_DK), lambda b, h: (0, 0, h)),
        pl.BlockSpec((1, GLA_DV), lambda b, h: (0, 0)),
    ]
    out_specs = [pl.BlockSpec((L, GLA_DV), lambda b, h: (b, h)),
                 pl.BlockSpec((nctx, GLA_DV), lambda b, h: (b, h))]
    out_shape = [jax.ShapeDtypeStruct((ml, GLA_VAL), BF16), jax.ShapeDtypeStruct((mc, GLA_VAL), BF16)]
    if not ctx_out:
        out_specs, out_shape = out_specs[:1], out_shape[:1]
    outs = pl.pallas_call(
        functools.partial(_gla_kernel, ctx_out=ctx_out),
        grid=(batch, GLA_HEADS),
        in_specs=in_specs,
        out_specs=out_specs,
        out_shape=out_shape,
        scratch_shapes=[
            pltpu.VMEM((GLA_DK, GLA_DV), F32), pltpu.VMEM((GLA_DK, GLA_DV), F32),
            pltpu.VMEM((2 * GLA_STEP, GLA_DK, GLA_DV), F32),
            pltpu.VMEM((2 * GLA_STEP, GLA_DK, GLA_DV), BF16),
            pltpu.VMEM((2, nctx + L, GLA_DK), BF16),
            pltpu.VMEM((2, nctx + L, GLA_DK), BF16),
            pltpu.VMEM((2, nctx + L, GLA_DK), F32),
            pltpu.VMEM((nctx + L, GLA_DV), BF16),
            pltpu.VMEM((L, GLA_DV), F32), pltpu.VMEM((nctx, GLA_DV), F32),
            pltpu.VMEM((L, GLA_DV), F32), pltpu.VMEM((nctx, GLA_DV), F32),
        ],
        compiler_params=_cparams(("arbitrary", "arbitrary")),
        name="gla_mix",
    )(hin_lat, hin_lat, hin_lat, r_lat, hin_lat, hin_ctx, hin_ctx, hin_ctx, r_ctx, hin_ctx, wa2p, ba, norm_g)
    return (outs[0], outs[1]) if ctx_out else (outs[0], None)


def _mods(mod_l, norm_g, batch):
    shift, scale, gate = mod_l[:, :D_MODEL], mod_l[:, D_MODEL:2 * D_MODEL], mod_l[:, 2 * D_MODEL:]
    gs = (norm_g.astype(F32)[None, :] * (1.0 + scale))[:, None, :]
    sh = shift[:, None, :]
    gt = gate[:, None, :]
    lat = (gs[:batch], sh[:batch], gt[:batch])
    ctx = (gs[batch:batch + 1], sh[batch:batch + 1], gt[batch:batch + 1])
    return lat, ctx


def even_layer(x_lat, x_ctx, hn_l, hn_c, gates, next_mods, w_in, w_out, j, s5p, w_glu, qn_g, kn_g,
               lam_vecs, subln_g, lam_init, batch, rope):
    L = x_lat.shape[0] // batch
    gt_l, gt_c = gates
    rest = 5 * BRANCH
    u3_l = mm_in(hn_l, w_in, j, 0, BRANCH, out_dtype=BF16, gb_out=True, name="mm_in_u")
    u3_c = mm_in(hn_c, w_in, j, 0, BRANCH, out_dtype=BF16, gb_out=True, name="mm_in_u_ctx")
    hin_l = mm_in(hn_l, w_in, j, BRANCH, rest, out_dtype=BF16, name="mm_in_rest")
    hin_c = mm_in(hn_c, w_in, j, BRANCH, rest, out_dtype=BF16, name="mm_in_rest_ctx")

    ez, kl, cl, ptab, sel = s5p
    h_l, h_c = s5_states(u3_l, u3_c, ez, ptab, j, batch)
    y_l, y_c = s5_outputs(u3_l, u3_c, h_l, h_c, kl, cl, sel, j)
    wg = w_glu.astype(BF16)
    a_s5_l = s5_finish(y_l, wg, hin_l, 0)
    a_s5_c = s5_finish(y_c, wg, hin_c, 0)

    lv = lam_vecs.astype(F32)
    lam = jnp.exp(jnp.sum(lv[0] * lv[1])) - jnp.exp(jnp.sum(lv[2] * lv[3])) + lam_init
    lam_row = jnp.full((1, LANES), lam, F32)
    qg = jnp.tile(qn_g.astype(F32), 2)[None, :]
    kg = jnp.tile(kn_g.astype(F32), 2)[None, :]
    sg = subln_g.astype(F32)[None, :]
    cb = BRANCH // LANES
    cos_t, sin_t, ones_bd = rope
    a_da_l, a_da_c = diff_attention(hin_l, hin_c, cb, 2 * cb, 3 * cb, 4 * cb, batch, qg, kg, sg, lam_row,
                                    lam_init, cos_t, sin_t, ones_bd)

    nm_l, nm_c = next_mods
    x_lat, hn_l = mm_out([a_s5_l, a_da_l], w_out, j, x_lat, gt_l, nm_l, L)
    x_ctx, hn_c = mm_out([a_s5_c, a_da_c], w_out, j, x_ctx, gt_c, nm_c, x_ctx.shape[0])
    return x_lat, x_ctx, hn_l, hn_c


def odd_layer(x_lat, x_ctx, hn_l, hn_c, gates, next_mods, w_in, w_out, j, wa1, wa2, ba, gla_norm_g, batch,
              with_ctx_out):
    L = x_lat.shape[0] // batch
    gt_l, gt_c = gates
    n = 3 * D_MODEL
    w_aux = jnp.zeros((D_MODEL, LANES), F32).at[:, :GLA_RANK].set(wa1[0]).at[:, GLA_RANK:2 * GLA_RANK].set(wa1[1])
    w_aux = w_aux.astype(BF16)
    hin_l, r_l = mm_in(hn_l, w_in, j, 0, n, out_dtype=BF16, w_aux=w_aux, name="mm_in_odd")
    if with_ctx_out:
        hin_c, r_c = mm_in(hn_c, w_in, j, 0, n, out_dtype=BF16, w_aux=w_aux, name="mm_in_odd_ctx")
    else:
        hin_c, r_c = mm_in(hn_c, w_in, j, GLA_KEY, GLA_KEY + GLA_VAL, out_dtype=BF16, w_aux=w_aux,
                           name="mm_in_odd_ctx_kv")
    wa2p = jnp.zeros((2, LANES, GLA_KEY), F32)
    wa2p = wa2p.at[0, :GLA_RANK].set(wa2[0]).at[1, GLA_RANK:2 * GLA_RANK].set(wa2[1]).astype(BF16)
    ng = gla_norm_g.astype(F32)[None, :]
    a_l, a_c = gla_mix(hin_l, hin_c, r_l, r_c, wa2p, ba.astype(F32).reshape(2, 1, GLA_KEY), ng, batch,
                       with_ctx_out)
    nm_l, nm_c = next_mods
    x_lat, hn_l = mm_out([a_l], w_out, j, x_lat, gt_l, nm_l, L)
    if with_ctx_out:
        x_ctx, hn_c = mm_out([a_c], w_out, j, x_ctx, gt_c, nm_c, x_ctx.shape[0])
    return x_lat, x_ctx, hn_l, hn_c


def kernel(x, c, ctx, c_ctx, ada_w, ada_b, norm_g, ev_w_in, ev_w_out, s5_a_re, s5_a_im, s5_log_dt, s5_b_re, s5_b_im, s5_c_re, s5_c_im, s5_d, s5_w_glu, da_qn_g, da_kn_g, da_lam, da_subln_g, od_w_in, od_w_out, gla_wa1, gla_wa2, gla_ba, gla_norm_g):
    batch, L, _ = x.shape
    nctx = ctx.shape[1]
    x_lat = x.reshape(batch * L, D_MODEL)
    x_ctx = ctx.reshape(batch * nctx, D_MODEL)
    cond = jnp.zeros((8, D_MODEL), F32).at[:batch].set(c).at[batch].set(c_ctx)
    mod = modulation_all(cond, ada_w, ada_b)

    cos_t, sin_t = rope_tables(L)
    rope = (cos_t, sin_t, seg_ones())
    mods = [_mods(mod[i], norm_g[i], batch) for i in range(DEPTH)]
    ev_w_out_b, od_w_out_b = ev_w_out.astype(BF16), od_w_out.astype(BF16)

    s5p = jax.vmap(functools.partial(s5_prepare, ncl=L // S5_T))(
        s5_a_re, s5_a_im, s5_log_dt, s5_b_re, s5_b_im, s5_c_re, s5_c_im, s5_d) + (s5_select_matrix(),)

    (gs_l, sh_l, _), (gs_c, sh_c, _) = mods[0]
    hn_l = prenorm(x_lat, gs_l, sh_l, L)
    hn_c = prenorm(x_ctx, gs_c, sh_c, x_ctx.shape[0])
    for i in range(DEPTH):
        j = i // 2
        with_ctx_out = i < DEPTH - 1
        gates = (mods[i][0][2], mods[i][1][2])
        next_mods = (mods[i + 1][0][:2], mods[i + 1][1][:2]) if with_ctx_out else (None, None)
        if i % 2 == 0:
            lam_init = 0.8 - 0.6 * math.exp(-0.3 * i)
            x_lat, x_ctx, hn_l, hn_c = even_layer(
                x_lat, x_ctx, hn_l, hn_c, gates, next_mods, ev_w_in, ev_w_out_b, j, s5p, s5_w_glu[j],
                da_qn_g[j], da_kn_g[j], da_lam[j], da_subln_g[j], lam_init, batch, rope)
        else:
            x_lat, x_ctx, hn_l, hn_c = odd_layer(
                x_lat, x_ctx, hn_l, hn_c, gates, next_mods, od_w_in, od_w_out_b, j, gla_wa1[j], gla_wa2[j],
                gla_ba[j], gla_norm_g[j], batch, with_ctx_out)
    return x_lat.reshape(batch, L, D_MODEL)
```

```python
import functools
import math

import jax
import jax.numpy as jnp
from jax import lax
from jax.experimental import pallas as pl
from jax.experimental.pallas import tpu as pltpu

F32 = jnp.float32
BF16 = jnp.bfloat16

D_MODEL = 2048
DEPTH = 4
GRID_W = 64
EPS = 1e-6
BRANCH = D_MODEL // 2
S5_GROUP = 16
S5_GROUPS = BRANCH // S5_GROUP
S5_STATE = 64
DA_HEAD = 64
DA_HEADS = BRANCH // (2 * DA_HEAD)
DA_VDIM = 2 * DA_HEAD
ROPE_BASE = 10000.0
GLA_HEADS = 4
GLA_KEY = D_MODEL // 2
GLA_VAL = D_MODEL
GLA_DK = GLA_KEY // GLA_HEADS
GLA_DV = GLA_VAL // GLA_HEADS
GLA_RANK = 16
GLA_TAU = 16.0
GLA_CHUNK = 64

LANES = 128
VMEM_LIMIT = 56 * 1024 * 1024

S5_T = 16
S5_GB = BRANCH // LANES
S5_GPB = LANES // S5_GROUP
S5_W = S5_T * LANES
S5_SW = S5_GPB * S5_STATE


def _cparams(sem):
    return pltpu.CompilerParams(dimension_semantics=sem, vmem_limit_bytes=VMEM_LIMIT)


def _mod_kernel(c_ref, w_ref, b_ref, o_ref):
    c = c_ref[...]
    s = (c * jax.nn.sigmoid(c)).astype(BF16)
    acc = jnp.dot(s, w_ref[...].astype(BF16), preferred_element_type=F32)
    o_ref[...] = acc + b_ref[...]


def modulation_all(cond, ada_w, ada_b):
    tn = 512
    n = 3 * D_MODEL
    return pl.pallas_call(
        _mod_kernel,
        grid=(DEPTH, n // tn),
        in_specs=[
            pl.BlockSpec((8, D_MODEL), lambda l, j: (0, 0)),
            pl.BlockSpec((None, D_MODEL, tn), lambda l, j: (l, 0, j)),
            pl.BlockSpec((None, 1, tn), lambda l, j: (l, 0, j)),
        ],
        out_specs=pl.BlockSpec((None, 8, tn), lambda l, j: (l, 0, j)),
        out_shape=jax.ShapeDtypeStruct((DEPTH, 8, n), F32),
        compiler_params=_cparams(("arbitrary", "arbitrary")),
        name="modulation",
    )(cond, ada_w, ada_b.reshape(DEPTH, 1, n))


def _modulate(x, gs, sh):
    ms = jnp.mean(x * x, axis=-1, keepdims=True)
    return x * lax.rsqrt(ms + EPS) * gs + sh


def _prenorm_kernel(x_ref, gs_ref, sh_ref, o_ref):
    o_ref[...] = _modulate(x_ref[...], gs_ref[...], sh_ref[...]).astype(o_ref.dtype)


def prenorm(x, gs, sh, rows_per_mod):
    m = x.shape[0]
    tm = min(512, rows_per_mod)
    tpm = rows_per_mod // tm
    return pl.pallas_call(
        _prenorm_kernel,
        grid=(m // tm,),
        in_specs=[
            pl.BlockSpec((tm, D_MODEL), lambda i: (i, 0)),
            pl.BlockSpec((None, 1, D_MODEL), lambda i: (i // tpm, 0, 0)),
            pl.BlockSpec((None, 1, D_MODEL), lambda i: (i // tpm, 0, 0)),
        ],
        out_specs=pl.BlockSpec((tm, D_MODEL), lambda i: (i, 0)),
        out_shape=jax.ShapeDtypeStruct((m, D_MODEL), BF16),
        compiler_params=_cparams(("arbitrary",)),
        name="prenorm",
    )(x, gs, sh)


def _mm_in_kernel(a_ref, w_ref, *rest, gb_out, has_aux):
    if has_aux:
        wa_ref, o_ref, aux_ref = rest

        @pl.when(pl.program_id(1) == 0)
        def _():
            aux_ref[...] = jnp.dot(a_ref[...], wa_ref[...], preferred_element_type=F32)
    elif gb_out:
        o_ref, stage_ref = rest
    else:
        (o_ref,) = rest

    acc = jnp.dot(a_ref[...], w_ref[...].astype(BF16), preferred_element_type=F32)
    if gb_out:
        n_rows = acc.shape[0] // S5_T
        for q in range(acc.shape[1] // LANES):
            stage_ref[q] = acc[:, q * LANES:(q + 1) * LANES]
        for q in range(acc.shape[1] // LANES):
            for t in range(S5_T):
                o_ref[q, :, t * LANES:(t + 1) * LANES] = stage_ref[q, pl.ds(t, n_rows, stride=S5_T), :].astype(
                    o_ref.dtype)
    else:
        o_ref[...] = acc.astype(o_ref.dtype)


def _mm_in_qk_kernel(a_ref, w_ref, *rest, rope, q_tiles, k_tiles):
    if rope:
        cos_ref, sin_ref, qg_ref, kg_ref, ones_ref, o_ref = rest
    else:
        qg_ref, kg_ref, ones_ref, o_ref = rest
    j = pl.program_id(1)
    acc = jnp.dot(a_ref[...], w_ref[...].astype(BF16), preferred_element_type=F32)
    reps = acc.shape[1] // LANES

    def normed(g_ref, scale):
        x = acc * lax.rsqrt(_seg_mean_sq(acc, ones_ref[...]) + EPS) * jnp.concatenate([g_ref[...]] * reps, axis=1)
        if rope:
            x = _rope(x, jnp.concatenate([cos_ref[...]] * reps, axis=1),
                      jnp.concatenate([sin_ref[...]] * reps, axis=1))
        return (x * scale).astype(o_ref.dtype)

    is_q = (j >= q_tiles[0]) & (j < q_tiles[1])
    is_k = (j >= k_tiles[0]) & (j < k_tiles[1])

    @pl.when(is_q)
    def _():
        o_ref[...] = normed(qg_ref, DA_HEAD ** -0.5 * LOG2E)

    @pl.when(is_k)
    def _():
        o_ref[...] = normed(kg_ref, 1.0)

    @pl.when(jnp.logical_not(is_q | is_k))
    def _():
        o_ref[...] = acc.astype(o_ref.dtype)


def mm_in_qk(a, w_stack, layer, col0, ncols, q_col, k_col, qg, kg, ones_bd, rope_tabs, name):
    m = a.shape[0]
    rope = rope_tabs is not None
    tm = rope_tabs[0].shape[0] if rope else min(2048, m)
    tn = 512
    assert m % tm == 0 and ncols % tn == 0 and col0 % tn == 0 and q_col % tn == 0 and k_col % tn == 0
    jb = col0 // tn
    small = lambda i, j: (0, 0)
    in_specs = [pl.BlockSpec((tm, D_MODEL), lambda i, j: (i, 0)),
                pl.BlockSpec((None, D_MODEL, tn), lambda i, j: (layer, 0, j + jb))]
    args = [a, w_stack]
    if rope:
        in_specs += [pl.BlockSpec((tm, LANES), small), pl.BlockSpec((tm, LANES), small)]
        args += list(rope_tabs)
    in_specs += [pl.BlockSpec((1, LANES), small), pl.BlockSpec((1, LANES), small),
                 pl.BlockSpec((MXU_N, MXU_N), small)]
    args += [qg, kg, ones_bd]
    return pl.pallas_call(
        functools.partial(_mm_in_qk_kernel, rope=rope,
                          q_tiles=(q_col // tn, (q_col + BRANCH) // tn),
                          k_tiles=(k_col // tn, (k_col + BRANCH) // tn)),
        grid=(m // tm, ncols // tn),
        in_specs=in_specs,
        out_specs=pl.BlockSpec((tm, tn), lambda i, j: (i, j)),
        out_shape=jax.ShapeDtypeStruct((m, ncols), BF16),
        compiler_params=_cparams(("arbitrary", "arbitrary")),
        name=name,
    )(*args)


def mm_in(a, w_stack, layer, col0, ncols, *, out_dtype, gb_out=False, w_aux=None, name="mm_in"):
    m = a.shape[0]
    tm = min(2048, m)
    tn = 512
    assert m % tm == 0 and ncols % tn == 0 and col0 % tn == 0
    jb = col0 // tn
    in_specs = [
        pl.BlockSpec((tm, D_MODEL), lambda i, j: (i, 0)),
        pl.BlockSpec((None, D_MODEL, tn), lambda i, j: (layer, 0, j + jb)),
    ]
    args = [a, w_stack]
    scratch = []
    if gb_out:
        out_shape = [jax.ShapeDtypeStruct((ncols // LANES, m // S5_T, S5_W), out_dtype)]
        out_specs = [pl.BlockSpec((tn // LANES, tm // S5_T, S5_W), lambda i, j: (j, i, 0))]
        scratch = [pltpu.VMEM((tn // LANES, tm, LANES), F32)]
    else:
        out_shape = [jax.ShapeDtypeStruct((m, ncols), out_dtype)]
        out_specs = [pl.BlockSpec((tm, tn), lambda i, j: (i, j))]
    if w_aux is not None:
        in_specs.append(pl.BlockSpec((D_MODEL, LANES), lambda i, j: (0, 0)))
        args.append(w_aux)
        out_shape.append(jax.ShapeDtypeStruct((m, LANES), F32))
        out_specs.append(pl.BlockSpec((tm, LANES), lambda i, j: (i, 0)))
    outs = pl.pallas_call(
        functools.partial(_mm_in_kernel, gb_out=gb_out, has_aux=w_aux is not None),
        grid=(m // tm, ncols // tn),
        in_specs=in_specs,
        out_specs=out_specs,
        out_shape=out_shape,
        scratch_shapes=scratch,
        compiler_params=_cparams(("arbitrary", "arbitrary")),
        name=name,
    )(*args)
    return outs if w_aux is not None else outs[0]


def _shift_rows(h, s, up):
    n = h.shape[0]
    row = lax.broadcasted_iota(jnp.int32, h.shape, 0)
    if up:
        return jnp.where(row >= n - s, 0.0, pltpu.roll(h, n - s, 0))
    return jnp.where(row < s, 0.0, pltpu.roll(h, s, 0))


def _chunk_scan(zr, zi, pr, pi, reverse):
    n = zr.shape[0]
    ntab = pr.shape[0]
    hr, hi = zr, zi
    s = 1
    while s < n:
        idx = ntab - 1 - s if reverse else s
        ar, ai = pr[idx:idx + 1], pi[idx:idx + 1]
        sr, si = _shift_rows(hr, s, reverse), _shift_rows(hi, s, reverse)
        hr, hi = hr + ar * sr - ai * si, hi + ar * si + ai * sr
        s *= 2
    return hr, hi


def _group_mask(shape, row_shift, col_shift):
    rg = lax.broadcasted_iota(jnp.int32, shape, 0) >> row_shift
    cg = (lax.broadcasted_iota(jnp.int32, shape, 1) >> col_shift) & (S5_GPB - 1)
    return rg == cg


def _s5_state_kernel(xl_ref, xc_ref, ez_ref, p_ref, hl_ref, hc_ref, wz_ref):
    ncc = xc_ref.shape[0]

    @pl.when(pl.program_id(1) == 0)
    def _():
        mask = _group_mask((LANES, S5_SW), 4, 6)
        for t in range(S5_T):
            for q in range(4):
                e = jnp.concatenate([ez_ref[t, q]] * S5_GPB, axis=0)
                wz_ref[t * LANES:(t + 1) * LANES, q * S5_SW:(q + 1) * S5_SW] = jnp.where(
                    mask, e, jnp.zeros_like(e))

    x = jnp.concatenate([xc_ref[...], xl_ref[...]], axis=0)
    z = jnp.dot(x, wz_ref[...], preferred_element_type=F32)
    p = p_ref[...]
    outs_c, outs_l = [], []
    for d in range(2):
        rev = d == 1
        c0 = 2 * d * S5_SW
        zr, zi = z[:, c0:c0 + S5_SW], z[:, c0 + S5_SW:c0 + 2 * S5_SW]
        pr, pi = p[:, c0:c0 + S5_SW], p[:, c0 + S5_SW:c0 + 2 * S5_SW]
        cr, ci = _chunk_scan(zr[:ncc], zi[:ncc], pr, pi, rev)
        lr, li = _chunk_scan(zr[ncc:], zi[ncc:], pr, pi, rev)
        if rev:
            car_r, car_i = cr[0:1], ci[0:1]
        else:
            car_r, car_i = cr[ncc - 1:ncc], ci[ncc - 1:ncc]
        hcr, hci = _shift_rows(cr, 1, rev), _shift_rows(ci, 1, rev)
        hlr = _shift_rows(lr, 1, rev) + pr * car_r - pi * car_i
        hli = _shift_rows(li, 1, rev) + pr * car_i + pi * car_r
        outs_c += [hcr, hci]
        outs_l += [hlr, hli]
    hc_ref[...] = jnp.concatenate(outs_c, axis=1).astype(hc_ref.dtype)
    hl_ref[...] = jnp.concatenate(outs_l, axis=1).astype(hl_ref.dtype)


def s5_states(u3_lat, u3_ctx, ez, ptab, layer, batch):
    ncl = u3_lat.shape[1] // batch
    ncc = u3_ctx.shape[1] // batch
    assert ptab.shape[2] == ncl
    return pl.pallas_call(
        _s5_state_kernel,
        grid=(S5_GB, batch),
        in_specs=[
            pl.BlockSpec((None, ncl, S5_W), lambda g, b: (g, b, 0)),
            pl.BlockSpec((None, ncc, S5_W), lambda g, b: (g, b, 0)),
            pl.BlockSpec((None, None, S5_T, 4, S5_GROUP, S5_SW), lambda g, b: (layer, g, 0, 0, 0, 0)),
            pl.BlockSpec((None, None, ncl, 4 * S5_SW), lambda g, b: (layer, g, 0, 0)),
        ],
        out_specs=[
            pl.BlockSpec((None, ncl, 4 * S5_SW), lambda g, b: (g, b, 0)),
            pl.BlockSpec((None, ncc, 4 * S5_SW), lambda g, b: (g, b, 0)),
        ],
        out_shape=[
            jax.ShapeDtypeStruct((S5_GB, u3_lat.shape[1], 4 * S5_SW), BF16),
            jax.ShapeDtypeStruct((S5_GB, u3_ctx.shape[1], 4 * S5_SW), BF16),
        ],
        scratch_shapes=[pltpu.VMEM((S5_W, 4 * S5_SW), BF16)],
        compiler_params=_cparams(("arbitrary", "arbitrary")),
        name="s5_states",
    )(u3_lat, u3_ctx, ez, ptab)


S5_TAPS = 2 * S5_T
S5_SEL_IN = LANES
S5_SEL_OUT = (LANES // S5_GROUP) * LANES


def _s5_out_kernel(xl_ref, xc_ref, hl_ref, hc_ref, kl_ref, cl_ref, sel_ref, yl_ref, yc_ref, wt_ref, wm_ref):
    nl = xl_ref.shape[0]
    sel = sel_ref[...]
    taps = []
    mask_t = _group_mask((LANES, S5_SEL_OUT), 4, 4)
    for a in range(S5_TAPS * S5_GROUP // S5_SEL_IN):
        t = jnp.dot(kl_ref[:, a * S5_SEL_IN:(a + 1) * S5_SEL_IN], sel, preferred_element_type=F32)
        taps.append(jnp.where(mask_t, t, 0.0).astype(BF16))
    taps = jnp.concatenate(taps, axis=1)
    for s in range(S5_T):
        lo = (S5_T - 1 - s) * LANES
        wt_ref[s * LANES:(s + 1) * LANES, :] = taps[:, lo:lo + S5_W]
    mask_m = _group_mask((S5_SW, S5_SEL_OUT), 6, 4)
    for q in range(4):
        for a in range(S5_T * S5_GROUP // S5_SEL_IN):
            m = jnp.dot(cl_ref[q, :, a * S5_SEL_IN:(a + 1) * S5_SEL_IN], sel, preferred_element_type=F32)
            wm_ref[q * S5_SW:(q + 1) * S5_SW, a * S5_SEL_OUT:(a + 1) * S5_SEL_OUT] = jnp.where(
                mask_m, m, 0.0).astype(BF16)
    x = jnp.concatenate([xl_ref[...], xc_ref[...]], axis=0)
    h = jnp.concatenate([hl_ref[...], hc_ref[...]], axis=0)
    y = jnp.dot(x, wt_ref[...], preferred_element_type=F32)
    y = y + jnp.dot(h, wm_ref[...], preferred_element_type=F32)
    yl_ref[...] = y[:nl].astype(yl_ref.dtype)
    yc_ref[...] = y[nl:].astype(yc_ref.dtype)


def s5_outputs(u3_lat, u3_ctx, h_lat, h_ctx, kl, cl, sel, layer):
    nl, nc = u3_lat.shape[1], u3_ctx.shape[1]
    return pl.pallas_call(
        _s5_out_kernel,
        grid=(S5_GB,),
        in_specs=[
            pl.BlockSpec((None, nl, S5_W), lambda g: (g, 0, 0)),
            pl.BlockSpec((None, nc, S5_W), lambda g: (g, 0, 0)),
            pl.BlockSpec((None, nl, 4 * S5_SW), lambda g: (g, 0, 0)),
            pl.BlockSpec((None, nc, 4 * S5_SW), lambda g: (g, 0, 0)),
            pl.BlockSpec((None, None, LANES, S5_TAPS * S5_GROUP), lambda g: (layer, g, 0, 0)),
            pl.BlockSpec((None, 4, None, S5_SW, S5_T * S5_GROUP), lambda g: (layer, 0, g, 0, 0)),
            pl.BlockSpec((S5_SEL_IN, S5_SEL_OUT), lambda g: (0, 0)),
        ],
        out_specs=[
            pl.BlockSpec((None, nl, S5_W), lambda g: (g, 0, 0)),
            pl.BlockSpec((None, nc, S5_W), lambda g: (g, 0, 0)),
        ],
        out_shape=[
            jax.ShapeDtypeStruct((S5_GB, nl, S5_W), BF16),
            jax.ShapeDtypeStruct((S5_GB, nc, S5_W), BF16),
        ],
        scratch_shapes=[pltpu.VMEM((S5_W, S5_W), BF16), pltpu.VMEM((4 * S5_SW, S5_W), BF16)],
        compiler_params=_cparams(("arbitrary",)),
        name="s5_outputs",
    )(u3_lat, u3_ctx, h_lat, h_ctx, kl, cl, sel)


def _s5_finish_kernel(y_ref, wg_ref, zs_ref, o_ref, stage_ref):
    n_rows = y_ref.shape[1]
    for q in range(S5_GB):
        for t in range(S5_T):
            stage_ref[q, pl.ds(t, n_rows, stride=S5_T), :] = y_ref[q, :, t * LANES:(t + 1) * LANES].astype(F32)
    y = jnp.concatenate([stage_ref[q] for q in range(S5_GB)], axis=1)
    g = jax.nn.gelu(y)
    t = jnp.dot(g.astype(BF16), wg_ref[...], preferred_element_type=F32)
    zs = zs_ref[...].astype(F32)
    o_ref[...] = (g * jax.nn.sigmoid(t) * (zs * jax.nn.sigmoid(zs))).astype(o_ref.dtype)


def s5_finish(y3, w_glu, hin, zs_block):
    m = y3.shape[1] * S5_T
    tm = min(512, m)
    return pl.pallas_call(
        _s5_finish_kernel,
        grid=(m // tm,),
        in_specs=[
            pl.BlockSpec((S5_GB, tm // S5_T, S5_W), lambda i: (0, i, 0)),
            pl.BlockSpec((BRANCH, BRANCH), lambda i: (0, 0)),
            pl.BlockSpec((tm, BRANCH), lambda i: (i, zs_block)),
        ],
        out_specs=pl.BlockSpec((tm, BRANCH), lambda i: (i, 0)),
        out_shape=jax.ShapeDtypeStruct((m, BRANCH), BF16),
        scratch_shapes=[pltpu.VMEM((S5_GB, tm, LANES), F32)],
        compiler_params=_cparams(("arbitrary",)),
        name="s5_finish",
    )(y3, w_glu, hin)


def s5_prepare(a_re, a_im, log_dt, b_re, b_im, c_re, c_im, d_skip, ncl):
    T, G, P, C = S5_T, S5_GROUPS, S5_STATE, S5_GROUP
    GB, GPB, SW = S5_GB, S5_GPB, S5_SW
    hp = lax.Precision.HIGHEST
    a_re, a_im = a_re.astype(F32), a_im.astype(F32)
    dt = jnp.exp(log_dt.astype(F32))[..., None]
    la_re, la_im = a_re * dt, a_im * dt
    mag = jnp.exp(la_re)
    lb_re, lb_im = mag * jnp.cos(la_im), mag * jnp.sin(la_im)
    nr, ni = lb_re - 1.0, lb_im
    den = a_re * a_re + a_im * a_im
    f_re = (nr * a_re + ni * a_im) / den
    f_im = (ni * a_re - nr * a_im) / den
    bb_re = f_re[..., None] * b_re - f_im[..., None] * b_im
    bb_im = f_re[..., None] * b_im + f_im[..., None] * b_re

    def cpow(k, lr, li):
        m = jnp.exp(k * lr)
        return m * jnp.cos(k * li), m * jnp.sin(k * li)

    lad_re, lad_im = la_re.reshape(2, GB, 1, SW), la_im.reshape(2, GB, 1, SW)
    tt = jnp.arange(T, dtype=F32).reshape(1, 1, T, 1)
    pw_re, pw_im = cpow(tt, lad_re, lad_im)
    to_lanes = lambda w: w.reshape(2, GB, GPB, P, C).transpose(0, 1, 4, 2, 3).reshape(2, GB, C, SW)
    bt_re, bt_im = to_lanes(bb_re), to_lanes(bb_im)
    e_re = pw_re[:, :, :, None] * bt_re[:, :, None] - pw_im[:, :, :, None] * bt_im[:, :, None]
    e_im = pw_re[:, :, :, None] * bt_im[:, :, None] + pw_im[:, :, :, None] * bt_re[:, :, None]
    ez = jnp.stack([e_re[0][:, ::-1], e_im[0][:, ::-1], e_re[1], e_im[1]], axis=2)

    cr = c_re.astype(F32).reshape(2, GB, GPB, C, P)
    ci = c_im.astype(F32).reshape(2, GB, GPB, C, P)
    e6_re, e6_im = e_re.reshape(2, GB, T, C, GPB, P), e_im.reshape(2, GB, T, C, GPB, P)
    kk = (jnp.einsum("dbgop,dbtigp->dbgito", cr, e6_re, precision=hp)
          - jnp.einsum("dbgop,dbtigp->dbgito", ci, e6_im, precision=hp))
    skip = jnp.eye(C, dtype=F32) * d_skip.astype(F32).reshape(GB, GPB, C, 1)
    center = kk[0][..., 0, :] + kk[1][..., 0, :] + skip
    kl = jnp.concatenate([kk[1][..., :0:-1, :], center[..., None, :], kk[0][..., 1:, :],
                          jnp.zeros((GB, GPB, C, 1, C), F32)], axis=3)
    kl = kl.reshape(GB, LANES, S5_TAPS * C)

    lar_re, lar_im = la_re[..., None], la_im[..., None]
    steps = jnp.stack([jnp.arange(1, T + 1, dtype=F32), jnp.arange(T, 0, -1).astype(F32)])
    pr_re, pr_im = cpow(steps.reshape(2, 1, 1, T), lar_re, lar_im)
    lane = jnp.arange(T * C)
    rep_c = (jnp.arange(C)[:, None] == lane[None, :] % C).astype(F32)
    rep_t = (jnp.arange(T)[:, None] == lane[None, :] // C).astype(F32)
    cp = jnp.stack([c_re, c_im]).astype(F32).transpose(0, 1, 2, 4, 3)
    cp = jnp.einsum("rdgpc,cq->rdgpq", cp, rep_c, precision=hp)
    pr = jnp.einsum("rdgpt,tq->rdgpq", jnp.stack([pr_re, pr_im]), rep_t, precision=hp)
    cl_re = cp[0] * pr[0] - cp[1] * pr[1]
    cl_im = cp[0] * pr[1] + cp[1] * pr[0]
    cl = jnp.stack([cl_re[0], -cl_im[0], cl_re[1], -cl_im[1]], axis=0)
    cl = cl.reshape(4, GB, SW, T * C)

    kf = jnp.arange(ncl, dtype=F32).reshape(1, ncl, 1) * float(T)
    af_re, af_im = cpow(kf, lad_re[0], lad_im[0])
    ab_re, ab_im = cpow(kf[:, ::-1], lad_re[1], lad_im[1])
    tab = jnp.concatenate([af_re, af_im, ab_re, ab_im], axis=-1)
    return ez.astype(BF16), kl.astype(BF16), cl.astype(BF16), tab


def s5_select_matrix():
    r = jnp.arange(S5_SEL_IN)
    q = jnp.arange(S5_SEL_OUT)
    return ((r[:, None] // S5_GROUP == q[None, :] // LANES)
            & (r[:, None] % S5_GROUP == q[None, :] % S5_GROUP)).astype(BF16)


MXU_N = 256
LOG2E = 1.4426950408889634


def _seg_mean_sq(x, ones_bd):
    sq = (x * x).astype(BF16)
    parts = [jnp.dot(sq[:, t:t + MXU_N], ones_bd, preferred_element_type=F32)
             for t in range(0, x.shape[1], MXU_N)]
    return jnp.concatenate(parts, axis=1) * (1.0 / DA_HEAD)


def _rope(x, cos, sin_signed):
    n = x.shape[1]
    lane = lax.broadcasted_iota(jnp.int32, x.shape, 1)
    first = (lane % (DA_HEAD // 2)) < (DA_HEAD // 4)
    rot = jnp.where(first, pltpu.roll(x, n - DA_HEAD // 4, 1), pltpu.roll(x, DA_HEAD // 4, 1))
    return x * cos + rot * sin_signed


def _qk_prep_kernel(q_ref, k_ref, *rest, rope):
    if rope:
        cos_ref, sin_ref, qg_ref, kg_ref, ones_ref, o_ref = rest
    else:
        qg_ref, kg_ref, ones_ref, o_ref = rest
    ones_bd = ones_ref[...]
    reps = BRANCH // LANES
    for x_ref, g_ref, scale, c0 in ((q_ref, qg_ref, DA_HEAD ** -0.5 * LOG2E, 0), (k_ref, kg_ref, 1.0, BRANCH)):
        x = x_ref[...].astype(F32)
        x = x * lax.rsqrt(_seg_mean_sq(x, ones_bd) + EPS) * jnp.concatenate([g_ref[...]] * reps, axis=1)
        if rope:
            x = _rope(x, jnp.concatenate([cos_ref[...]] * reps, axis=1),
                      jnp.concatenate([sin_ref[...]] * reps, axis=1))
        o_ref[:, c0:c0 + BRANCH] = (x * scale).astype(o_ref.dtype)


def qk_prep(hin, q_block, k_block, qg, kg, ones_bd, rope_tabs, seq_len):
    m = hin.shape[0]
    tm = min(512, m)
    rope = rope_tabs is not None
    in_specs = [pl.BlockSpec((tm, BRANCH), lambda i: (i, q_block)),
                pl.BlockSpec((tm, BRANCH), lambda i: (i, k_block))]
    args = [hin, hin]
    if rope:
        tps = seq_len // tm
        in_specs += [pl.BlockSpec((tm, LANES), lambda i: (i % tps, 0)),
                     pl.BlockSpec((tm, LANES), lambda i: (i % tps, 0))]
        args += list(rope_tabs)
    in_specs += [pl.BlockSpec((1, LANES), lambda i: (0, 0)), pl.BlockSpec((1, LANES), lambda i: (0, 0)),
                 pl.BlockSpec((MXU_N, MXU_N), lambda i: (0, 0))]
    args += [qg, kg, ones_bd]
    return pl.pallas_call(
        functools.partial(_qk_prep_kernel, rope=rope),
        grid=(m // tm,),
        in_specs=in_specs,
        out_specs=pl.BlockSpec((tm, 2 * BRANCH), lambda i: (i, 0)),
        out_shape=jax.ShapeDtypeStruct((m, 2 * BRANCH), BF16),
        compiler_params=_cparams(("arbitrary",)),
        name="qk_prep",
    )(*args)


def _stack_maps(q):
    lane = lax.broadcasted_iota(jnp.int32, q.shape, 1)
    return jnp.concatenate([jnp.where(lane < DA_HEAD, q, 0.0), jnp.where(lane >= DA_HEAD, q, 0.0)], axis=0)


def _diff_combine(pv, tq, lam, sg, lam_scale, zd):
    o = pv[:, :DA_VDIM] / pv[:, DA_VDIM:]
    o = o[:tq] - lam * o[tq:]
    o = o * lax.rsqrt(jnp.mean(o * o, axis=-1, keepdims=True) + EPS) * sg * lam_scale
    return o * (zd * jax.nn.sigmoid(zd))


def _diff_attn_ctx_kernel(q_ref, kc_ref, vc_ref, zd_ref, sg_ref, lam_ref, o_ref, *, lam_scale):
    v1 = jnp.concatenate([vc_ref[...], jnp.ones_like(vc_ref)], axis=1)
    tq = q_ref.shape[0]
    s = lax.dot_general(_stack_maps(q_ref[...]), kc_ref[...], (((1,), (1,)), ((), ())),
                        preferred_element_type=F32)
    p = jnp.exp2(s - jnp.max(s, axis=-1, keepdims=True))
    pv = jnp.dot(p.astype(BF16), v1, preferred_element_type=F32)
    o_ref[...] = _diff_combine(pv, tq, lam_ref[...], sg_ref[...], lam_scale,
                               zd_ref[...].astype(F32)).astype(o_ref.dtype)


def _diff_attn_lat_kernel(q_ref, kl_ref, vl_ref, kc_ref, vc_ref, zd_ref, sg_ref, lam_ref, o_ref,
                          kn_ref, v1_ref, s0_ref, s1_ref, m0_ref, m1_ref, *, tq, lam_scale):
    nctx = kc_ref.shape[0]
    nq = q_ref.shape[0] // tq

    kn_ref[:, 0:nctx] = jnp.transpose(kc_ref[...].astype(F32)).astype(BF16)
    kn_ref[:, nctx:] = jnp.transpose(kl_ref[...].astype(F32)).astype(BF16)
    v1_ref[0:nctx, :] = jnp.concatenate([vc_ref[...], jnp.ones_like(vc_ref)], axis=1)
    v1_ref[nctx:, :] = jnp.concatenate([vl_ref[...], jnp.ones_like(vl_ref)], axis=1)

    bufs = ((s0_ref, m0_ref), (s1_ref, m1_ref))

    def scores(i, slot):
        s_ref, m_ref = bufs[slot]
        rows = pl.ds(pl.multiple_of(i * tq, tq), tq)
        s = jnp.dot(_stack_maps(q_ref[rows, :]), kn_ref[...], preferred_element_type=F32)
        s_ref[...] = s
        m_ref[...] = jnp.broadcast_to(jnp.max(s, axis=-1, keepdims=True), m_ref.shape)

    def finish(i, slot):
        s_ref, m_ref = bufs[slot]
        rows = pl.ds(pl.multiple_of(i * tq, tq), tq)
        p = jnp.exp2(s_ref[...] - m_ref[:, 0:1])
        pv = jnp.dot(p.astype(BF16), v1_ref[...], preferred_element_type=F32)
        o = _diff_combine(pv, tq, lam_ref[...], sg_ref[...], lam_scale, zd_ref[rows, :].astype(F32))
        o_ref[rows, :] = o.astype(o_ref.dtype)

    assert nq % 2 == 0
    scores(0, 0)

    def body(j, carry):
        scores(2 * j + 1, 1)
        finish(2 * j, 0)
        scores(2 * j + 2, 0)
        finish(2 * j + 1, 1)
        return carry

    lax.fori_loop(0, nq // 2 - 1, body, 0)
    scores(nq - 1, 1)
    finish(nq - 2, 0)
    finish(nq - 1, 1)


def diff_attention(hin_lat, hin_ctx, col_q, col_k, col_v, col_zd, batch, qg, kg, sg, lam_row, lam_init,
                   cos_t, sin_t, ones_bd):
    ml, mc = hin_lat.shape[0], hin_ctx.shape[0]
    L, nctx = ml // batch, mc // batch
    tq = min(256, L)
    qk_lat, qk_ctx = hin_lat, hin_ctx
    kcol = col_k
    small = lambda b, h: (0, 0)
    common = [pl.BlockSpec((1, LANES), small), pl.BlockSpec((1, LANES), small)]
    cargs = [sg, lam_row]
    lam_scale = 1.0 - lam_init
    nk = nctx + L
    y_lat = pl.pallas_call(
        functools.partial(_diff_attn_lat_kernel, tq=tq, lam_scale=lam_scale),
        grid=(batch, DA_HEADS),
        in_specs=[
            pl.BlockSpec((L, LANES), lambda b, h: (b, col_q + h)),
            pl.BlockSpec((L, LANES), lambda b, h: (b, kcol + h)),
            pl.BlockSpec((L, LANES), lambda b, h: (b, col_v + h)),
            pl.BlockSpec((nctx, LANES), lambda b, h: (b, kcol + h)),
            pl.BlockSpec((nctx, LANES), lambda b, h: (b, col_v + h)),
            pl.BlockSpec((L, LANES), lambda b, h: (b, col_zd + h)),
        ] + common,
        out_specs=pl.BlockSpec((L, LANES), lambda b, h: (b, h)),
        out_shape=jax.ShapeDtypeStruct((ml, BRANCH), BF16),
        scratch_shapes=[
            pltpu.VMEM((LANES, nk), BF16),
            pltpu.VMEM((nk, 2 * DA_VDIM), BF16),
            pltpu.VMEM((2 * tq, nk), F32), pltpu.VMEM((2 * tq, nk), F32),
            pltpu.VMEM((2 * tq, LANES), F32), pltpu.VMEM((2 * tq, LANES), F32),
        ],
        compiler_params=_cparams(("arbitrary", "arbitrary")),
        name="diff_attn_lat",
    )(qk_lat, qk_lat, hin_lat, qk_ctx, hin_ctx, hin_lat, *cargs)
    y_ctx = pl.pallas_call(
        functools.partial(_diff_attn_ctx_kernel, lam_scale=lam_scale),
        grid=(batch, DA_HEADS),
        in_specs=[
            pl.BlockSpec((nctx, LANES), lambda b, h: (b, col_q + h)),
            pl.BlockSpec((nctx, LANES), lambda b, h: (b, kcol + h)),
            pl.BlockSpec((nctx, LANES), lambda b, h: (b, col_v + h)),
            pl.BlockSpec((nctx, LANES), lambda b, h: (b, col_zd + h)),
        ] + common,
        out_specs=pl.BlockSpec((nctx, LANES), lambda b, h: (b, h)),
        out_shape=jax.ShapeDtypeStruct((mc, BRANCH), BF16),
        compiler_params=_cparams(("arbitrary", "arbitrary")),
        name="diff_attn_ctx",
    )(qk_ctx, qk_ctx, hin_ctx, hin_ctx, *cargs)
    return y_lat, y_ctx


def seg_ones():
    seg = jnp.arange(MXU_N) // DA_HEAD
    return (seg[:, None] == seg[None, :]).astype(BF16)


def rope_tables(n_tokens):
    rows = n_tokens // GRID_W
    row = jnp.repeat(jnp.arange(rows, dtype=F32), GRID_W)
    col = jnp.tile(jnp.arange(GRID_W, dtype=F32), rows)
    n_freq = DA_HEAD // 4
    inv_freq = ROPE_BASE ** (-jnp.arange(n_freq, dtype=F32) / n_freq)
    ang_r = row[:, None] * inv_freq
    ang_c = col[:, None] * inv_freq
    ang = jnp.concatenate([ang_r, ang_r, ang_c, ang_c], axis=-1)
    sign = jnp.tile(jnp.concatenate([-jnp.ones(n_freq, F32), jnp.ones(n_freq, F32)]), 2)
    cos = jnp.tile(jnp.cos(ang), (1, 2))
    sin_signed = jnp.tile(jnp.sin(ang) * sign, (1, 2))
    return cos, sin_signed


def _mm_out_kernel(*refs, n_parts, with_next):
    a_refs, w_refs = refs[:n_parts], refs[n_parts:2 * n_parts]
    rest = refs[2 * n_parts:]
    if with_next:
        x_ref, g_ref, gs_ref, sh_ref, o_ref, hn_ref = rest
    else:
        x_ref, g_ref, o_ref = rest
    acc = None
    for a_ref, w_ref in zip(a_refs, w_refs):
        t = jnp.dot(a_ref[...], w_ref[...], preferred_element_type=F32)
        acc = t if acc is None else acc + t
    xn = x_ref[...] + g_ref[...] * acc
    o_ref[...] = xn
    if with_next:
        hn_ref[...] = _modulate(xn, gs_ref[...], sh_ref[...]).astype(hn_ref.dtype)


def mm_out(a_parts, w_stack, layer, x, gate, next_mod, rows_per_mod):
    m = x.shape[0]
    n_parts = len(a_parts)
    kp = w_stack.shape[1] // n_parts
    tm = min(512, rows_per_mod)
    tpm = rows_per_mod // tm
    mod_spec = pl.BlockSpec((None, 1, D_MODEL), lambda i: (i // tpm, 0, 0))
    in_specs = [pl.BlockSpec((tm, kp), lambda i: (i, 0)) for _ in a_parts]
    in_specs += [pl.BlockSpec((None, kp, D_MODEL), functools.partial(lambda i, p: (layer, p, 0), p=p))
                 for p in range(n_parts)]
    in_specs += [pl.BlockSpec((tm, D_MODEL), lambda i: (i, 0)), mod_spec]
    args = [*a_parts, *([w_stack] * n_parts), x, gate]
    out_specs = [pl.BlockSpec((tm, D_MODEL), lambda i: (i, 0))]
    out_shape = [jax.ShapeDtypeStruct((m, D_MODEL), F32)]
    if next_mod is not None:
        in_specs += [mod_spec, mod_spec]
        args += list(next_mod)
        out_specs.append(pl.BlockSpec((tm, D_MODEL), lambda i: (i, 0)))
        out_shape.append(jax.ShapeDtypeStruct((m, D_MODEL), BF16))
    outs = pl.pallas_call(
        functools.partial(_mm_out_kernel, n_parts=n_parts, with_next=next_mod is not None),
        grid=(m // tm,),
        in_specs=in_specs,
        out_specs=out_specs,
        out_shape=out_shape,
        compiler_params=_cparams(("arbitrary",)),
        name="mm_out",
    )(*args)
    return (outs[0], outs[1]) if next_mod is not None else (outs[0], None)


GLA_STEP = 4


def _seg_cumsum(x, reverse):
    n = x.shape[0]
    r = lax.broadcasted_iota(jnp.int32, (MXU_N, MXU_N), 0)
    c = lax.broadcasted_iota(jnp.int32, (MXU_N, MXU_N), 1)
    same = (r // GLA_CHUNK) == (c // GLA_CHUNK)
    tri = (same & ((c >= r) if reverse else (c <= r))).astype(BF16)
    hi = x.astype(BF16)
    lo = (x - hi.astype(F32)).astype(BF16)
    parts = []
    for t in range(0, n, MXU_N):
        parts.append(jnp.dot(tri, hi[t:t + MXU_N], preferred_element_type=F32)
                     + jnp.dot(tri, lo[t:t + MXU_N], preferred_element_type=F32))
    return jnp.concatenate(parts, axis=0)


def _gla_kernel(ql_ref, kl_ref, vl_ref, rl_ref, zl_ref, qc_ref, kc_ref, vc_ref, rc_ref, zc_ref,
                wa_ref, ba_ref, ng_ref, *rest, ctx_out):
    if ctx_out:
        ol_ref, oc_ref = rest[:2]
        rest = rest[2:]
    else:
        ol_ref, oc_ref = rest[0], None
        rest = rest[1:]
    sf_ref, sb_ref, kv_ref, sin_ref, qd_ref, kd_ref, b_ref, vb_ref, ofl_ref, ofc_ref, obl_ref, obc_ref = rest
    C = GLA_CHUNK
    L, nctx = ql_ref.shape[0], kc_ref.shape[0]
    nl, nc = L // C, nctx // C
    row = lax.broadcasted_iota(jnp.int32, (C, C), 0)
    col = lax.broadcasted_iota(jnp.int32, (C, C), 1)

    for base, n, q_ref, k_ref, v_ref, r_ref in ((0, nctx, qc_ref, kc_ref, vc_ref, rc_ref),
                                                (nctx, L, ql_ref, kl_ref, vl_ref, rl_ref)):
        rows = slice(base, base + n)
        vb_ref[rows, :] = v_ref[...].astype(BF16)
        r = r_ref[...].astype(BF16)
        k = k_ref[...].astype(F32)
        with_q = ctx_out or base > 0
        for d in range(2):
            logits = jnp.dot(r, wa_ref[d], preferred_element_type=F32) + ba_ref[d]
            ls = jnp.minimum(logits, 0.0) - jnp.log(1.0 + jnp.exp(-jnp.abs(logits)))
            b = _seg_cumsum(ls * (1.0 / GLA_TAU), reverse=d == 1)
            b_ref[d, rows, :] = b
            kd_ref[d, rows, :] = (k * jnp.exp(-b)).astype(BF16)
            if with_q:
                qd_ref[d, rows, :] = (q_ref[...].astype(F32) * (GLA_DK ** -0.5) * jnp.exp(b)).astype(BF16)

    sf_ref[...] = jnp.zeros_like(sf_ref)
    sb_ref[...] = jnp.zeros_like(sb_ref)

    def segment(base, n_chunks, of_ref, ob_ref):
        step = min(GLA_STEP, n_chunks)
        assert n_chunks % step == 0

        def body(i, carry):
            work = []
            for u in range(step):
                cf = i * step + u
                cb = n_chunks - 1 - cf
                work.append((0, u, pl.multiple_of(base + cf * C, C), pl.multiple_of(cf * C, C)))
                work.append((1, step + u, pl.multiple_of(base + cb * C, C), pl.multiple_of(cb * C, C)))
            outs = {0: of_ref, 1: ob_ref}
            with_out = of_ref is not None
            scores, decs = {}, {}
            if with_out:
                for d, slot, row0, _ in work:
                    rows = pl.ds(row0, C)
                    scores[slot] = lax.dot_general(qd_ref[d, rows, :], kd_ref[d, rows, :],
                                                   (((1,), (1,)), ((), ())), preferred_element_type=F32)
            for d, slot, row0, _ in work:
                rows = pl.ds(row0, C)
                dec = jnp.exp(b_ref[d, pl.ds(row0 if d == 1 else row0 + C - 1, 1), :])
                k_end = (kd_ref[d, rows, :].astype(F32) * dec).astype(BF16)
                kv_ref[slot] = lax.dot_general(k_end, vb_ref[rows, :], (((0,), (0,)), ((), ())),
                                               preferred_element_type=F32)
                decs[slot] = jnp.transpose(jnp.broadcast_to(dec, (LANES, GLA_DK)))[:, 0:1]
            if with_out:
                for d, slot, row0, orow0 in work:
                    incl = (col >= row) if d == 1 else (col <= row)
                    sc = jnp.where(incl, scores[slot], 0.0).astype(BF16)
                    outs[d][pl.ds(orow0, C), :] = jnp.dot(sc, vb_ref[pl.ds(row0, C), :],
                                                          preferred_element_type=F32)
            for d, slot, _, _ in work:
                s_ref = sb_ref if d == 1 else sf_ref
                s_old = s_ref[...]
                sin_ref[slot] = s_old.astype(BF16)
                s_ref[...] = decs[slot] * s_old + kv_ref[slot]
            if with_out:
                for d, slot, row0, orow0 in work:
                    outs[d][pl.ds(orow0, C), :] += jnp.dot(qd_ref[d, pl.ds(row0, C), :], sin_ref[slot],
                                                           preferred_element_type=F32)
            return carry

        lax.fori_loop(0, n_chunks // step, body, 0)

    segment(0, nc, ofc_ref if ctx_out else None, obc_ref if ctx_out else None)
    segment(nctx, nl, ofl_ref, obl_ref)
    def finish(of_ref, ob_ref, z_ref, o_ref):
        o = of_ref[...] + ob_ref[...]
        z = z_ref[...].astype(F32)
        y = o * lax.rsqrt(jnp.mean(o * o, axis=-1, keepdims=True) + EPS) * ng_ref[...]
        o_ref[...] = (y * (z * jax.nn.sigmoid(z))).astype(o_ref.dtype)

    finish(ofl_ref, obl_ref, zl_ref, ol_ref)
    if ctx_out:
        finish(ofc_ref, obc_ref, zc_ref, oc_ref)


def gla_mix(hin_lat, hin_ctx, r_lat, r_ctx, wa2p, ba, norm_g, batch, ctx_out):
    ml, mc = hin_lat.shape[0], hin_ctx.shape[0]
    L, nctx = ml // batch, mc // batch
    kb = GLA_KEY // GLA_DK
    vb = 2 * GLA_KEY // GLA_DV
    zb = (2 * GLA_KEY + GLA_VAL) // GLA_DV

    def seg_specs(n, kv_only):
        qo, ko, vo, zo = (0, 0, GLA_KEY // GLA_DV, GLA_KEY // GLA_DV) if kv_only else (0, kb, vb, zb)
        return [pl.BlockSpec((n, GLA_DK), lambda b, h: (b, qo + h)),
                pl.BlockSpec((n, GLA_DK), lambda b, h: (b, ko + h)),
                pl.BlockSpec((n, GLA_DV), lambda b, h: (b, vo + h)),
                pl.BlockSpec((n, LANES), lambda b, h: (b, 0)),
                pl.BlockSpec((n, GLA_DV), lambda b, h: (b, zo + h))]

    in_specs = seg_specs(L, False) + seg_specs(nctx, not ctx_out) + [
        pl.BlockSpec((2, LANES, GLA_DK), lambda b, h: (0, 0, h)),
        pl.BlockSpec((2, 1, GLA_DK), lambda b, h: (0, 0, h)),
        pl.BlockSpec((1, GLA_DV), lambda b, h: (0, 0)),
    ]
    out_specs = [pl.BlockSpec((L, GLA_DV), lambda b, h: (b, h)),
                 pl.BlockSpec((nctx, GLA_DV), lambda b, h: (b, h))]
    out_shape = [jax.ShapeDtypeStruct((ml, GLA_VAL), BF16), jax.ShapeDtypeStruct((mc, GLA_VAL), BF16)]
    if not ctx_out:
        out_specs, out_shape = out_specs[:1], out_shape[:1]
    outs = pl.pallas_call(
        functools.partial(_gla_kernel, ctx_out=ctx_out),
        grid=(batch, GLA_HEADS),
        in_specs=in_specs,
        out_specs=out_specs,
        out_shape=out_shape,
        scratch_shapes=[
            pltpu.VMEM((GLA_DK, GLA_DV), F32), pltpu.VMEM((GLA_DK, GLA_DV), F32),
            pltpu.VMEM((2 * GLA_STEP, GLA_DK, GLA_DV), F32),
            pltpu.VMEM((2 * GLA_STEP, GLA_DK, GLA_DV), BF16),
            pltpu.VMEM((2, nctx + L, GLA_DK), BF16),
            pltpu.VMEM((2, nctx + L, GLA_DK), BF16),
            pltpu.VMEM((2, nctx + L, GLA_DK), F32),
            pltpu.VMEM((nctx + L, GLA_DV), BF16),
            pltpu.VMEM((L, GLA_DV), F32), pltpu.VMEM((nctx, GLA_DV), F32),
            pltpu.VMEM((L, GLA_DV), F32), pltpu.VMEM((nctx, GLA_DV), F32),
        ],
        compiler_params=_cparams(("arbitrary", "arbitrary")),
        name="gla_mix",
    )(hin_lat, hin_lat, hin_lat, r_lat, hin_lat, hin_ctx, hin_ctx, hin_ctx, r_ctx, hin_ctx, wa2p, ba, norm_g)
    return (outs[0], outs[1]) if ctx_out else (outs[0], None)


def _mods(mod_l, norm_g, batch):
    shift, scale, gate = mod_l[:, :D_MODEL], mod_l[:, D_MODEL:2 * D_MODEL], mod_l[:, 2 * D_MODEL:]
    gs = (norm_g.astype(F32)[None, :] * (1.0 + scale))[:, None, :]
    sh = shift[:, None, :]
    gt = gate[:, None, :]
    lat = (gs[:batch], sh[:batch], gt[:batch])
    ctx = (gs[batch:batch + 1], sh[batch:batch + 1], gt[batch:batch + 1])
    return lat, ctx


def even_layer(x_lat, x_ctx, hn_l, hn_c, gates, next_mods, w_in, w_out, j, s5p, w_glu, qn_g, kn_g,
               lam_vecs, subln_g, lam_init, batch, rope):
    L = x_lat.shape[0] // batch
    gt_l, gt_c = gates
    rest = 5 * BRANCH
    u3_l = mm_in(hn_l, w_in, j, 0, BRANCH, out_dtype=BF16, gb_out=True, name="mm_in_u")
    u3_c = mm_in(hn_c, w_in, j, 0, BRANCH, out_dtype=BF16, gb_out=True, name="mm_in_u_ctx")
    qg = jnp.tile(qn_g.astype(F32), 2)[None, :]
    kg = jnp.tile(kn_g.astype(F32), 2)[None, :]
    cos_t, sin_t, ones_bd = rope
    hin_l = mm_in_qk(hn_l, w_in, j, BRANCH, rest, BRANCH, 2 * BRANCH, qg, kg, ones_bd, (cos_t, sin_t),
                     name="mm_in_rest")
    hin_c = mm_in_qk(hn_c, w_in, j, BRANCH, rest, BRANCH, 2 * BRANCH, qg, kg, ones_bd, None,
                     name="mm_in_rest_ctx")

    ez, kl, cl, ptab, sel = s5p
    h_l, h_c = s5_states(u3_l, u3_c, ez, ptab, j, batch)
    y_l, y_c = s5_outputs(u3_l, u3_c, h_l, h_c, kl, cl, sel, j)
    wg = w_glu.astype(BF16)
    a_s5_l = s5_finish(y_l, wg, hin_l, 0)
    a_s5_c = s5_finish(y_c, wg, hin_c, 0)

    lv = lam_vecs.astype(F32)
    lam = jnp.exp(jnp.sum(lv[0] * lv[1])) - jnp.exp(jnp.sum(lv[2] * lv[3])) + lam_init
    lam_row = jnp.full((1, LANES), lam, F32)
    qg = jnp.tile(qn_g.astype(F32), 2)[None, :]
    kg = jnp.tile(kn_g.astype(F32), 2)[None, :]
    sg = subln_g.astype(F32)[None, :]
    cb = BRANCH // LANES
    cos_t, sin_t, ones_bd = rope
    a_da_l, a_da_c = diff_attention(hin_l, hin_c, cb, 2 * cb, 3 * cb, 4 * cb, batch, qg, kg, sg, lam_row,
                                    lam_init, cos_t, sin_t, ones_bd)

    nm_l, nm_c = next_mods
    x_lat, hn_l = mm_out([a_s5_l, a_da_l], w_out, j, x_lat, gt_l, nm_l, L)
    x_ctx, hn_c = mm_out([a_s5_c, a_da_c], w_out, j, x_ctx, gt_c, nm_c, x_ctx.shape[0])
    return x_lat, x_ctx, hn_l, hn_c


def odd_layer(x_lat, x_ctx, hn_l, hn_c, gates, next_mods, w_in, w_out, j, wa1, wa2, ba, gla_norm_g, batch,
              with_ctx_out):
    L = x_lat.shape[0] // batch
    gt_l, gt_c = gates
    n = 3 * D_MODEL
    w_aux = jnp.zeros((D_MODEL, LANES), F32).at[:, :GLA_RANK].set(wa1[0]).at[:, GLA_RANK:2 * GLA_RANK].set(wa1[1])
    w_aux = w_aux.astype(BF16)
    hin_l, r_l = mm_in(hn_l, w_in, j, 0, n, out_dtype=BF16, w_aux=w_aux, name="mm_in_odd")
    if with_ctx_out:
        hin_c, r_c = mm_in(hn_c, w_in, j, 0, n, out_dtype=BF16, w_aux=w_aux, name="mm_in_odd_ctx")
    else:
        hin_c, r_c = mm_in(hn_c, w_in, j, GLA_KEY, GLA_KEY + GLA_VAL, out_dtype=BF16, w_aux=w_aux,
                           name="mm_in_odd_ctx_kv")
    wa2p = jnp.zeros((2, LANES, GLA_KEY), F32)
    wa2p = wa2p.at[0, :GLA_RANK].set(wa2[0]).at[1, GLA_RANK:2 * GLA_RANK].set(wa2[1]).astype(BF16)
    ng = gla_norm_g.astype(F32)[None, :]
    a_l, a_c = gla_mix(hin_l, hin_c, r_l, r_c, wa2p, ba.astype(F32).reshape(2, 1, GLA_KEY), ng, batch,
                       with_ctx_out)
    nm_l, nm_c = next_mods
    x_lat, hn_l = mm_out([a_l], w_out, j, x_lat, gt_l, nm_l, L)
    if with_ctx_out:
        x_ctx, hn_c = mm_out([a_c], w_out, j, x_ctx, gt_c, nm_c, x_ctx.shape[0])
    return x_lat, x_ctx, hn_l, hn_c


def kernel(x, c, ctx, c_ctx, ada_w, ada_b, norm_g, ev_w_in, ev_w_out, s5_a_re, s5_a_im, s5_log_dt, s5_b_re, s5_b_im, s5_c_re, s5_c_im, s5_d, s5_w_glu, da_qn_g, da_kn_g, da_lam, da_subln_g, od_w_in, od_w_out, gla_wa1, gla_wa2, gla_ba, gla_norm_g):
    batch, L, _ = x.shape
    nctx = ctx.shape[1]
    x_lat = x.reshape(batch * L, D_MODEL)
    x_ctx = ctx.reshape(batch * nctx, D_MODEL)
    cond = jnp.zeros((8, D_MODEL), F32).at[:batch].set(c).at[batch].set(c_ctx)
    mod = modulation_all(cond, ada_w, ada_b)

    cos_t, sin_t = rope_tables(L)
    rope = (cos_t, sin_t, seg_ones())
    mods = [_mods(mod[i], norm_g[i], batch) for i in range(DEPTH)]
    ev_w_out_b, od_w_out_b = ev_w_out.astype(BF16), od_w_out.astype(BF16)

    s5p = jax.vmap(functools.partial(s5_prepare, ncl=L // S5_T))(
        s5_a_re, s5_a_im, s5_log_dt, s5_b_re, s5_b_im, s5_c_re, s5_c_im, s5_d) + (s5_select_matrix(),)

    (gs_l, sh_l, _), (gs_c, sh_c, _) = mods[0]
    hn_l = prenorm(x_lat, gs_l, sh_l, L)
    hn_c = prenorm(x_ctx, gs_c, sh_c, x_ctx.shape[0])
    for i in range(DEPTH):
        j = i // 2
        with_ctx_out = i < DEPTH - 1
        gates = (mods[i][0][2], mods[i][1][2])
        next_mods = (mods[i + 1][0][:2], mods[i + 1][1][:2]) if with_ctx_out else (None, None)
        if i % 2 == 0:
            lam_init = 0.8 - 0.6 * math.exp(-0.3 * i)
            x_lat, x_ctx, hn_l, hn_c = even_layer(
                x_lat, x_ctx, hn_l, hn_c, gates, next_mods, ev_w_in, ev_w_out_b, j, s5p, s5_w_glu[j],
                da_qn_g[j], da_kn_g[j], da_lam[j], da_subln_g[j], lam_init, batch, rope)
        else:
            x_lat, x_ctx, hn_l, hn_c = odd_layer(
                x_lat, x_ctx, hn_l, hn_c, gates, next_mods, od_w_in, od_w_out_b, j, gla_wa1[j], gla_wa2[j],
                gla_ba[j], gla_norm_g[j], batch, with_ctx_out)
    return x_lat.reshape(batch, L, D_MODEL)
```
